```python
import math
import jax, jax.numpy as jnp
from jax import lax
import numpy as np

D_MODEL = 1024
BATCH = 1
SEQ = 16384
DEPTH = 4

GRID_W = 64
CTX_LEN = 256
N_MIXERS = 4
EPS = 1e-6
NEG_INF = -1e30
CHUNK = 64

HG_HEADS = 8
HG_DK = 128
HG_DV = D_MODEL // HG_HEADS
HG_IN = 3 * HG_HEADS * HG_DK + HG_HEADS * HG_DV + D_MODEL

SSM_D_INNER = 2 * D_MODEL
SSM_HEADDIM = 64
SSM_HEADS = SSM_D_INNER // SSM_HEADDIM
SSM_GROUPS = 4
SSM_STATE = 128
SSM_CONV = 5
SSM_CONV_DIM = SSM_D_INNER + 2 * SSM_GROUPS * SSM_STATE
SSM_IN_DIM = SSM_D_INNER + SSM_CONV_DIM + 2 * SSM_HEADS

POOL_WINDOWS = (2, 4, 8, 16)
POOL_GROUP = D_MODEL // len(POOL_WINDOWS)

ATT_HEAD_DIM = 64
ATT_Q_HEADS = D_MODEL // ATT_HEAD_DIM
ATT_KV_HEADS = 4
ATT_WINDOW = 128
ATT_BLOCK = 128
ROPE_THETA = 10000.0

FFN_HIDDEN = -(-8 * D_MODEL // (3 * 256)) * 256

N_OCC = tuple((DEPTH - m + N_MIXERS - 1) // N_MIXERS for m in range(N_MIXERS))

kernel_name = "hybrid_interleaved_dit_prefix_ctx"


def rmsnorm(x, w):
    xf = x.astype(jnp.float32)
    y = xf * lax.rsqrt(jnp.mean(xf * xf, axis=-1, keepdims=True) + EPS)
    return (y * w.astype(jnp.float32)).astype(x.dtype)


def swiglu(h, w_in, w_out):
    gate, up = jnp.split(h @ w_in, 2, axis=-1)
    return (jax.nn.silu(gate) * up) @ w_out


def flip(t):
    return jnp.flip(t, axis=1)


def gla_chunk_scan(q, k, v, log_f, s0):
    f32 = jnp.float32
    b, n, h, dk = q.shape
    dv = v.shape[-1]
    nc = n // CHUNK
    qc, kc, lf = (t.astype(f32).reshape(b, nc, CHUNK, h, dk) for t in (q, k, log_f))
    vc = v.astype(f32).reshape(b, nc, CHUNK, h, dv)
    cum = jnp.cumsum(lf, axis=2)
    ref = cum[:, :, CHUNK // 2 - 1:CHUNK // 2]
    qr = qc * jnp.exp(cum - ref)
    kr = kc * jnp.exp(ref - cum)
    causal = jnp.tril(jnp.ones((CHUNK, CHUNK), bool))
    att = jnp.where(causal, jnp.einsum('bclhd,bcshd->bchls', qr, kr), 0.0)
    o_intra = jnp.einsum('bchls,bcshv->bclhv', att, vc)
    last = cum[:, :, -1:]
    ds = jnp.einsum('bcshd,bcshv->bchdv', kc * jnp.exp(last - cum), vc)
    decay = jnp.exp(last[:, :, 0])

    def step(s, inp):
        g, d = inp
        return g[..., None] * s + d, s

    s_fin, s_in = lax.scan(step, s0.astype(f32), (jnp.moveaxis(decay, 1, 0), jnp.moveaxis(ds, 1, 0)))
    s_in = jnp.moveaxis(s_in, 0, 1)
    o_inter = jnp.einsum('bclhd,bchdv->bclhv', qc * jnp.exp(cum), s_in)
    o = (o_intra + o_inter).reshape(b, n, h, dv)
    return o.astype(v.dtype), s_fin


def hgrn2_mixer(h_lat, h_ctx, w_in, lb_f, lb_b, gn_w, w_out, need_ctx):
    f32 = jnp.float32
    hk = HG_HEADS * HG_DK

    def prep(h):
        bsz, m, _ = h.shape
        q, zf, zb, v, g = jnp.split(h @ w_in, [hk, 2 * hk, 3 * hk, 3 * hk + HG_HEADS * HG_DV], axis=-1)

        def gate(z, lb):
            z32 = z.astype(f32)
            log_f = jnp.log(lb + (1.0 - lb) * jax.nn.sigmoid(z32))
            k = (1.0 - lb) * jax.nn.sigmoid(-z32)
            return k.reshape(bsz, m, HG_HEADS, HG_DK), log_f.reshape(bsz, m, HG_HEADS, HG_DK)

        kf, lff = gate(zf, lb_f)
        kb, lfb = gate(zb, lb_b)
        q = jax.nn.silu(q).reshape(bsz, m, HG_HEADS, HG_DK)
        v = v.reshape(bsz, m, HG_HEADS, HG_DV)
        return q, kf, lff, kb, lfb, v, g

    def readout(o, g):
        bsz, m = o.shape[:2]
        o = rmsnorm(o, gn_w).reshape(bsz, m, HG_HEADS * HG_DV).astype(g.dtype)
        return (o * jax.nn.silu(g)) @ w_out

    bsz = h_lat.shape[0]
    s0 = jnp.zeros((bsz, HG_HEADS, HG_DK, HG_DV), f32)
    qc, kfc, lfc, kbc, lbc, vc, gc = prep(h_ctx)
    oc_f, sc_f = gla_chunk_scan(qc, kfc, vc, lfc, s0)
    oc_b, sc_b = gla_chunk_scan(flip(qc), flip(kbc), flip(vc), flip(lbc), s0)
    ql, kfl, lfl, kbl, lbl, vl, gl = prep(h_lat)
    ol_f, _ = gla_chunk_scan(ql, kfl, vl, lfl, sc_f)
    ol_b, _ = gla_chunk_scan(flip(ql), flip(kbl), flip(vl), flip(lbl), sc_b)
    y_lat = readout(ol_f + flip(ol_b), gl)
    y_ctx = readout(oc_f + flip(oc_b), gc) if need_ctx else None
    return y_lat, y_ctx


def ssd_chunk_scan(x, dt, a, bm, cm, s0):
    f32 = jnp.float32
    b, n, h, p = x.shape
    g, ns = bm.shape[2], bm.shape[3]
    hpg = h // g
    nc = n // CHUNK
    xc = x.astype(f32).reshape(b, nc, CHUNK, g, hpg, p)
    dtc = dt.astype(f32).reshape(b, nc, CHUNK, g, hpg)
    bc = bm.astype(f32).reshape(b, nc, CHUNK, g, ns)
    cc = cm.astype(f32).reshape(b, nc, CHUNK, g, ns)
    cum = jnp.cumsum(dtc * a.reshape(g, hpg), axis=2)
    seg = cum[:, :, :, None] - cum[:, :, None]
    causal = jnp.tril(jnp.ones((CHUNK, CHUNK), bool))[:, :, None, None]
    lmat = jnp.exp(jnp.where(causal, seg, -jnp.inf))
    cb = jnp.einsum('bctgn,bcsgn->bctsg', cc, bc)
    xdt = xc * dtc[..., None]
    y_diag = jnp.einsum('bctsgh,bcsghp->bctghp', cb[..., None] * lmat, xdt)
    last = cum[:, :, -1:]
    ds = jnp.einsum('bcsgn,bcsgh,bcsghp->bcghpn', bc, jnp.exp(last - cum), xdt)
    decay = jnp.exp(last[:, :, 0])

    def step(s, inp):
        d, u = inp
        return d[..., None, None] * s + u, s

    s_fin, s_in = lax.scan(step, s0.astype(f32), (jnp.moveaxis(decay, 1, 0), jnp.moveaxis(ds, 1, 0)))
    s_in = jnp.moveaxis(s_in, 0, 1)
    y_off = jnp.einsum('bctgn,bcghpn,bctgh->bctghp', cc, s_in, jnp.exp(cum))
    y = (y_diag + y_off).reshape(b, n, h, p)
    return y.astype(x.dtype), s_fin


def centred_depthwise_conv(x, w, bias):
    width, ch = w.shape
    half = width // 2
    y = lax.conv_general_dilated(x, w[:, None, :].astype(x.dtype), window_strides=(1,),
                                 padding=[(half, width - 1 - half)],
                                 dimension_numbers=('NWC', 'WIO', 'NWC'), feature_group_count=ch)
    return y + bias


def mamba2_mixer(h_lat, h_ctx, w_in, conv_w, conv_b, dt_bias, a_log, d_skip, norm_w, w_out, need_ctx):
    f32 = jnp.float32
    gn = SSM_GROUPS * SSM_STATE
    a = -jnp.exp(a_log.astype(f32))

    def prep(h):
        bsz, m, _ = h.shape
        z, xbc, dt = jnp.split(h @ w_in, [SSM_D_INNER, SSM_D_INNER + SSM_CONV_DIM], axis=-1)
        xbc = jax.nn.silu(centred_depthwise_conv(xbc, conv_w, conv_b))
        xs, bm, cm = jnp.split(xbc, [SSM_D_INNER, SSM_D_INNER + gn], axis=-1)
        xs = xs.reshape(bsz, m, SSM_HEADS, SSM_HEADDIM)
        bm = bm.reshape(bsz, m, SSM_GROUPS, SSM_STATE)
        cm = cm.reshape(bsz, m, SSM_GROUPS, SSM_STATE)
        dt = jax.nn.softplus(dt.astype(f32).reshape(bsz, m, 2, SSM_HEADS) + dt_bias.astype(f32))
        return z, xs, bm, cm, dt

    def readout(y_f, y_b, xs, z):
        bsz, m = xs.shape[:2]
        y = (y_f + y_b + d_skip[:, None] * xs).reshape(bsz, m, SSM_D_INNER) * jax.nn.silu(z)
        y = rmsnorm(y.reshape(bsz, m, SSM_GROUPS, SSM_D_INNER // SSM_GROUPS),
                    norm_w.reshape(SSM_GROUPS, -1)).reshape(bsz, m, SSM_D_INNER)
        return y @ w_out

    bsz = h_lat.shape[0]
    s0 = jnp.zeros((bsz, SSM_GROUPS, SSM_HEADS // SSM_GROUPS, SSM_HEADDIM, SSM_STATE), f32)
    zc, xc, bc, cc, dtc = prep(h_ctx)
    yc_f, sc_f = ssd_chunk_scan(xc, dtc[:, :, 0], a[0], bc, cc, s0)
    yc_b, sc_b = ssd_chunk_scan(flip(xc), flip(dtc[:, :, 1]), a[1], flip(bc), flip(cc), s0)
    zl, xl, bl, cl, dtl = prep(h_lat)
    yl_f, _ = ssd_chunk_scan(xl, dtl[:, :, 0], a[0], bl, cl, sc_f)
    yl_b, _ = ssd_chunk_scan(flip(xl), flip(dtl[:, :, 1]), a[1], flip(bl), flip(cl), sc_b)
    y_lat = readout(yl_f, flip(yl_b), xl, zl)
    y_ctx = readout(yc_f, flip(yc_b), xc, zc) if need_ctx else None
    return y_lat, y_ctx


def pool_mixer(h, w_grp, ch_scale):
    f32 = jnp.float32
    bsz, m, _ = h.shape
    hf = h.astype(f32)
    cs = jnp.concatenate([jnp.zeros((bsz, 1, D_MODEL), f32), jnp.cumsum(hf, axis=1)], axis=1)
    t = jnp.arange(m)
    outs = []
    for gi, win in enumerate(POOL_WINDOWS):
        lo = jnp.clip(t - win // 2, 0, m)
        hi = jnp.clip(t + win - win // 2, 0, m)
        sl = slice(gi * POOL_GROUP, (gi + 1) * POOL_GROUP)
        cg = cs[:, :, sl]
        mean = (cg[:, hi] - cg[:, lo]) / (hi - lo).astype(f32)[None, :, None]
        delta = (mean - hf[:, :, sl]).astype(h.dtype)
        outs.append(delta @ w_grp[gi])
    return jnp.concatenate(outs, axis=-1) * ch_scale


def axial_rope_tables(row_ids, col_ids):
    f32 = jnp.float32
    half = ATT_HEAD_DIM // 2
    inv = ROPE_THETA ** (-jnp.arange(0, half, 2, dtype=f32) / half)

    def tab(pos):
        ang = pos.astype(f32)[:, None] * inv[None]
        ang = jnp.concatenate([ang, ang], axis=-1)
        return jnp.cos(ang), jnp.sin(ang)

    cr, sr = tab(row_ids)
    cc, sc = tab(col_ids)
    return jnp.concatenate([cr, cc], -1), jnp.concatenate([sr, sc], -1)


def rotate_half(x):
    x1, x2 = jnp.split(x, 2, axis=-1)
    return jnp.concatenate([-x2, x1], axis=-1)


def apply_axial_rope(x, cos, sin):
    xr, xc = jnp.split(x, 2, axis=-1)
    rot = jnp.concatenate([rotate_half(xr), rotate_half(xc)], axis=-1)
    return (x * cos + rot * sin).astype(x.dtype)


def softmax_with_sink(scores, sink):
    s = jnp.concatenate([scores, jnp.broadcast_to(sink, scores.shape[:-1] + (1,))], axis=-1)
    return jax.nn.softmax(s, axis=-1)[..., :-1]


def window_gqa_mixer(h_lat, h_ctx, w_qkv, sink, w_out, cos, sin, need_ctx):
    f32 = jnp.float32
    grp = ATT_Q_HEADS // ATT_KV_HEADS
    qd, kd = ATT_Q_HEADS * ATT_HEAD_DIM, ATT_KV_HEADS * ATT_HEAD_DIM
    scale = ATT_HEAD_DIM ** -0.5

    def proj(h):
        bsz, m, _ = h.shape
        q, k, v = jnp.split(h @ w_qkv, [qd, qd + kd], axis=-1)
        return (q.reshape(bsz, m, ATT_KV_HEADS, grp, ATT_HEAD_DIM),
                k.reshape(bsz, m, ATT_KV_HEADS, ATT_HEAD_DIM),
                v.reshape(bsz, m, ATT_KV_HEADS, ATT_HEAD_DIM))

    bsz, n, _ = h_lat.shape
    ql, kl, vl = proj(h_lat)
    qc, kc, vc = proj(h_ctx)
    ql = apply_axial_rope(ql, cos[:, None, None], sin[:, None, None])
    kl = apply_axial_rope(kl, cos[:, None], sin[:, None])
    sink_l = sink.astype(f32).reshape(ATT_KV_HEADS, grp)[None, :, :, None, None]

    span = ATT_BLOCK + 2 * ATT_WINDOW
    pad = ((0, 0), (ATT_WINDOW, ATT_WINDOW), (0, 0), (0, 0))
    kp = jnp.pad(kl, pad)
    vp = jnp.pad(vl, pad)
    k_rel = jnp.arange(span) - ATT_WINDOW
    band = jnp.abs(k_rel[None, :] - jnp.arange(ATT_BLOCK)[:, None]) <= ATT_WINDOW

    def block(bi):
        start = bi * ATT_BLOCK
        qb = lax.dynamic_slice_in_dim(ql, start, ATT_BLOCK, axis=1)
        kb = lax.dynamic_slice_in_dim(kp, start, span, axis=1)
        vb = lax.dynamic_slice_in_dim(vp, start, span, axis=1)
        kpos = start + k_rel
        valid = band & ((kpos >= 0) & (kpos < n))[None, :]
        s_lat = jnp.where(valid, jnp.einsum('bqkgd,bmkd->bkgqm', qb, kb).astype(f32) * scale, NEG_INF)
        s_ctx = jnp.einsum('bqkgd,bckd->bkgqc', qb, kc).astype(f32) * scale
        p = softmax_with_sink(jnp.concatenate([s_lat, s_ctx], axis=-1), sink_l).astype(vl.dtype)
        return (jnp.einsum('bkgqm,bmkd->bqkgd', p[..., :span], vb)
                + jnp.einsum('bkgqc,bckd->bqkgd', p[..., span:], vc))

    o = lax.map(block, jnp.arange(n // ATT_BLOCK))
    y_lat = jnp.moveaxis(o, 0, 1).reshape(bsz, n, D_MODEL) @ w_out
    y_ctx = None
    if need_ctx:
        m = h_ctx.shape[1]
        s = jnp.einsum('bqkgd,bckd->bkgqc', qc, kc).astype(f32) * scale
        p = softmax_with_sink(s, sink_l).astype(vc.dtype)
        y_ctx = jnp.einsum('bkgqc,bckd->bqkgd', p, vc).reshape(bsz, m, D_MODEL) @ w_out
    return y_lat, y_ctx


def setup_inputs(seed: int = 0) -> dict:
    key = jax.random.key(seed)
    ks = iter(jax.random.split(key, 40))
    f32 = jnp.float32
    D = D_MODEL
    n_a, n_b, n_c, n_d = N_OCC

    def nrm(shape, s):
        return jax.random.normal(next(ks), shape, f32) * s

    def unif(shape, lo, hi):
        return jax.random.uniform(next(ks), shape, f32, lo, hi)

    dt0 = jnp.exp(unif((n_b, 2, SSM_HEADS), math.log(1e-3), math.log(1e-1)))
    return {
        "x": nrm((BATCH, SEQ, D), 1.0),
        "c": nrm((BATCH, D), 1.0),
        "ctx": nrm((BATCH, CTX_LEN, D), 1.0),
        "c_ctx": nrm((D,), 1.0),
        "ada_w": nrm((DEPTH, D, 6 * D), 0.5 * D ** -0.5),
        "ada_b": nrm((DEPTH, 6 * D), 0.02),
        "norm_w": 1.0 + nrm((DEPTH, 2, D), 0.02),
        "ffn_w_in": nrm((DEPTH, D, 2 * FFN_HIDDEN), D ** -0.5),
        "ffn_w_out": nrm((DEPTH, FFN_HIDDEN, D), FFN_HIDDEN ** -0.5),
        "final_norm_w": 1.0 + nrm((D,), 0.02),
        "hg_w_in": nrm((n_a, D, HG_IN), D ** -0.5),
        "hg_lb": nrm((2, DEPTH + 1, HG_HEADS * HG_DK), 0.1),
        "hg_norm_w": 1.0 + nrm((n_a, HG_DV), 0.02),
        "hg_w_out": nrm((n_a, HG_HEADS * HG_DV, D), (HG_HEADS * HG_DV) ** -0.5),
        "ssm_w_in": nrm((n_b, D, SSM_IN_DIM), D ** -0.5),
        "ssm_conv_w": nrm((n_b, SSM_CONV, SSM_CONV_DIM), SSM_CONV ** -0.5),
        "ssm_conv_b": nrm((n_b, SSM_CONV_DIM), 0.02),
        "ssm_dt_bias": dt0 + jnp.log(-jnp.expm1(-dt0)),
        "ssm_a_log": jnp.log(unif((n_b, 2, SSM_HEADS), 1.0, 16.0)),
        "ssm_d": 1.0 + nrm((n_b, SSM_HEADS), 0.02),
        "ssm_norm_w": 1.0 + nrm((n_b, SSM_D_INNER), 0.02),
        "ssm_w_out": nrm((n_b, SSM_D_INNER, D), SSM_D_INNER ** -0.5),
        "pool_w": nrm((n_c, len(POOL_WINDOWS), POOL_GROUP, POOL_GROUP), POOL_GROUP ** -0.5),
        "pool_scale": 1.0 + nrm((n_c, D), 0.02),
        "att_w_qkv": nrm((n_d, D, (ATT_Q_HEADS + 2 * ATT_KV_HEADS) * ATT_HEAD_DIM), D ** -0.5),
        "att_sink": nrm((n_d, ATT_Q_HEADS), 0.5),
        "att_w_out": nrm((n_d, D, D), D ** -0.5),
    }


def reference(x, c, ctx, c_ctx, ada_w, ada_b, norm_w, ffn_w_in, ffn_w_out, final_norm_w,
              hg_w_in, hg_lb, hg_norm_w, hg_w_out,
              ssm_w_in, ssm_conv_w, ssm_conv_b, ssm_dt_bias, ssm_a_log, ssm_d, ssm_norm_w, ssm_w_out,
              pool_w, pool_scale, att_w_qkv, att_sink, att_w_out):
    f32 = jnp.float32
    bsz, n, _ = x.shape
    rows = n // GRID_W
    row_ids = jnp.repeat(jnp.arange(rows), GRID_W)
    col_ids = jnp.tile(jnp.arange(GRID_W), rows)
    cos, sin = axial_rope_tables(row_ids, col_ids)

    lb_tab = jnp.cumsum(jax.nn.softmax(hg_lb.astype(f32), axis=1), axis=1)

    silu_c = jax.nn.silu(c)
    silu_cc = jax.nn.silu(c_ctx)[None]
    x_lat, x_ctx = x, ctx
    for i in range(DEPTH):
        kind, j = i % N_MIXERS, i // N_MIXERS
        need_ctx = i < DEPTH - 1
        mod_l = jnp.split((silu_c @ ada_w[i] + ada_b[i])[:, None], 6, axis=-1)
        mod_c = jnp.split((silu_cc @ ada_w[i] + ada_b[i])[:, None], 6, axis=-1)
        sh_l, sc_l, g_l, sh2_l, sc2_l, g2_l = mod_l
        sh_c, sc_c, g_c, sh2_c, sc2_c, g2_c = mod_c

        h_lat = rmsnorm(x_lat, norm_w[i, 0]) * (1 + sc_l) + sh_l
        h_ctx = rmsnorm(x_ctx, norm_w[i, 0]) * (1 + sc_c) + sh_c if (need_ctx or kind != 2) else None

        if kind == 0:
            y_lat, y_ctx = hgrn2_mixer(h_lat, h_ctx, hg_w_in[j], lb_tab[0, i], lb_tab[1, i],
                                       hg_norm_w[j], hg_w_out[j], need_ctx)
        elif kind == 1:
            y_lat, y_ctx = mamba2_mixer(h_lat, h_ctx, ssm_w_in[j], ssm_conv_w[j], ssm_conv_b[j],
                                        ssm_dt_bias[j], ssm_a_log[j], ssm_d[j], ssm_norm_w[j],
                                        ssm_w_out[j], need_ctx)
        elif kind == 2:
            y_lat = pool_mixer(h_lat, pool_w[j], pool_scale[j])
            y_ctx = pool_mixer(h_ctx, pool_w[j], pool_scale[j]) if need_ctx else None
        else:
            y_lat, y_ctx = window_gqa_mixer(h_lat, h_ctx, att_w_qkv[j], att_sink[j], att_w_out[j],
                                            cos, sin, need_ctx)

        x_lat = x_lat + g_l * y_lat
        x_lat = x_lat + g2_l * swiglu(rmsnorm(x_lat, norm_w[i, 1]) * (1 + sc2_l) + sh2_l,
                                      ffn_w_in[i], ffn_w_out[i])
        if need_ctx:
            x_ctx = x_ctx + g_c * y_ctx
            x_ctx = x_ctx + g2_c * swiglu(rmsnorm(x_ctx, norm_w[i, 1]) * (1 + sc2_c) + sh2_c,
                                          ffn_w_in[i], ffn_w_out[i])
    return rmsnorm(x_lat, final_norm_w)
```

```python
import functools
import math

import jax
import jax.numpy as jnp
from jax import lax
from jax.experimental import pallas as pl
from jax.experimental.pallas import tpu as pltpu

F32 = jnp.float32
BF16 = jnp.bfloat16

D_MODEL = 1024
DEPTH = 4
GRID_W = 64
EPS = 1e-6
NEG_INF = -1e30

HG_HEADS = 8
HG_DK = 128
HG_CHUNK = 64

SSM_D_INNER = 2 * D_MODEL
SSM_HEADDIM = 64
SSM_HEADS = SSM_D_INNER // SSM_HEADDIM
SSM_GROUPS = 4
SSM_STATE = 128
SSM_CONV = 5
SSM_BC = 2 * SSM_GROUPS * SSM_STATE
SSM_HPG = SSM_HEADS // SSM_GROUPS

POOL_WINDOWS = (2, 4, 8, 16)
POOL_GROUP = D_MODEL // len(POOL_WINDOWS)

ATT_HEAD_DIM = 64
ATT_Q_HEADS = D_MODEL // ATT_HEAD_DIM
ATT_KV_HEADS = 4
ATT_GRP = ATT_Q_HEADS // ATT_KV_HEADS
ATT_WINDOW = 128
ROPE_THETA = 10000.0

FFN_HIDDEN = -(-8 * D_MODEL // (3 * 256)) * 256

LANES = 128
SUBLANES = 8
TOKEN_BLOCK = 256
FFN_TILE = 256
VMEM_LIMIT = 48 * 1024 * 1024

SH1, SC1, G1, SH2, SC2, G2 = range(6)


def _cparams(sem):
    return pltpu.CompilerParams(dimension_semantics=sem, vmem_limit_bytes=VMEM_LIMIT)


def _row_tile(m, target):
    best = SUBLANES
    for t in range(SUBLANES, target + 1, SUBLANES):
        if m % t == 0:
            best = t
    return best


def _dot(a, b):
    return jnp.dot(a, b, preferred_element_type=F32)


def _dot_nt(a, b):
    return lax.dot_general(a, b, (((1,), (1,)), ((), ())), preferred_element_type=F32)


def _dot_tn(a, b):
    return lax.dot_general(a, b, (((0,), (0,)), ((), ())), preferred_element_type=F32)


def _split2(x):
    hi = x.astype(BF16)
    lo = (x - hi.astype(F32)).astype(BF16)
    return hi, lo


def _split3(x):
    hi = x.astype(BF16)
    r = x - hi.astype(F32)
    mid = r.astype(BF16)
    lo = (r - mid.astype(F32)).astype(BF16)
    return hi, mid, lo


def _silu(x):
    return x * jax.nn.sigmoid(x)


def _mod_rows(mod_ref, slot, is_ctx):
    lo, hi = slot * D_MODEL, (slot + 1) * D_MODEL
    return jnp.where(is_ctx, mod_ref[1:2, lo:hi], mod_ref[0:1, lo:hi])


def _rms(x):
    return x * lax.rsqrt(jnp.mean(x * x, axis=-1, keepdims=True) + EPS)


def _modnorm(x, nw, mod_ref, sh_slot, sc_slot, is_ctx):
    y = _rms(x) * nw
    return y * (1.0 + _mod_rows(mod_ref, sc_slot, is_ctx)) + _mod_rows(mod_ref, sh_slot, is_ctx)


def _is_ctx_rows(row0, rows, n_lat):
    return (row0 + lax.broadcasted_iota(jnp.int32, (rows, 1), 0)) >= n_lat


def _ada_kernel(c_ref, w_ref, b_ref, o_ref):
    a = _silu(c_ref[...])
    a_hi, a_lo = _split2(a)
    w_hi, w_lo = _split2(w_ref[...])
    o_ref[...] = _dot(a_hi, w_hi) + _dot(a_hi, w_lo) + _dot(a_lo, w_hi) + b_ref[...]


def _ada_table(c, c_ctx, ada_w, ada_b):
    cc = jnp.concatenate([c[0:1], c_ctx[None], jnp.zeros((SUBLANES - 2, D_MODEL), F32)], axis=0)
    n = 6 * D_MODEL
    tn = n // 4
    return pl.pallas_call(
        _ada_kernel,
        grid=(DEPTH, n // tn),
        in_specs=[pl.BlockSpec((SUBLANES, D_MODEL), lambda l, j: (0, 0)),
                  pl.BlockSpec((None, D_MODEL, tn), lambda l, j: (l, 0, j)),
                  pl.BlockSpec((None, 1, tn), lambda l, j: (l, 0, j))],
        out_specs=pl.BlockSpec((None, SUBLANES, tn), lambda l, j: (l, 0, j)),
        out_shape=jax.ShapeDtypeStruct((DEPTH, SUBLANES, n), F32),
        compiler_params=_cparams(("parallel", "parallel")),
    )(cc, ada_w, ada_b.reshape(DEPTH, 1, n))


def _norm_proj_kernel(x_ref, nw_ref, mod_ref, w_ref, *rest, tm, n_lat, rope_tiles):
    if rope_tiles:
        cos_ref, sa_ref, sb_ref, o_ref, h_ref = rest
    else:
        o_ref, h_ref = rest
    i, j = pl.program_id(0), pl.program_id(1)

    @pl.when(j == 0)
    def _():
        is_ctx = _is_ctx_rows(i * tm, tm, n_lat)
        h_ref[...] = _modnorm(x_ref[...], nw_ref[...], mod_ref, SH1, SC1, is_ctx).astype(BF16)

    y = _dot(h_ref[...], w_ref[...])
    if not rope_tiles:
        o_ref[...] = y.astype(o_ref.dtype)
        return

    @pl.when(j < rope_tiles)
    def _():
        for s_ in range(y.shape[1] // LANES):
            sl = slice(s_ * LANES, (s_ + 1) * LANES)
            ys = y[:, sl]
            r = ys * cos_ref[...] + pltpu.roll(ys, LANES - 16, axis=1) * sa_ref[...] + pltpu.roll(ys, 16, axis=1) * sb_ref[...]
            o_ref[:, sl] = r.astype(o_ref.dtype)

    @pl.when(j >= rope_tiles)
    def _():
        o_ref[...] = y.astype(o_ref.dtype)


def _norm_proj(x, nw, mods, layer, w, tn, n_lat, out_dtype, rope=None, rope_tiles=0):
    m = x.shape[0]
    n = w.shape[1]
    tm = _row_tile(m, 1280)
    in_specs = [pl.BlockSpec((tm, D_MODEL), lambda i, j: (i, 0)),
                pl.BlockSpec((1, D_MODEL), lambda i, j: (0, 0)),
                pl.BlockSpec((None, SUBLANES, 6 * D_MODEL), lambda i, j: (layer, 0, 0)),
                pl.BlockSpec((D_MODEL, tn), lambda i, j: (0, j))]
    args = [x, nw, mods, w]
    if rope_tiles:
        in_specs += [pl.BlockSpec((tm, LANES), lambda i, j: (i, 0))] * 3
        args += list(rope)
    return pl.pallas_call(
        functools.partial(_norm_proj_kernel, tm=tm, n_lat=n_lat, rope_tiles=rope_tiles),
        grid=(m // tm, n // tn),
        in_specs=in_specs,
        out_specs=pl.BlockSpec((tm, tn), lambda i, j: (i, j)),
        out_shape=jax.ShapeDtypeStruct((m, n), out_dtype),
        scratch_shapes=[pltpu.VMEM((tm, D_MODEL), BF16)],
        compiler_params=_cparams(("parallel", "arbitrary")),
    )(*args)


def _mix_ffn_kernel(x_ref, a_ref, wo_ref, *rest, tm, n_lat, pool, final):
    rest = list(rest)
    ps_ref = rest.pop(0) if pool else None
    mod_ref, nw_ref, wg_ref, wu_ref, wd_ref = rest[:5]
    rest = rest[5:]
    fw_ref = rest.pop(0) if final else None
    o_ref, h_ref, acc_ref = rest
    i, j = pl.program_id(0), pl.program_id(1)
    is_ctx = _is_ctx_rows(i * tm, tm, n_lat)

    @pl.when(j == 0)
    def _():
        if pool:
            parts = [_dot(a_ref[:, g * POOL_GROUP:(g + 1) * POOL_GROUP], wo_ref[g]) for g in range(len(POOL_WINDOWS))]
            y = jnp.concatenate(parts, axis=-1) * ps_ref[...]
        else:
            y = _dot(a_ref[...], wo_ref[...])
        x1 = x_ref[...] + _mod_rows(mod_ref, G1, is_ctx) * y
        o_ref[...] = x1
        h_ref[...] = _modnorm(x1, nw_ref[...], mod_ref, SH2, SC2, is_ctx).astype(BF16)
        acc_ref[...] = jnp.zeros_like(acc_ref)

    h = h_ref[...]
    act = (_silu(_dot(h, wg_ref[...])) * _dot(h, wu_ref[...])).astype(BF16)
    acc_ref[...] += _dot(act, wd_ref[...])

    @pl.when(j == pl.num_programs(1) - 1)
    def _():
        x2 = o_ref[...] + _mod_rows(mod_ref, G2, is_ctx) * acc_ref[...]
        o_ref[...] = _rms(x2) * fw_ref[...] if final else x2


def _mix_ffn(x, a, wo, mods, layer, nw2, w_in, w_out, n_lat, rows, pool_scale=None, final_w=None):
    pool = pool_scale is not None
    final = final_w is not None
    tm = _row_tile(rows, 640)
    nh = FFN_HIDDEN // FFN_TILE
    ka = a.shape[1]
    in_specs = [pl.BlockSpec((tm, D_MODEL), lambda i, j: (i, 0)),
                pl.BlockSpec((tm, ka), lambda i, j: (i, 0))]
    args = [x, a, wo]
    if pool:
        in_specs += [pl.BlockSpec(wo.shape, lambda i, j: (0, 0, 0)), pl.BlockSpec((1, D_MODEL), lambda i, j: (0, 0))]
        args.append(pool_scale)
    else:
        in_specs.append(pl.BlockSpec(wo.shape, lambda i, j: (0, 0)))
    in_specs += [pl.BlockSpec((None, SUBLANES, 6 * D_MODEL), lambda i, j: (layer, 0, 0)),
                 pl.BlockSpec((1, D_MODEL), lambda i, j: (0, 0)),
                 pl.BlockSpec((D_MODEL, FFN_TILE), lambda i, j: (0, j)),
                 pl.BlockSpec((D_MODEL, FFN_TILE), lambda i, j: (0, j + nh)),
                 pl.BlockSpec((FFN_TILE, D_MODEL), lambda i, j: (j, 0))]
    args += [mods, nw2, w_in, w_in, w_out]
    if final:
        in_specs.append(pl.BlockSpec((1, D_MODEL), lambda i, j: (0, 0)))
        args.append(final_w)
    return pl.pallas_call(
        functools.partial(_mix_ffn_kernel, tm=tm, n_lat=n_lat, pool=pool, final=final),
        grid=(rows // tm, nh),
        in_specs=in_specs,
        out_specs=pl.BlockSpec((tm, D_MODEL), lambda i, j: (i, 0)),
        out_shape=jax.ShapeDtypeStruct((rows, D_MODEL), F32),
        scratch_shapes=[pltpu.VMEM((tm, D_MODEL), BF16), pltpu.VMEM((tm, D_MODEL), F32)],
        compiler_params=_cparams(("parallel", "arbitrary")),
    )(*args)


def _scan_block(i, nb, rev):
    if rev:
        return jnp.where(i == 0, nb - 1, nb - 1 - i)
    return jnp.where(i == 0, nb - 1, i - 1)


def _cumsum_rows(x, rev):
    n = x.shape[0]
    row = lax.broadcasted_iota(jnp.int32, x.shape, 0)
    k = 1
    while k < n:
        if rev:
            x = x + jnp.where(row < n - k, pltpu.roll(x, n - k, axis=0), 0.0)
        else:
            x = x + jnp.where(row >= k, pltpu.roll(x, k, axis=0), 0.0)
        k *= 2
    return x


def _hgrn_kernel(q_ref, z_ref, v_ref, lb_ref, *rest, rev):
    if rev:
        g_ref, of_ref, gn_ref, o_ref, st_ref = rest
    else:
        o_ref, st_ref = rest
    L = HG_CHUNK
    nch = TOKEN_BLOCK // L

    @pl.when(pl.program_id(0) == 0)
    def _():
        st_ref[...] = jnp.zeros_like(st_ref)

    r_i = lax.broadcasted_iota(jnp.int32, (L, L), 0)
    c_i = lax.broadcasted_iota(jnp.int32, (L, L), 1)
    keep = (r_i <= c_i) if rev else (r_i >= c_i)
    mid = L // 2 if rev else L // 2 - 1
    end = 0 if rev else L - 1

    def chunk(c, carry):
        r0 = pl.multiple_of(((nch - 1 - c) if rev else c) * L, L)
        rows = pl.ds(r0, L)
        for h in range(HG_HEADS):
            sl = slice(h * HG_DK, (h + 1) * HG_DK)
            q = _silu(q_ref[rows, sl])
            z = z_ref[rows, sl]
            v = v_ref[rows, sl].astype(BF16)
            lb = lb_ref[:, sl]
            lf = jnp.log(lb + (1.0 - lb) * jax.nn.sigmoid(z))
            k = (1.0 - lb) * jax.nn.sigmoid(-z)
            cum = _cumsum_rows(lf, rev)
            ref = cum[mid:mid + 1]
            last = cum[end:end + 1]
            att = _dot_nt((q * jnp.exp(cum - ref)).astype(BF16), (k * jnp.exp(ref - cum)).astype(BF16))
            att = jnp.where(keep, att, 0.0).astype(BF16)
            st = st_ref[h]
            o = _dot_nt((q * jnp.exp(cum)).astype(BF16), st.astype(BF16)) + _dot(att, v)
            st_ref[h] = st * jnp.exp(last) + _dot_tn(v, (k * jnp.exp(last - cum)).astype(BF16))
            if rev:
                o = o + of_ref[rows, sl]
                y = _rms(o) * gn_ref[...]
                o_ref[rows, sl] = (y * _silu(g_ref[rows, sl])).astype(o_ref.dtype)
            else:
                o_ref[rows, sl] = o
        return carry

    lax.fori_loop(0, nch, chunk, 0)


def _hgrn_scan(p, lb, rev, o_f=None, gn_w=None):
    m = p.shape[0]
    nb = m // TOKEN_BLOCK
    blk = functools.partial(_scan_block, nb=nb, rev=rev)
    zcol = 2 if rev else 1
    spec = lambda col: pl.BlockSpec((TOKEN_BLOCK, D_MODEL), lambda i: (blk(i), col))
    in_specs = [spec(0), spec(zcol), spec(3), pl.BlockSpec((1, D_MODEL), lambda i: (0, 0))]
    args = [p, p, p, lb]
    if rev:
        in_specs += [spec(4), spec(0), pl.BlockSpec((1, HG_DK), lambda i: (0, 0))]
        args += [p, o_f, gn_w]
    return pl.pallas_call(
        functools.partial(_hgrn_kernel, rev=rev),
        grid=(nb,),
        in_specs=in_specs,
        out_specs=spec(0),
        out_shape=jax.ShapeDtypeStruct((m, D_MODEL), BF16 if rev else F32),
        scratch_shapes=[pltpu.VMEM((HG_HEADS, HG_DK, HG_DK), F32)],
        compiler_params=_cparams(("arbitrary",)),
    )(*args)


def _ssd_kernel(*refs, rev, nb):
    if rev:
        (xc_ref, dt_ref, dtb_ref, alog_ref, z_ref, yf_ref, dsk_ref, nw_ref, o_ref, st_ref) = refs
    else:
        (x_ref, bc_ref, xp_ref, xn_ref, bcp_ref, bcn_ref, cw_ref, cb_ref, dt_ref, dtb_ref, alog_ref,
         y_ref, xc_ref, st_ref, ext_ref) = refs
    T = TOKEN_BLOCK
    i = pl.program_id(0)

    @pl.when(i == 0)
    def _():
        st_ref[...] = jnp.zeros_like(st_ref)

    if not rev:
        b = _scan_block(i, nb, rev)
        is_lat = b < nb - 1
        vp = jnp.where(is_lat & (b > 0), 1.0, 0.0)
        vn = jnp.where(is_lat & (b < nb - 2), 1.0, 0.0)
        half = SSM_CONV // 2
        for src, prev, nxt, c0, width in ((x_ref, xp_ref, xn_ref, 0, SSM_D_INNER), (bc_ref, bcp_ref, bcn_ref, SSM_D_INNER, SSM_BC)):
            ext_ref[0:SUBLANES, 0:width] = prev[...] * vp
            ext_ref[SUBLANES:SUBLANES + T, 0:width] = src[...]
            ext_ref[SUBLANES + T:2 * SUBLANES + T, 0:width] = nxt[...] * vn
            acc = cb_ref[:, c0:c0 + width] + cw_ref[0:1, c0:c0 + width] * ext_ref[SUBLANES - half:SUBLANES - half + T, 0:width]
            for jt in range(1, SSM_CONV):
                s = SUBLANES - half + jt
                acc = acc + cw_ref[jt:jt + 1, c0:c0 + width] * ext_ref[s:s + T, 0:width]
            xc_ref[:, c0:c0 + width] = _silu(acc).astype(xc_ref.dtype)

    dt = jax.nn.softplus(dt_ref[...] + dtb_ref[...])
    dta = dt * (-jnp.exp(alog_ref[...]))
    r_i = lax.broadcasted_iota(jnp.int32, (T, T), 0)
    c_i = lax.broadcasted_iota(jnp.int32, (T, T), 1)
    keep = (r_i <= c_i) if rev else (r_i >= c_i)
    tri = jnp.where(keep, 1.0, 0.0).astype(BF16)
    parts = _split3(dta)
    cum = sum(_dot(tri, p) for p in parts)
    cum_t = cum.T
    end = 0 if rev else T - 1
    total = cum[end:end + 1]

    e_h = lax.broadcasted_iota(jnp.int32, (LANES, SSM_D_INNER), 0)
    e_c = lax.broadcasted_iota(jnp.int32, (LANES, SSM_D_INNER), 1)
    expand_m = jnp.where(e_c // SSM_HEADDIM == e_h, 1.0, 0.0).astype(BF16)

    def expand(f):
        hi, lo = _split2(f)
        return _dot(hi, expand_m) + _dot(lo, expand_m)

    dt_x = expand(dt)
    ecum_x = expand(jnp.exp(cum))
    etail_x = expand(jnp.exp(total - cum))
    dec_x = expand(jnp.broadcast_to(jnp.exp(total), (SUBLANES, LANES)))[0:1]

    lane = lax.broadcasted_iota(jnp.int32, (T, LANES), 1)
    first = lane < SSM_HEADDIM
    gw = SSM_HPG * SSM_HEADDIM
    for g in range(SSM_GROUPS):
        gl = slice(g * gw, (g + 1) * gw)
        b_g = xc_ref[:, SSM_D_INNER + g * SSM_STATE:SSM_D_INNER + (g + 1) * SSM_STATE].astype(BF16)
        c_off = SSM_D_INNER + SSM_GROUPS * SSM_STATE + g * SSM_STATE
        c_g = xc_ref[:, c_off:c_off + SSM_STATE].astype(BF16)
        cb = _dot_nt(c_g, b_g)
        xdt = xc_ref[:, gl].astype(F32) * dt_x[:, gl]
        st = st_ref[g]
        y = _dot(c_g, st.astype(BF16)) * ecum_x[:, gl]
        st_ref[g] = st * dec_x[:, gl] + _dot_tn(b_g, (xdt * etail_x[:, gl]).astype(BF16))
        xdt_b = xdt.astype(BF16)
        for pr in range(SSM_HPG // 2):
            h0 = g * SSM_HPG + 2 * pr
            pl_ = slice(pr * LANES, (pr + 1) * LANES)
            ys = []
            for hh in (h0, h0 + 1):
                seg = cum[:, hh:hh + 1] - cum_t[hh:hh + 1, :]
                w = (cb * jnp.where(keep, jnp.exp(seg), 0.0)).astype(BF16)
                ys.append(_dot(w, xdt_b[:, pl_]))
            yp = y[:, pl_] + jnp.where(first, ys[0], ys[1])
            ol = slice(g * gw + pr * LANES, g * gw + (pr + 1) * LANES)
            if rev:
                yt = yp + yf_ref[:, ol] + dsk_ref[:, ol] * xc_ref[:, ol].astype(F32)
                y_store = yt * _silu(z_ref[:, ol])
                o_ref[:, ol] = y_store.astype(o_ref.dtype)
            else:
                y_ref[:, ol] = yp
        if rev:
            yg = o_ref[:, gl].astype(F32)
            o_ref[:, gl] = (_rms(yg) * nw_ref[:, gl]).astype(o_ref.dtype)


def _ssd_scan(p, rev, dt_bias, a_log, *, conv_w=None, conv_b=None, xc=None, y_f=None, d_skip=None, norm_w=None):
    m = p.shape[0]
    T = TOKEN_BLOCK
    nb = m // T
    blk = functools.partial(_scan_block, nb=nb, rev=rev)
    row8 = T // SUBLANES
    dt_col = (SSM_D_INNER + SSM_D_INNER + SSM_BC) // LANES + (1 if rev else 0)
    vec = lambda w: pl.BlockSpec((1, w), lambda i: (0, 0))
    dt_spec = pl.BlockSpec((T, LANES), lambda i: (blk(i), dt_col))
    if rev:
        in_specs = [pl.BlockSpec((T, SSM_D_INNER + SSM_BC), lambda i: (blk(i), 0)), dt_spec, vec(LANES), vec(LANES),
                    pl.BlockSpec((T, SSM_D_INNER), lambda i: (blk(i), 0)),
                    pl.BlockSpec((T, SSM_D_INNER), lambda i: (blk(i), 0)),
                    vec(SSM_D_INNER), vec(SSM_D_INNER)]
        args = [xc, p, dt_bias, a_log, p, y_f, d_skip, norm_w]
        out_shape = jax.ShapeDtypeStruct((m, SSM_D_INNER), F32)
        out_specs = pl.BlockSpec((T, SSM_D_INNER), lambda i: (blk(i), 0))
        scratch = [pltpu.VMEM((SSM_GROUPS, SSM_STATE, SSM_HPG * SSM_HEADDIM), F32)]
    else:
        prev = lambda i: jnp.maximum(blk(i) * row8 - 1, 0)
        nxt = lambda i: jnp.minimum((blk(i) + 1) * row8, m // SUBLANES - 1)
        bc_col = 2 * SSM_D_INNER // SSM_BC
        in_specs = [pl.BlockSpec((T, SSM_D_INNER), lambda i: (blk(i), 1)),
                    pl.BlockSpec((T, SSM_BC), lambda i: (blk(i), bc_col)),
                    pl.BlockSpec((SUBLANES, SSM_D_INNER), lambda i: (prev(i), 1)),
                    pl.BlockSpec((SUBLANES, SSM_D_INNER), lambda i: (nxt(i), 1)),
                    pl.BlockSpec((SUBLANES, SSM_BC), lambda i: (prev(i), bc_col)),
                    pl.BlockSpec((SUBLANES, SSM_BC), lambda i: (nxt(i), bc_col)),
                    pl.BlockSpec((SSM_CONV, SSM_D_INNER + SSM_BC), lambda i: (0, 0)),
                    vec(SSM_D_INNER + SSM_BC), dt_spec, vec(LANES), vec(LANES)]
        args = [p, p, p, p, p, p, conv_w, conv_b, p, dt_bias, a_log]
        out_shape = (jax.ShapeDtypeStruct((m, SSM_D_INNER), F32),
                     jax.ShapeDtypeStruct((m, SSM_D_INNER + SSM_BC), F32))
        out_specs = (pl.BlockSpec((T, SSM_D_INNER), lambda i: (blk(i), 0)),
                     pl.BlockSpec((T, SSM_D_INNER + SSM_BC), lambda i: (blk(i), 0)))
        scratch = [pltpu.VMEM((SSM_GROUPS, SSM_STATE, SSM_HPG * SSM_HEADDIM), F32),
                   pltpu.VMEM((T + 2 * SUBLANES, SSM_D_INNER), F32)]
    return pl.pallas_call(
        functools.partial(_ssd_kernel, rev=rev, nb=nb),
        grid=(nb,),
        in_specs=in_specs,
        out_specs=out_specs,
        out_shape=out_shape,
        scratch_shapes=scratch,
        compiler_params=_cparams(("arbitrary",)),
    )(*args)


def _pool_kernel(x_ref, xp_ref, xn_ref, nw_ref, mod_ref, o_ref, ext_ref, *, nb, n_lat, n_ctx):
    T = TOKEN_BLOCK
    b = pl.program_id(0)
    is_ctx = b == nb - 1
    vp = jnp.where((b > 0) & (b < nb - 1), 1.0, 0.0)
    vn = jnp.where(b < nb - 2, 1.0, 0.0)
    norm = lambda x: _modnorm(x, nw_ref[...], mod_ref, SH1, SC1, is_ctx)
    ext_ref[0:SUBLANES] = norm(xp_ref[...]) * vp
    ext_ref[SUBLANES:SUBLANES + T] = norm(x_ref[...])
    ext_ref[SUBLANES + T:] = norm(xn_ref[...]) * vn
    t = jnp.where(is_ctx, 0, b * T) + lax.broadcasted_iota(jnp.int32, (T, 1), 0)
    seq = jnp.where(is_ctx, n_ctx, n_lat)
    for gi, win in enumerate(POOL_WINDOWS):
        sl = slice(gi * POOL_GROUP, (gi + 1) * POOL_GROUP)
        s0 = SUBLANES - win // 2
        acc = ext_ref[s0:s0 + T, sl]
        for jw in range(1, win):
            acc = acc + ext_ref[s0 + jw:s0 + jw + T, sl]
        cnt = jnp.minimum(t + (win - win // 2), seq) - jnp.maximum(t - win // 2, 0)
        o_ref[:, sl] = (acc / cnt.astype(F32) - ext_ref[SUBLANES:SUBLANES + T, sl]).astype(o_ref.dtype)


def _pool_delta(x, nw, mods, layer, n_lat, n_ctx):
    m = x.shape[0]
    T = TOKEN_BLOCK
    nb = m // T
    row8 = T // SUBLANES
    return pl.pallas_call(
        functools.partial(_pool_kernel, nb=nb, n_lat=n_lat, n_ctx=n_ctx),
        grid=(nb,),
        in_specs=[pl.BlockSpec((T, D_MODEL), lambda i: (i, 0)),
                  pl.BlockSpec((SUBLANES, D_MODEL), lambda i: (jnp.maximum(i * row8 - 1, 0), 0)),
                  pl.BlockSpec((SUBLANES, D_MODEL), lambda i: (jnp.minimum((i + 1) * row8, m // SUBLANES - 1), 0)),
                  pl.BlockSpec((1, D_MODEL), lambda i: (0, 0)),
                  pl.BlockSpec((None, SUBLANES, 6 * D_MODEL), lambda i: (layer, 0, 0))],
        out_specs=pl.BlockSpec((T, D_MODEL), lambda i: (i, 0)),
        out_shape=jax.ShapeDtypeStruct((m, D_MODEL), BF16),
        scratch_shapes=[pltpu.VMEM((T + 2 * SUBLANES, D_MODEL), F32)],
        compiler_params=_cparams(("parallel",)),
    )(x, x, x, nw, mods)


def _attn_kernel(q_ref, kc_ref, kp_ref, kn_ref, kx_ref, sink_ref, o_ref, *, n_lat):
    T = TOKEN_BLOCK
    W = ATT_WINDOW
    i = pl.program_id(0)
    kv = jnp.concatenate([kp_ref[...], kc_ref[...], kn_ref[...], kx_ref[...]], axis=0)
    span = T + 2 * W
    nk = kv.shape[0]
    rows = ATT_GRP * T
    qpos = i * T + lax.broadcasted_iota(jnp.int32, (rows, nk), 0) % T
    col = lax.broadcasted_iota(jnp.int32, (rows, nk), 1)
    kpos = i * T - W + col
    valid = (col >= span) | ((jnp.abs(kpos - qpos) <= W) & (kpos >= 0) & (kpos < n_lat))
    head_row = lax.broadcasted_iota(jnp.int32, (rows, 1), 0) // T
    kd = ATT_KV_HEADS * ATT_HEAD_DIM
    for kh in range(ATT_KV_HEADS):
        k = kv[:, kh * ATT_HEAD_DIM:(kh + 1) * ATT_HEAD_DIM]
        v = kv[:, kd + kh * ATT_HEAD_DIM:kd + (kh + 1) * ATT_HEAD_DIM]
        q = jnp.concatenate([q_ref[:, (kh * ATT_GRP + g) * ATT_HEAD_DIM:(kh * ATT_GRP + g + 1) * ATT_HEAD_DIM]
                             for g in range(ATT_GRP)], axis=0)
        s = jnp.where(valid, _dot_nt(q, k), NEG_INF)
        sink = jnp.zeros((rows, 1), F32)
        for g in range(ATT_GRP):
            sink = jnp.where(head_row == g, sink_ref[kh * ATT_GRP + g], sink)
        mx = jnp.maximum(jnp.max(s, axis=-1, keepdims=True), sink)
        p = jnp.exp(s - mx)
        den = jnp.sum(p, axis=-1, keepdims=True) + jnp.exp(sink - mx)
        o = _dot(p.astype(BF16), v) / den
        for g in range(0, ATT_GRP, 2):
            pair = jnp.concatenate([o[g * T:(g + 1) * T], o[(g + 1) * T:(g + 2) * T]], axis=-1)
            c0 = (kh * ATT_GRP + g) * ATT_HEAD_DIM
            o_ref[:, c0:c0 + 2 * ATT_HEAD_DIM] = pair.astype(o_ref.dtype)


def _attention(qkv, sink, n_lat, n_ctx):
    T = TOKEN_BLOCK
    W = ATT_WINDOW
    nq = n_lat // T
    kvw = 2 * ATT_KV_HEADS * ATT_HEAD_DIM
    kv_col = D_MODEL // kvw
    wpb = T // W
    return pl.pallas_call(
        functools.partial(_attn_kernel, n_lat=n_lat),
        grid=(nq,),
        in_specs=[pl.BlockSpec((T, D_MODEL), lambda i: (i, 0)),
                  pl.BlockSpec((T, kvw), lambda i: (i, kv_col)),
                  pl.BlockSpec((W, kvw), lambda i: (jnp.maximum(i * wpb - 1, 0), kv_col)),
                  pl.BlockSpec((W, kvw), lambda i: (jnp.minimum((i + 1) * wpb, n_lat // W - 1), kv_col)),
                  pl.BlockSpec((n_ctx, kvw), lambda i: (n_lat // n_ctx, kv_col)),
                  pl.BlockSpec(memory_space=pltpu.SMEM)],
        out_specs=pl.BlockSpec((T, D_MODEL), lambda i: (i, 0)),
        out_shape=jax.ShapeDtypeStruct((n_lat, D_MODEL), BF16),
        compiler_params=_cparams(("parallel",)),
    )(qkv, qkv, qkv, qkv, qkv, sink)


def _rope_tables(n_lat, n_ctx):
    half = ATT_HEAD_DIM // 2
    inv = ROPE_THETA ** (-jnp.arange(0, half, 2, dtype=F32) / half)
    t = jnp.arange(n_lat)
    ang = jnp.concatenate([(t // GRID_W).astype(F32)[:, None] * inv[None]] * 2
                          + [(t % GRID_W).astype(F32)[:, None] * inv[None]] * 2, axis=-1)
    cos, sin = jnp.cos(ang), jnp.sin(ang)
    first = (jnp.arange(ATT_HEAD_DIM) % half) < half // 2
    scale = ATT_HEAD_DIM ** -0.5
    tabs = [cos, jnp.where(first, -sin, 0.0), jnp.where(first, 0.0, sin)]
    ident = [jnp.ones((n_ctx, ATT_HEAD_DIM), F32), jnp.zeros((n_ctx, ATT_HEAD_DIM), F32), jnp.zeros((n_ctx, ATT_HEAD_DIM), F32)]
    reps = LANES // ATT_HEAD_DIM
    return [jnp.tile(jnp.concatenate([a, b], axis=0), (1, reps)) for a, b in zip(tabs, ident)], scale


def kernel(x, c, ctx, c_ctx, ada_w, ada_b, norm_w, ffn_w_in, ffn_w_out, final_norm_w, hg_w_in, hg_lb, hg_norm_w, hg_w_out, ssm_w_in, ssm_conv_w, ssm_conv_b, ssm_dt_bias, ssm_a_log, ssm_d, ssm_norm_w, ssm_w_out, pool_w, pool_scale, att_w_qkv, att_sink, att_w_out):
    assert x.shape[0] == 1 and DEPTH == 4
    n_lat, n_ctx = x.shape[1], ctx.shape[1]
    assert n_lat % TOKEN_BLOCK == 0 and n_ctx == TOKEN_BLOCK
    xs = jnp.concatenate([x[0], ctx[0]], axis=0)
    mods = _ada_table(c, c_ctx, ada_w, ada_b)
    lb_tab = jnp.cumsum(jax.nn.softmax(hg_lb.astype(F32), axis=1), axis=1)
    row = lambda v: v.reshape(1, -1)
    ffn = lambda l: (ffn_w_in[l].astype(BF16), ffn_w_out[l].astype(BF16))

    p = _norm_proj(xs, row(norm_w[0, 0]), mods, 0, hg_w_in[0].astype(BF16), 1024, n_lat, F32)
    o_f = _hgrn_scan(p, row(lb_tab[0, 0]), False)
    a = _hgrn_scan(p, row(lb_tab[1, 0]), True, o_f=o_f, gn_w=row(hg_norm_w[0]))
    xs = _mix_ffn(xs, a, hg_w_out[0].astype(BF16), mods, 0, row(norm_w[0, 1]), *ffn(0), n_lat, xs.shape[0])

    w = ssm_w_in[0]
    zx = 2 * SSM_D_INNER + SSM_BC
    pad = jnp.zeros((D_MODEL, LANES - SSM_HEADS), F32)
    w = jnp.concatenate([w[:, :zx], w[:, zx:zx + SSM_HEADS], pad, w[:, zx + SSM_HEADS:], pad], axis=1)
    p = _norm_proj(xs, row(norm_w[1, 0]), mods, 1, w.astype(BF16), 768, n_lat, F32)
    lane_pad = lambda v: jnp.concatenate([v, jnp.zeros((LANES - SSM_HEADS,), F32)]).reshape(1, LANES)
    y_f, xc = _ssd_scan(p, False, lane_pad(ssm_dt_bias[0, 0]), lane_pad(ssm_a_log[0, 0]),
                        conv_w=ssm_conv_w[0], conv_b=row(ssm_conv_b[0]))
    a = _ssd_scan(p, True, lane_pad(ssm_dt_bias[0, 1]), lane_pad(ssm_a_log[0, 1]), xc=xc, y_f=y_f,
                  d_skip=row(jnp.repeat(ssm_d[0], SSM_HEADDIM)), norm_w=row(ssm_norm_w[0]))
    xs = _mix_ffn(xs, a.astype(BF16), ssm_w_out[0].astype(BF16), mods, 1, row(norm_w[1, 1]), *ffn(1), n_lat, xs.shape[0])

    a = _pool_delta(xs, row(norm_w[2, 0]), mods, 2, n_lat, n_ctx)
    xs = _mix_ffn(xs, a, pool_w[0].astype(BF16), mods, 2, row(norm_w[2, 1]), *ffn(2), n_lat, xs.shape[0],
                  pool_scale=row(pool_scale[0]))

    rope, scale = _rope_tables(n_lat, n_ctx)
    qd = ATT_Q_HEADS * ATT_HEAD_DIM
    wq = jnp.concatenate([att_w_qkv[0][:, :qd] * scale, att_w_qkv[0][:, qd:]], axis=1)
    kd = ATT_KV_HEADS * ATT_HEAD_DIM
    qkv = _norm_proj(xs, row(norm_w[3, 0]), mods, 3, wq.astype(BF16), 256, n_lat, BF16, rope=rope,
                     rope_tiles=(qd + kd) // 256)
    a = _attention(qkv, att_sink[0], n_lat, n_ctx)
    out = _mix_ffn(xs, a, att_w_out[0].astype(BF16), mods, 3, row(norm_w[3, 1]), *ffn(3), n_lat, n_lat,
                   final_w=row(final_norm_w))
    return out[None]
```

```python
import functools

import jax
import jax.numpy as jnp
from jax import lax
from jax.experimental import pallas as pl
from jax.experimental.pallas import tpu as pltpu

F32 = jnp.float32
BF16 = jnp.bfloat16

D_MODEL = 1024
DEPTH = 4
GRID_W = 64
EPS = 1e-6
NEG_INF = -1e30

HG_HEADS = 8
HG_DK = 128
HG_CHUNK = 64

SSM_D_INNER = 2 * D_MODEL
SSM_HEADDIM = 64
SSM_HEADS = SSM_D_INNER // SSM_HEADDIM
SSM_GROUPS = 4
SSM_STATE = 128
SSM_CONV = 5
SSM_BC = 2 * SSM_GROUPS * SSM_STATE
SSM_XBC = SSM_D_INNER + SSM_BC
SSM_HPG = SSM_HEADS // SSM_GROUPS

POOL_WINDOWS = (2, 4, 8, 16)
POOL_GROUP = D_MODEL // len(POOL_WINDOWS)

ATT_HEAD_DIM = 64
ATT_Q_HEADS = D_MODEL // ATT_HEAD_DIM
ATT_KV_HEADS = 4
ATT_GRP = ATT_Q_HEADS // ATT_KV_HEADS
ATT_WINDOW = 128
ROPE_THETA = 10000.0

FFN_HIDDEN = -(-8 * D_MODEL // (3 * 256)) * 256

LANES = 128
SUBLANES = 8
BF16_ROWS = 16
MXU_COLS = 256
TOKEN_BLOCK = 256
ROW_TILE = 640
VMEM_LIMIT = 56 * 1024 * 1024

SH1, SC1, G1, SH2, SC2, G2 = range(6)


def _cparams(sem):
    return pltpu.CompilerParams(dimension_semantics=sem, vmem_limit_bytes=VMEM_LIMIT)


def _row_tile(m, target):
    best = None
    for t in range(HG_CHUNK, target + 1, HG_CHUNK):
        if m % t == 0:
            best = t
    assert best is not None
    return best


def _resident(shape):
    zeros = (0,) * len(shape)
    return pl.BlockSpec(shape, lambda *_: zeros, pipeline_mode=pl.Buffered(1))


def _dot(a, b):
    return jnp.dot(a, b, preferred_element_type=F32)


def _dot_nt(a, b):
    return lax.dot_general(a, b, (((1,), (1,)), ((), ())), preferred_element_type=F32)


def _dot_tn(a, b):
    return lax.dot_general(a, b, (((0,), (0,)), ((), ())), preferred_element_type=F32)


def _split2(x):
    hi = x.astype(BF16)
    lo = (x - hi.astype(F32)).astype(BF16)
    return hi, lo


def _split3(x):
    hi = x.astype(BF16)
    r = x - hi.astype(F32)
    mid = r.astype(BF16)
    lo = (r - mid.astype(F32)).astype(BF16)
    return hi, mid, lo


def _silu(x):
    return x * jax.nn.sigmoid(x)


def _mod_rows(mod_ref, slot, is_ctx):
    lo, hi = slot * D_MODEL, (slot + 1) * D_MODEL
    return jnp.where(is_ctx, mod_ref[1:2, lo:hi], mod_ref[0:1, lo:hi])


def _rms(x):
    return x * lax.rsqrt(jnp.mean(x * x, axis=-1, keepdims=True) + EPS)


def _modnorm(x, nw, mod_ref, sh_slot, sc_slot, is_ctx):
    y = _rms(x) * nw
    return y * (1.0 + _mod_rows(mod_ref, sc_slot, is_ctx)) + _mod_rows(mod_ref, sh_slot, is_ctx)


def _is_ctx_rows(row0, rows, n_lat):
    return (row0 + lax.broadcasted_iota(jnp.int32, (rows, 1), 0)) >= n_lat


def _mod_spec(layer):
    return pl.BlockSpec((None, SUBLANES, 6 * D_MODEL), lambda *_: (layer, 0, 0))


def _ada_kernel(c_ref, w_ref, b_ref, o_ref):
    a = _silu(c_ref[...])
    a_hi, a_lo = _split2(a)
    w_hi, w_lo = _split2(w_ref[...])
    o_ref[...] = _dot(a_hi, w_hi) + _dot(a_hi, w_lo) + _dot(a_lo, w_hi) + b_ref[...]


def _ada_table(c, c_ctx, ada_w, ada_b):
    cc = jnp.concatenate([c[0:1], c_ctx[None], jnp.zeros((SUBLANES - 2, D_MODEL), F32)], axis=0)
    n = 6 * D_MODEL
    tn = n // 4
    return pl.pallas_call(
        _ada_kernel,
        grid=(DEPTH, n // tn),
        in_specs=[pl.BlockSpec((SUBLANES, D_MODEL), lambda l, j: (0, 0)),
                  pl.BlockSpec((None, D_MODEL, tn), lambda l, j: (l, 0, j)),
                  pl.BlockSpec((None, 1, tn), lambda l, j: (l, 0, j))],
        out_specs=pl.BlockSpec((None, SUBLANES, tn), lambda l, j: (l, 0, j)),
        out_shape=jax.ShapeDtypeStruct((DEPTH, SUBLANES, n), F32),
        compiler_params=_cparams(("parallel", "parallel")),
    )(cc, ada_w, ada_b.reshape(DEPTH, 1, n))


def _cumsum_rows(x, rev):
    n = x.shape[0]
    row = lax.broadcasted_iota(jnp.int32, x.shape, 0)
    k = 1
    while k < n:
        if rev:
            x = x + jnp.where(row < n - k, pltpu.roll(x, n - k, axis=0), 0.0)
        else:
            x = x + jnp.where(row >= k, pltpu.roll(x, k, axis=0), 0.0)
        k *= 2
    return x


def _normed_rows(x_ref, nw_ref, mod_ref, tm, n_lat):
    is_ctx = _is_ctx_rows(pl.program_id(0) * tm, tm, n_lat)
    return _modnorm(x_ref[...], nw_ref[...], mod_ref, SH1, SC1, is_ctx).astype(BF16)


def _proj_hgrn_kernel(x_ref, nw_ref, mod_ref, w_ref, lb_ref, qs_ref, cumf_ref, kf_ref, cumb_ref, kb_ref, v_ref, gs_ref,
                      *, tm, n_lat):
    h = _normed_rows(x_ref, nw_ref, mod_ref, tm, n_lat)
    col = lambda k: slice(k * D_MODEL, (k + 1) * D_MODEL)
    qs_ref[...] = _silu(_dot(h, w_ref[:, col(0)])).astype(qs_ref.dtype)
    for d, (cum_ref, k_ref) in enumerate(((cumf_ref, kf_ref), (cumb_ref, kb_ref))):
        lb = lb_ref[d:d + 1, :]
        f = lb + (1.0 - lb) * jax.nn.sigmoid(_dot(h, w_ref[:, col(1 + d)]))
        k_ref[...] = (1.0 - f).astype(k_ref.dtype)
        lf = jnp.log(f)
        for c in range(tm // HG_CHUNK):
            rows = slice(c * HG_CHUNK, (c + 1) * HG_CHUNK)
            cum_ref[rows, :] = _cumsum_rows(lf[rows], rev=bool(d))
    v_ref[...] = _dot(h, w_ref[:, col(3)]).astype(v_ref.dtype)
    gs_ref[...] = _silu(_dot(h, w_ref[:, col(4)])).astype(gs_ref.dtype)


def _proj_ssd_kernel(x_ref, nw_ref, mod_ref, w_ref, wdt_ref, zs_ref, xbc_ref, dt_ref, *, tm, n_lat):
    h = _normed_rows(x_ref, nw_ref, mod_ref, tm, n_lat)
    for c in range(SSM_D_INNER // D_MODEL):
        sl = slice(c * D_MODEL, (c + 1) * D_MODEL)
        zs_ref[:, sl] = _silu(_dot(h, w_ref[:, sl])).astype(zs_ref.dtype)
    for c in range(SSM_XBC // D_MODEL):
        sl = slice(c * D_MODEL, (c + 1) * D_MODEL)
        xbc_ref[:, sl] = _dot(h, w_ref[:, SSM_D_INNER + c * D_MODEL:SSM_D_INNER + (c + 1) * D_MODEL]).astype(xbc_ref.dtype)
    dt_ref[...] = _dot(h, wdt_ref[...])


def _proj_attn_kernel(x_ref, nw_ref, mod_ref, w_ref, cos_ref, sa_ref, sb_ref, o_ref, *, tm, n_lat, rope_cols):
    h = _normed_rows(x_ref, nw_ref, mod_ref, tm, n_lat)
    n = w_ref.shape[1]
    for c in range(n // MXU_COLS):
        y = _dot(h, w_ref[:, c * MXU_COLS:(c + 1) * MXU_COLS])
        for s_ in range(MXU_COLS // LANES):
            c0 = c * MXU_COLS + s_ * LANES
            ys = y[:, s_ * LANES:(s_ + 1) * LANES]
            if c0 < rope_cols:
                ys = (ys * cos_ref[...] + pltpu.roll(ys, LANES - 16, axis=1) * sa_ref[...]
                      + pltpu.roll(ys, 16, axis=1) * sb_ref[...])
            o_ref[:, c0:c0 + LANES] = ys.astype(o_ref.dtype)


def _proj_call(kernel_fn, x, nw, mods, layer, extra_in, extra_specs, out_shapes, n_lat, **kw):
    m = x.shape[0]
    tm = _row_tile(m, ROW_TILE)
    row = lambda w: pl.BlockSpec((tm, w), lambda i: (i, 0))
    return pl.pallas_call(
        functools.partial(kernel_fn, tm=tm, n_lat=n_lat, **kw),
        grid=(m // tm,),
        in_specs=[row(D_MODEL), _resident((1, D_MODEL)), _mod_spec(layer)] + [s(tm) if callable(s) else s for s in extra_specs],
        out_specs=[row(s.shape[1]) for s in out_shapes],
        out_shape=out_shapes,
        compiler_params=_cparams(("parallel",)),
    )(x, nw, mods, *extra_in)


def _ffn_chunks():
    out, c0 = [], 0
    while c0 < FFN_HIDDEN:
        cw = min(D_MODEL, FFN_HIDDEN - c0)
        assert cw % MXU_COLS == 0
        out.append((c0, cw))
        c0 += cw
    return out


def _mix_ffn_kernel(x_ref, a_ref, wo_ref, *rest, tm, n_lat, pool, final):
    rest = list(rest)
    ps_ref = rest.pop(0) if pool else None
    mod_ref, nw_ref, win_ref, wout_ref = rest[:4]
    rest = rest[4:]
    fw_ref = rest.pop(0) if final else None
    (o_ref,) = rest
    is_ctx = _is_ctx_rows(pl.program_id(0) * tm, tm, n_lat)
    if pool:
        parts = [_dot(a_ref[:, g * POOL_GROUP:(g + 1) * POOL_GROUP], wo_ref[g]) for g in range(len(POOL_WINDOWS))]
        y = jnp.concatenate(parts, axis=-1) * ps_ref[...]
    else:
        y = _dot(a_ref[...], wo_ref[...])
    x1 = x_ref[...] + _mod_rows(mod_ref, G1, is_ctx) * y
    h = _modnorm(x1, nw_ref[...], mod_ref, SH2, SC2, is_ctx).astype(BF16)
    acc = None
    for c0, cw in _ffn_chunks():
        gate = _dot(h, win_ref[:, c0:c0 + cw])
        up = _dot(h, win_ref[:, FFN_HIDDEN + c0:FFN_HIDDEN + c0 + cw])
        part = _dot((_silu(gate) * up).astype(BF16), wout_ref[c0:c0 + cw, :])
        acc = part if acc is None else acc + part
    x2 = x1 + _mod_rows(mod_ref, G2, is_ctx) * acc
    o_ref[...] = _rms(x2) * fw_ref[...] if final else x2


def _mix_ffn(x, a, wo, mods, layer, nw2, w_in, w_out, n_lat, rows, pool_scale=None, final_w=None):
    pool = pool_scale is not None
    final = final_w is not None
    tm = _row_tile(rows, ROW_TILE)
    row = lambda w: pl.BlockSpec((tm, w), lambda i: (i, 0))
    in_specs = [row(D_MODEL), row(a.shape[1]), _resident(wo.shape)]
    args = [x, a, wo]
    if pool:
        in_specs.append(_resident((1, D_MODEL)))
        args.append(pool_scale)
    in_specs += [_mod_spec(layer), _resident((1, D_MODEL)), _resident(w_in.shape), _resident(w_out.shape)]
    args += [mods, nw2, w_in, w_out]
    if final:
        in_specs.append(_resident((1, D_MODEL)))
        args.append(final_w)
    return pl.pallas_call(
        functools.partial(_mix_ffn_kernel, tm=tm, n_lat=n_lat, pool=pool, final=final),
        grid=(rows // tm,),
        in_specs=in_specs,
        out_specs=row(D_MODEL),
        out_shape=jax.ShapeDtypeStruct((rows, D_MODEL), F32),
        compiler_params=_cparams(("parallel",)),
    )(*args)


def _scan_block(i, nb, rev):
    if rev:
        return jnp.where(i == 0, nb - 1, nb - 1 - i)
    return jnp.where(i == 0, nb - 1, i - 1)


def _hgrn_kernel(q_ref, cum_ref, k_ref, v_ref, *rest, rev):
    if rev:
        g_ref, of_ref, gn_ref, o_ref, st_ref = rest
    else:
        o_ref, st_ref = rest
    L = HG_CHUNK
    nch = TOKEN_BLOCK // L

    @pl.when(pl.program_id(0) == 0)
    def _():
        st_ref[...] = jnp.zeros_like(st_ref)

    r_i = lax.broadcasted_iota(jnp.int32, (L, L), 0)
    c_i = lax.broadcasted_iota(jnp.int32, (L, L), 1)
    keep = (r_i <= c_i) if rev else (r_i >= c_i)
    mid = L // 2 if rev else L // 2 - 1
    end = 0 if rev else L - 1

    def chunk(c, carry):
        r0 = pl.multiple_of(((nch - 1 - c) if rev else c) * L, L)
        rows = pl.ds(r0, L)
        for h in range(HG_HEADS):
            sl = slice(h * HG_DK, (h + 1) * HG_DK)
            q = q_ref[rows, sl].astype(F32)
            k = k_ref[rows, sl].astype(F32)
            v = v_ref[rows, sl]
            cum = cum_ref[rows, sl]
            ref = cum[mid:mid + 1]
            last = cum[end:end + 1]
            att = _dot_nt((q * jnp.exp(cum - ref)).astype(BF16), (k * jnp.exp(ref - cum)).astype(BF16))
            att = jnp.where(keep, att, 0.0).astype(BF16)
            st = st_ref[h]
            o = _dot_nt((q * jnp.exp(cum)).astype(BF16), st.astype(BF16)) + _dot(att, v)
            st_ref[h] = st * jnp.exp(last) + _dot_tn(v, (k * jnp.exp(last - cum)).astype(BF16))
            if rev:
                o = o + of_ref[rows, sl].astype(F32)
                y = _rms(o) * gn_ref[...]
                o_ref[rows, sl] = (y * g_ref[rows, sl].astype(F32)).astype(o_ref.dtype)
            else:
                o_ref[rows, sl] = o.astype(o_ref.dtype)
        return carry

    lax.fori_loop(0, nch, chunk, 0)


def _hgrn_scan(qs, cum, k, v, rev, gs=None, o_f=None, gn_w=None):
    m = qs.shape[0]
    nb = m // TOKEN_BLOCK
    blk = functools.partial(_scan_block, nb=nb, rev=rev)
    spec = pl.BlockSpec((TOKEN_BLOCK, D_MODEL), lambda i: (blk(i), 0))
    in_specs = [spec] * 4
    args = [qs, cum, k, v]
    if rev:
        in_specs += [spec, spec, _resident((1, HG_DK))]
        args += [gs, o_f, gn_w]
    return pl.pallas_call(
        functools.partial(_hgrn_kernel, rev=rev),
        grid=(nb,),
        in_specs=in_specs,
        out_specs=spec,
        out_shape=jax.ShapeDtypeStruct((m, D_MODEL), BF16),
        scratch_shapes=[pltpu.VMEM((HG_HEADS, HG_DK, HG_DK), F32)],
        compiler_params=_cparams(("arbitrary",)),
    )(*args)


def _ssd_kernel(*refs, rev, nb):
    if rev:
        (xc_ref, dt_ref, dtb_ref, alog_ref, zs_ref, yf_ref, dsk_ref, nw_ref, o_ref, st_ref, stage_ref) = refs
    else:
        (x_ref, bc_ref, xp_ref, xn_ref, bcp_ref, bcn_ref, cw_ref, cb_ref, dt_ref, dtb_ref, alog_ref,
         y_ref, xc_ref, st_ref, ext_ref) = refs
    T = TOKEN_BLOCK
    H = BF16_ROWS
    i = pl.program_id(0)

    @pl.when(i == 0)
    def _():
        st_ref[...] = jnp.zeros_like(st_ref)

    if not rev:
        b = _scan_block(i, nb, rev)
        is_lat = b < nb - 1
        vp = jnp.where(is_lat & (b > 0), 1.0, 0.0)
        vn = jnp.where(is_lat & (b < nb - 2), 1.0, 0.0)
        half = SSM_CONV // 2
        for src, prev, nxt, c0, width in ((x_ref, xp_ref, xn_ref, 0, SSM_D_INNER), (bc_ref, bcp_ref, bcn_ref, SSM_D_INNER, SSM_BC)):
            ext_ref[0:H, 0:width] = prev[...].astype(F32) * vp
            ext_ref[H:H + T, 0:width] = src[...].astype(F32)
            ext_ref[H + T:2 * H + T, 0:width] = nxt[...].astype(F32) * vn
            acc = cb_ref[:, c0:c0 + width] + cw_ref[0:1, c0:c0 + width] * ext_ref[H - half:H - half + T, 0:width]
            for jt in range(1, SSM_CONV):
                s = H - half + jt
                acc = acc + cw_ref[jt:jt + 1, c0:c0 + width] * ext_ref[s:s + T, 0:width]
            xc_ref[:, c0:c0 + width] = _silu(acc).astype(xc_ref.dtype)

    dt = jax.nn.softplus(dt_ref[...] + dtb_ref[...])
    dta = dt * (-jnp.exp(alog_ref[...]))
    r_i = lax.broadcasted_iota(jnp.int32, (T, T), 0)
    c_i = lax.broadcasted_iota(jnp.int32, (T, T), 1)
    keep = (r_i <= c_i) if rev else (r_i >= c_i)
    tri = jnp.where(keep, 1.0, 0.0).astype(BF16)
    cum = sum(_dot(tri, p) for p in _split3(dta))
    cum_t = cum.T
    end = 0 if rev else T - 1
    total = cum[end:end + 1]

    e_h = lax.broadcasted_iota(jnp.int32, (LANES, SSM_D_INNER), 0)
    e_c = lax.broadcasted_iota(jnp.int32, (LANES, SSM_D_INNER), 1)
    expand_m = jnp.where(e_c // SSM_HEADDIM == e_h, 1.0, 0.0).astype(BF16)

    def expand(f):
        hi, lo = _split2(f)
        return _dot(hi, expand_m) + _dot(lo, expand_m)

    dt_x = expand(dt)
    ecum_x = expand(jnp.exp(cum))
    etail_x = expand(jnp.exp(total - cum))
    dec_x = expand(jnp.broadcast_to(jnp.exp(total), (SUBLANES, LANES)))[0:1]

    lane = lax.broadcasted_iota(jnp.int32, (T, LANES), 1)
    first = lane < SSM_HEADDIM
    gw = SSM_HPG * SSM_HEADDIM
    for g in range(SSM_GROUPS):
        gl = slice(g * gw, (g + 1) * gw)
        b_g = xc_ref[:, SSM_D_INNER + g * SSM_STATE:SSM_D_INNER + (g + 1) * SSM_STATE]
        c_off = SSM_D_INNER + SSM_GROUPS * SSM_STATE + g * SSM_STATE
        c_g = xc_ref[:, c_off:c_off + SSM_STATE]
        cb = _dot_nt(c_g, b_g)
        xdt = xc_ref[:, gl].astype(F32) * dt_x[:, gl]
        st = st_ref[g]
        y = _dot(c_g, st.astype(BF16)) * ecum_x[:, gl]
        st_ref[g] = st * dec_x[:, gl] + _dot_tn(b_g, (xdt * etail_x[:, gl]).astype(BF16))
        xdt_b = xdt.astype(BF16)
        for pr in range(SSM_HPG // 2):
            h0 = g * SSM_HPG + 2 * pr
            pl_ = slice(pr * LANES, (pr + 1) * LANES)
            ys = []
            for hh in (h0, h0 + 1):
                seg = cum[:, hh:hh + 1] - cum_t[hh:hh + 1, :]
                w = (cb * jnp.where(keep, jnp.exp(seg), 0.0)).astype(BF16)
                ys.append(_dot(w, xdt_b[:, pl_]))
            yp = y[:, pl_] + jnp.where(first, ys[0], ys[1])
            ol = slice(g * gw + pr * LANES, g * gw + (pr + 1) * LANES)
            if rev:
                yt = yp + yf_ref[:, ol].astype(F32) + dsk_ref[:, ol] * xc_ref[:, ol].astype(F32)
                stage_ref[:, ol] = yt * zs_ref[:, ol].astype(F32)
            else:
                y_ref[:, ol] = yp.astype(y_ref.dtype)
        if rev:
            o_ref[:, gl] = (_rms(stage_ref[:, gl]) * nw_ref[:, gl]).astype(o_ref.dtype)


def _ssd_scan(rev, xbc, dt_raw, dt_bias, a_log, *, conv_w=None, conv_b=None, zs=None, xc=None, y_f=None, d_skip=None,
              norm_w=None):
    m = dt_raw.shape[0]
    T = TOKEN_BLOCK
    nb = m // T
    blk = functools.partial(_scan_block, nb=nb, rev=rev)
    hpb = T // BF16_ROWS
    dt_spec = pl.BlockSpec((T, LANES), lambda i: (blk(i), 1 if rev else 0))
    state = pltpu.VMEM((SSM_GROUPS, SSM_STATE, SSM_HPG * SSM_HEADDIM), F32)
    if rev:
        in_specs = [pl.BlockSpec((T, SSM_XBC), lambda i: (blk(i), 0)), dt_spec, _resident((1, LANES)), _resident((1, LANES)),
                    pl.BlockSpec((T, SSM_D_INNER), lambda i: (blk(i), 0)),
                    pl.BlockSpec((T, SSM_D_INNER), lambda i: (blk(i), 0)),
                    _resident((1, SSM_D_INNER)), _resident((1, SSM_D_INNER))]
        args = [xc, dt_raw, dt_bias, a_log, zs, y_f, d_skip, norm_w]
        out_shape = jax.ShapeDtypeStruct((m, SSM_D_INNER), BF16)
        out_specs = pl.BlockSpec((T, SSM_D_INNER), lambda i: (blk(i), 0))
        scratch = [state, pltpu.VMEM((T, SSM_D_INNER), F32)]
    else:
        prev = lambda i: jnp.maximum(blk(i) * hpb - 1, 0)
        nxt = lambda i: jnp.minimum((blk(i) + 1) * hpb, m // BF16_ROWS - 1)
        bc_col = SSM_D_INNER // SSM_BC
        in_specs = [pl.BlockSpec((T, SSM_D_INNER), lambda i: (blk(i), 0)),
                    pl.BlockSpec((T, SSM_BC), lambda i: (blk(i), bc_col)),
                    pl.BlockSpec((BF16_ROWS, SSM_D_INNER), lambda i: (prev(i), 0)),
                    pl.BlockSpec((BF16_ROWS, SSM_D_INNER), lambda i: (nxt(i), 0)),
                    pl.BlockSpec((BF16_ROWS, SSM_BC), lambda i: (prev(i), bc_col)),
                    pl.BlockSpec((BF16_ROWS, SSM_BC), lambda i: (nxt(i), bc_col)),
                    _resident((SSM_CONV, SSM_XBC)), _resident((1, SSM_XBC)), dt_spec,
                    _resident((1, LANES)), _resident((1, LANES))]
        args = [xbc, xbc, xbc, xbc, xbc, xbc, conv_w, conv_b, dt_raw, dt_bias, a_log]
        out_shape = (jax.ShapeDtypeStruct((m, SSM_D_INNER), BF16),
                     jax.ShapeDtypeStruct((m, SSM_XBC), BF16))
        out_specs = (pl.BlockSpec((T, SSM_D_INNER), lambda i: (blk(i), 0)),
                     pl.BlockSpec((T, SSM_XBC), lambda i: (blk(i), 0)))
        scratch = [state, pltpu.VMEM((T + 2 * BF16_ROWS, SSM_D_INNER), F32)]
    return pl.pallas_call(
        functools.partial(_ssd_kernel, rev=rev, nb=nb),
        grid=(nb,),
        in_specs=in_specs,
        out_specs=out_specs,
        out_shape=out_shape,
        scratch_shapes=scratch,
        compiler_params=_cparams(("arbitrary",)),
    )(*args)


def _pool_kernel(x_ref, xp_ref, xn_ref, nw_ref, mod_ref, o_ref, ext_ref, *, nb, n_lat, n_ctx):
    T = TOKEN_BLOCK
    b = pl.program_id(0)
    is_ctx = b == nb - 1
    vp = jnp.where((b > 0) & (b < nb - 1), 1.0, 0.0)
    vn = jnp.where(b < nb - 2, 1.0, 0.0)
    norm = lambda x: _modnorm(x, nw_ref[...], mod_ref, SH1, SC1, is_ctx)
    ext_ref[0:SUBLANES] = norm(xp_ref[...]) * vp
    ext_ref[SUBLANES:SUBLANES + T] = norm(x_ref[...])
    ext_ref[SUBLANES + T:] = norm(xn_ref[...]) * vn
    t = jnp.where(is_ctx, 0, b * T) + lax.broadcasted_iota(jnp.int32, (T, 1), 0)
    seq = jnp.where(is_ctx, n_ctx, n_lat)
    for gi, win in enumerate(POOL_WINDOWS):
        sl = slice(gi * POOL_GROUP, (gi + 1) * POOL_GROUP)
        s0 = SUBLANES - win // 2
        acc = ext_ref[s0:s0 + T, sl]
        for jw in range(1, win):
            acc = acc + ext_ref[s0 + jw:s0 + jw + T, sl]
        cnt = jnp.minimum(t + (win - win // 2), seq) - jnp.maximum(t - win // 2, 0)
        o_ref[:, sl] = (acc / cnt.astype(F32) - ext_ref[SUBLANES:SUBLANES + T, sl]).astype(o_ref.dtype)


def _pool_delta(x, nw, mods, layer, n_lat, n_ctx):
    m = x.shape[0]
    T = TOKEN_BLOCK
    nb = m // T
    row8 = T // SUBLANES
    return pl.pallas_call(
        functools.partial(_pool_kernel, nb=nb, n_lat=n_lat, n_ctx=n_ctx),
        grid=(nb,),
        in_specs=[pl.BlockSpec((T, D_MODEL), lambda i: (i, 0)),
                  pl.BlockSpec((SUBLANES, D_MODEL), lambda i: (jnp.maximum(i * row8 - 1, 0), 0)),
                  pl.BlockSpec((SUBLANES, D_MODEL), lambda i: (jnp.minimum((i + 1) * row8, m // SUBLANES - 1), 0)),
                  _resident((1, D_MODEL)), _mod_spec(layer)],
        out_specs=pl.BlockSpec((T, D_MODEL), lambda i: (i, 0)),
        out_shape=jax.ShapeDtypeStruct((m, D_MODEL), BF16),
        scratch_shapes=[pltpu.VMEM((T + 2 * SUBLANES, D_MODEL), F32)],
        compiler_params=_cparams(("parallel",)),
    )(x, x, x, nw, mods)


def _attn_kernel(q_ref, kc_ref, kp_ref, kn_ref, kx_ref, sink_ref, o_ref, *, n_lat):
    T = TOKEN_BLOCK
    W = ATT_WINDOW
    i = pl.program_id(0)
    kv = jnp.concatenate([kp_ref[...], kc_ref[...], kn_ref[...], kx_ref[...]], axis=0)
    span = T + 2 * W
    nk = kv.shape[0]
    rows = ATT_GRP * T
    qpos = i * T + lax.broadcasted_iota(jnp.int32, (rows, nk), 0) % T
    col = lax.broadcasted_iota(jnp.int32, (rows, nk), 1)
    kpos = i * T - W + col
    valid = (col >= span) | ((jnp.abs(kpos - qpos) <= W) & (kpos >= 0) & (kpos < n_lat))
    head_row = lax.broadcasted_iota(jnp.int32, (rows, 1), 0) // T
    kd = ATT_KV_HEADS * ATT_HEAD_DIM
    for kh in range(ATT_KV_HEADS):
        k = kv[:, kh * ATT_HEAD_DIM:(kh + 1) * ATT_HEAD_DIM]
        v = kv[:, kd + kh * ATT_HEAD_DIM:kd + (kh + 1) * ATT_HEAD_DIM]
        q = jnp.concatenate([q_ref[:, (kh * ATT_GRP + g) * ATT_HEAD_DIM:(kh * ATT_GRP + g + 1) * ATT_HEAD_DIM]
                             for g in range(ATT_GRP)], axis=0)
        s = jnp.where(valid, _dot_nt(q, k), NEG_INF)
        sink = jnp.zeros((rows, 1), F32)
        for g in range(ATT_GRP):
            sink = jnp.where(head_row == g, sink_ref[kh * ATT_GRP + g], sink)
        mx = jnp.maximum(jnp.max(s, axis=-1, keepdims=True), sink)
        p = jnp.exp(s - mx)
        den = jnp.sum(p, axis=-1, keepdims=True) + jnp.exp(sink - mx)
        o = _dot(p.astype(BF16), v) / den
        for g in range(0, ATT_GRP, 2):
            pair = jnp.concatenate([o[g * T:(g + 1) * T], o[(g + 1) * T:(g + 2) * T]], axis=-1)
            c0 = (kh * ATT_GRP + g) * ATT_HEAD_DIM
            o_ref[:, c0:c0 + 2 * ATT_HEAD_DIM] = pair.astype(o_ref.dtype)


def _attention(qkv, sink, n_lat, n_ctx):
    T = TOKEN_BLOCK
    W = ATT_WINDOW
    nq = n_lat // T
    kvw = 2 * ATT_KV_HEADS * ATT_HEAD_DIM
    kv_col = D_MODEL // kvw
    wpb = T // W
    return pl.pallas_call(
        functools.partial(_attn_kernel, n_lat=n_lat),
        grid=(nq,),
        in_specs=[pl.BlockSpec((T, D_MODEL), lambda i: (i, 0)),
                  pl.BlockSpec((T, kvw), lambda i: (i, kv_col)),
                  pl.BlockSpec((W, kvw), lambda i: (jnp.maximum(i * wpb - 1, 0), kv_col)),
                  pl.BlockSpec((W, kvw), lambda i: (jnp.minimum((i + 1) * wpb, n_lat // W - 1), kv_col)),
                  pl.BlockSpec((n_ctx, kvw), lambda i: (n_lat // n_ctx, kv_col)),
                  pl.BlockSpec(memory_space=pltpu.SMEM)],
        out_specs=pl.BlockSpec((T, D_MODEL), lambda i: (i, 0)),
        out_shape=jax.ShapeDtypeStruct((n_lat, D_MODEL), BF16),
        compiler_params=_cparams(("parallel",)),
    )(qkv, qkv, qkv, qkv, qkv, sink)


def _rope_tables(n_lat, n_ctx):
    half = ATT_HEAD_DIM // 2
    inv = ROPE_THETA ** (-jnp.arange(0, half, 2, dtype=F32) / half)
    t = jnp.arange(n_lat)
    ang = jnp.concatenate([(t // GRID_W).astype(F32)[:, None] * inv[None]] * 2
                          + [(t % GRID_W).astype(F32)[:, None] * inv[None]] * 2, axis=-1)
    cos, sin = jnp.cos(ang), jnp.sin(ang)
    first = (jnp.arange(ATT_HEAD_DIM) % half) < half // 2
    tabs = [cos, jnp.where(first, -sin, 0.0), jnp.where(first, 0.0, sin)]
    ident = [jnp.ones((n_ctx, ATT_HEAD_DIM), F32), jnp.zeros((n_ctx, ATT_HEAD_DIM), F32), jnp.zeros((n_ctx, ATT_HEAD_DIM), F32)]
    reps = LANES // ATT_HEAD_DIM
    return [jnp.tile(jnp.concatenate([a, b], axis=0), (1, reps)) for a, b in zip(tabs, ident)]


def kernel(x, c, ctx, c_ctx, ada_w, ada_b, norm_w, ffn_w_in, ffn_w_out, final_norm_w, hg_w_in, hg_lb, hg_norm_w, hg_w_out, ssm_w_in, ssm_conv_w, ssm_conv_b, ssm_dt_bias, ssm_a_log, ssm_d, ssm_norm_w, ssm_w_out, pool_w, pool_scale, att_w_qkv, att_sink, att_w_out):
    assert x.shape[0] == 1 and DEPTH == 4
    n_lat, n_ctx = x.shape[1], ctx.shape[1]
    assert n_lat % TOKEN_BLOCK == 0 and n_ctx == TOKEN_BLOCK
    xs = jnp.concatenate([x[0], ctx[0]], axis=0)
    m = xs.shape[0]
    mods = _ada_table(c, c_ctx, ada_w, ada_b)
    lb_tab = jnp.cumsum(jax.nn.softmax(hg_lb.astype(F32), axis=1), axis=1)
    row = lambda v: v.reshape(1, -1)
    ffn = lambda l: (ffn_w_in[l].astype(BF16), ffn_w_out[l].astype(BF16))
    sds = lambda w, dt: jax.ShapeDtypeStruct((m, w), dt)
    tile = lambda w: (lambda tm: pl.BlockSpec((tm, w), lambda i: (i, 0)))

    w = hg_w_in[0].astype(BF16)
    qs, cumf, kf, cumb, kb, v, gs = _proj_call(
        _proj_hgrn_kernel, xs, row(norm_w[0, 0]), mods, 0, [w, lb_tab[:, 0]], [_resident(w.shape), _resident((2, D_MODEL))],
        [sds(D_MODEL, BF16), sds(D_MODEL, F32), sds(D_MODEL, BF16), sds(D_MODEL, F32), sds(D_MODEL, BF16),
         sds(D_MODEL, BF16), sds(D_MODEL, BF16)], n_lat)
    o_f = _hgrn_scan(qs, cumf, kf, v, False)
    a = _hgrn_scan(qs, cumb, kb, v, True, gs=gs, o_f=o_f, gn_w=row(hg_norm_w[0]))
    xs = _mix_ffn(xs, a, hg_w_out[0].astype(BF16), mods, 0, row(norm_w[0, 1]), *ffn(0), n_lat, m)

    zx = SSM_D_INNER + SSM_XBC
    w = ssm_w_in[0][:, :zx].astype(BF16)
    pad = jnp.zeros((D_MODEL, LANES - SSM_HEADS), F32)
    wdt = jnp.concatenate([ssm_w_in[0][:, zx:zx + SSM_HEADS], pad, ssm_w_in[0][:, zx + SSM_HEADS:], pad], axis=1).astype(BF16)
    zs, xbc, dt_raw = _proj_call(
        _proj_ssd_kernel, xs, row(norm_w[1, 0]), mods, 1, [w, wdt], [_resident(w.shape), _resident(wdt.shape)],
        [sds(SSM_D_INNER, BF16), sds(SSM_XBC, BF16), sds(2 * LANES, F32)], n_lat)
    lane_pad = lambda v: jnp.concatenate([v, jnp.zeros((LANES - SSM_HEADS,), F32)]).reshape(1, LANES)
    y_f, xc = _ssd_scan(False, xbc, dt_raw, lane_pad(ssm_dt_bias[0, 0]), lane_pad(ssm_a_log[0, 0]),
                        conv_w=ssm_conv_w[0], conv_b=row(ssm_conv_b[0]))
    a = _ssd_scan(True, xbc, dt_raw, lane_pad(ssm_dt_bias[0, 1]), lane_pad(ssm_a_log[0, 1]), zs=zs, xc=xc, y_f=y_f,
                  d_skip=row(jnp.repeat(ssm_d[0], SSM_HEADDIM)), norm_w=row(ssm_norm_w[0]))
    xs = _mix_ffn(xs, a, ssm_w_out[0].astype(BF16), mods, 1, row(norm_w[1, 1]), *ffn(1), n_lat, m)

    a = _pool_delta(xs, row(norm_w[2, 0]), mods, 2, n_lat, n_ctx)
    xs = _mix_ffn(xs, a, pool_w[0].astype(BF16), mods, 2, row(norm_w[2, 1]), *ffn(2), n_lat, m,
                  pool_scale=row(pool_scale[0]))

    rope = _rope_tables(n_lat, n_ctx)
    qd = ATT_Q_HEADS * ATT_HEAD_DIM
    kd = ATT_KV_HEADS * ATT_HEAD_DIM
    wq = jnp.concatenate([att_w_qkv[0][:, :qd] * ATT_HEAD_DIM ** -0.5, att_w_qkv[0][:, qd:]], axis=1).astype(BF16)
    (qkv,) = _proj_call(
        _proj_attn_kernel, xs, row(norm_w[3, 0]), mods, 3, [wq] + rope, [_resident(wq.shape)] + [tile(LANES)] * 3,
        [sds(qd + 2 * kd, BF16)], n_lat, rope_cols=qd + kd)
    a = _attention(qkv, att_sink[0], n_lat, n_ctx)
    out = _mix_ffn(xs, a, att_w_out[0].astype(BF16), mods, 3, row(norm_w[3, 1]), *ffn(3), n_lat, n_lat,
                   final_w=row(final_norm_w))
    return out[None]
```

```python
import functools

import jax
import jax.numpy as jnp
from jax import lax
from jax.experimental import pallas as pl
from jax.experimental.pallas import tpu as pltpu

F32 = jnp.float32
BF16 = jnp.bfloat16

D_MODEL = 1024
DEPTH = 4
GRID_W = 64
EPS = 1e-6
NEG_INF = -1e30

HG_HEADS = 8
HG_DK = 128
HG_CHUNK = 64

SSM_D_INNER = 2 * D_MODEL
SSM_HEADDIM = 64
SSM_HEADS = SSM_D_INNER // SSM_HEADDIM
SSM_GROUPS = 4
SSM_STATE = 128
SSM_CONV = 5
SSM_BC = 2 * SSM_GROUPS * SSM_STATE
SSM_XBC = SSM_D_INNER + SSM_BC
SSM_HPG = SSM_HEADS // SSM_GROUPS

POOL_WINDOWS = (2, 4, 8, 16)
POOL_GROUP = D_MODEL // len(POOL_WINDOWS)

ATT_HEAD_DIM = 64
ATT_Q_HEADS = D_MODEL // ATT_HEAD_DIM
ATT_KV_HEADS = 4
ATT_GRP = ATT_Q_HEADS // ATT_KV_HEADS
ATT_WINDOW = 128
ROPE_THETA = 10000.0

FFN_HIDDEN = -(-8 * D_MODEL // (3 * 256)) * 256

LANES = 128
SUBLANES = 8
BF16_ROWS = 16
MXU_COLS = 256
TOKEN_BLOCK = 256
ROW_TILE = 640
VMEM_LIMIT = 56 * 1024 * 1024

SH1, SC1, G1, SH2, SC2, G2 = range(6)


def _cparams(sem):
    return pltpu.CompilerParams(dimension_semantics=sem, vmem_limit_bytes=VMEM_LIMIT)


def _row_tile(m, target):
    best = None
    for t in range(HG_CHUNK, target + 1, HG_CHUNK):
        if m % t == 0:
            best = t
    assert best is not None
    return best


def _resident(shape):
    zeros = (0,) * len(shape)
    return pl.BlockSpec(shape, lambda *_: zeros, pipeline_mode=pl.Buffered(1))


def _dot(a, b):
    return jnp.dot(a, b, preferred_element_type=F32)


def _dot_nt(a, b):
    return lax.dot_general(a, b, (((1,), (1,)), ((), ())), preferred_element_type=F32)


def _dot_tn(a, b):
    return lax.dot_general(a, b, (((0,), (0,)), ((), ())), preferred_element_type=F32)


def _split2(x):
    hi = x.astype(BF16)
    lo = (x - hi.astype(F32)).astype(BF16)
    return hi, lo


def _split3(x):
    hi = x.astype(BF16)
    r = x - hi.astype(F32)
    mid = r.astype(BF16)
    lo = (r - mid.astype(F32)).astype(BF16)
    return hi, mid, lo


def _silu(x):
    return x * jax.nn.sigmoid(x)


def _mod_rows(mod_ref, slot, is_ctx):
    lo, hi = slot * D_MODEL, (slot + 1) * D_MODEL
    return jnp.where(is_ctx, mod_ref[1:2, lo:hi], mod_ref[0:1, lo:hi])


def _rms(x):
    return x * lax.rsqrt(jnp.mean(x * x, axis=-1, keepdims=True) + EPS)


def _modnorm(x, nw, mod_ref, sh_slot, sc_slot, is_ctx):
    y = _rms(x) * nw
    return y * (1.0 + _mod_rows(mod_ref, sc_slot, is_ctx)) + _mod_rows(mod_ref, sh_slot, is_ctx)


def _is_ctx_rows(row0, rows, n_lat):
    return (row0 + lax.broadcasted_iota(jnp.int32, (rows, 1), 0)) >= n_lat


def _mod_spec(layer):
    return pl.BlockSpec((None, SUBLANES, 6 * D_MODEL), lambda *_: (layer, 0, 0))


def _ada_kernel(c_ref, w_ref, b_ref, o_ref):
    a = _silu(c_ref[...])
    a_hi, a_lo = _split2(a)
    w_hi, w_lo = _split2(w_ref[...])
    o_ref[...] = _dot(a_hi, w_hi) + _dot(a_hi, w_lo) + _dot(a_lo, w_hi) + b_ref[...]


def _ada_table(c, c_ctx, ada_w, ada_b):
    cc = jnp.concatenate([c[0:1], c_ctx[None], jnp.zeros((SUBLANES - 2, D_MODEL), F32)], axis=0)
    n = 6 * D_MODEL
    tn = n // 4
    return pl.pallas_call(
        _ada_kernel,
        grid=(DEPTH, n // tn),
        in_specs=[pl.BlockSpec((SUBLANES, D_MODEL), lambda l, j: (0, 0)),
                  pl.BlockSpec((None, D_MODEL, tn), lambda l, j: (l, 0, j)),
                  pl.BlockSpec((None, 1, tn), lambda l, j: (l, 0, j))],
        out_specs=pl.BlockSpec((None, SUBLANES, tn), lambda l, j: (l, 0, j)),
        out_shape=jax.ShapeDtypeStruct((DEPTH, SUBLANES, n), F32),
        compiler_params=_cparams(("parallel", "parallel")),
    )(cc, ada_w, ada_b.reshape(DEPTH, 1, n))


def _cumsum_rows(x, rev):
    n = x.shape[0]
    row = lax.broadcasted_iota(jnp.int32, x.shape, 0)
    k = 1
    while k < n:
        if rev:
            x = x + jnp.where(row < n - k, pltpu.roll(x, n - k, axis=0), 0.0)
        else:
            x = x + jnp.where(row >= k, pltpu.roll(x, k, axis=0), 0.0)
        k *= 2
    return x


def _normed_rows(x_ref, nw_ref, mod_ref, tm, n_lat):
    is_ctx = _is_ctx_rows(pl.program_id(0) * tm, tm, n_lat)
    return _modnorm(x_ref[...], nw_ref[...], mod_ref, SH1, SC1, is_ctx).astype(BF16)


def _proj_hgrn_kernel(x_ref, nw_ref, mod_ref, w_ref, lb_ref, qs_ref, cumf_ref, kf_ref, cumb_ref, kb_ref, v_ref, gs_ref,
                      *, tm, n_lat):
    h = _normed_rows(x_ref, nw_ref, mod_ref, tm, n_lat)
    col = lambda k: slice(k * D_MODEL, (k + 1) * D_MODEL)
    qs_ref[...] = _silu(_dot(h, w_ref[:, col(0)])).astype(qs_ref.dtype)
    for d, (cum_ref, k_ref) in enumerate(((cumf_ref, kf_ref), (cumb_ref, kb_ref))):
        lb = lb_ref[d:d + 1, :]
        f = lb + (1.0 - lb) * jax.nn.sigmoid(_dot(h, w_ref[:, col(1 + d)]))
        k_ref[...] = (1.0 - f).astype(k_ref.dtype)
        lf = jnp.log(f)
        for c in range(tm // HG_CHUNK):
            rows = slice(c * HG_CHUNK, (c + 1) * HG_CHUNK)
            cum_ref[rows, :] = _cumsum_rows(lf[rows], rev=bool(d))
    v_ref[...] = _dot(h, w_ref[:, col(3)]).astype(v_ref.dtype)
    gs_ref[...] = _silu(_dot(h, w_ref[:, col(4)])).astype(gs_ref.dtype)


def _conv_silu(taps, cw_ref, cb_ref, cols):
    acc = cb_ref[:, cols] + cw_ref[0:1, cols] * taps[0]
    for j in range(1, SSM_CONV):
        acc = acc + cw_ref[j:j + 1, cols] * taps[j]
    return _silu(acc)


def _proj_ssd_kernel(x_ref, xp_ref, xn_ref, nw_ref, mod_ref, w_ref, wdt_ref, cw_ref, cb_ref, zs_ref, xc_ref, dt_ref,
                     ext_ref, *, tm, n_lat, n_rows):
    i = pl.program_id(0)
    nt = n_rows // tm
    halo = SUBLANES
    half = SSM_CONV // 2
    norm = lambda ref, row0, rows: _modnorm(ref[...], nw_ref[...], mod_ref, SH1, SC1, _is_ctx_rows(row0, rows, n_lat))
    hf = norm(x_ref, i * tm, tm)
    h = hf.astype(BF16)
    for c in range(SSM_D_INNER // D_MODEL):
        sl = slice(c * D_MODEL, (c + 1) * D_MODEL)
        zs_ref[:, sl] = _silu(_dot(h, w_ref[:, sl])).astype(zs_ref.dtype)
    dt_ref[...] = _dot(h, wdt_ref[...])
    h_ext = jnp.concatenate([norm(xp_ref, i * tm - halo, halo), hf, norm(xn_ref, (i + 1) * tm, halo)], axis=0).astype(BF16)
    vp = jnp.where(i > 0, 1.0, 0.0)
    vn = jnp.where(i < nt - 1, 1.0, 0.0)
    chunks = [slice(c * D_MODEL, (c + 1) * D_MODEL) for c in range(SSM_XBC // D_MODEL)]
    for cols in chunks:
        y = _dot(h_ext, w_ref[:, SSM_D_INNER + cols.start:SSM_D_INNER + cols.stop])
        ext_ref[0:halo, cols] = y[0:halo] * vp
        ext_ref[halo:halo + tm, cols] = y[halo:halo + tm]
        ext_ref[halo + tm:, cols] = y[halo + tm:] * vn
    for cols in chunks:
        taps = [ext_ref[halo - half + j:halo - half + j + tm, cols] for j in range(SSM_CONV)]
        xc_ref[:, cols] = _conv_silu(taps, cw_ref, cb_ref, cols).astype(xc_ref.dtype)
    if n_lat % tm:
        bt, off = divmod(n_lat, tm)
        fix = BF16_ROWS
        assert off % fix == 0 and fix <= off <= tm - fix

        @pl.when(i == bt)
        def _():
            e0 = halo + off
            is_lat_row = lax.broadcasted_iota(jnp.int32, (2 * fix + 2 * halo, 1), 0) < fix + halo
            for cols in chunks:
                win = ext_ref[e0 - fix - halo:e0 + fix + halo, cols]
                lat_w = jnp.where(is_lat_row, win, 0.0)
                ctx_w = jnp.where(is_lat_row, 0.0, win)
                lat_taps = [lat_w[halo - half + j:halo - half + j + fix] for j in range(SSM_CONV)]
                ctx_taps = [ctx_w[halo + fix - half + j:halo + fix - half + j + fix] for j in range(SSM_CONV)]
                xc_ref[off - fix:off, cols] = _conv_silu(lat_taps, cw_ref, cb_ref, cols).astype(xc_ref.dtype)
                xc_ref[off:off + fix, cols] = _conv_silu(ctx_taps, cw_ref, cb_ref, cols).astype(xc_ref.dtype)


def _proj_attn_kernel(x_ref, nw_ref, mod_ref, w_ref, cos_ref, sa_ref, sb_ref, o_ref, *, tm, n_lat, rope_cols):
    h = _normed_rows(x_ref, nw_ref, mod_ref, tm, n_lat)
    n = w_ref.shape[1]
    for c in range(n // MXU_COLS):
        y = _dot(h, w_ref[:, c * MXU_COLS:(c + 1) * MXU_COLS])
        for s_ in range(MXU_COLS // LANES):
            c0 = c * MXU_COLS + s_ * LANES
            ys = y[:, s_ * LANES:(s_ + 1) * LANES]
            if c0 < rope_cols:
                ys = (ys * cos_ref[...] + pltpu.roll(ys, LANES - 16, axis=1) * sa_ref[...]
                      + pltpu.roll(ys, 16, axis=1) * sb_ref[...])
            o_ref[:, c0:c0 + LANES] = ys.astype(o_ref.dtype)


def _proj_call(kernel_fn, x, nw, mods, layer, extra_in, extra_specs, out_shapes, n_lat, scratch=None, halo=False, **kw):
    m = x.shape[0]
    tm = _row_tile(m, ROW_TILE)
    row = lambda w: pl.BlockSpec((tm, w), lambda i: (i, 0))
    lead, lead_specs = [x], [row(D_MODEL)]
    if halo:
        tpb = tm // SUBLANES
        lead += [x, x]
        lead_specs += [pl.BlockSpec((SUBLANES, D_MODEL), lambda i: (jnp.maximum(i * tpb - 1, 0), 0)),
                       pl.BlockSpec((SUBLANES, D_MODEL), lambda i: (jnp.minimum((i + 1) * tpb, m // SUBLANES - 1), 0))]
    return pl.pallas_call(
        functools.partial(kernel_fn, tm=tm, n_lat=n_lat, **kw),
        grid=(m // tm,),
        in_specs=lead_specs + [_resident((1, D_MODEL)), _mod_spec(layer)] + [s(tm) if callable(s) else s for s in extra_specs],
        out_specs=[row(s.shape[1]) for s in out_shapes],
        out_shape=out_shapes,
        scratch_shapes=scratch(tm) if scratch else [],
        compiler_params=_cparams(("parallel",)),
    )(*lead, nw, mods, *extra_in)


def _ffn_chunks():
    out, c0 = [], 0
    while c0 < FFN_HIDDEN:
        cw = min(D_MODEL, FFN_HIDDEN - c0)
        assert cw % MXU_COLS == 0
        out.append((c0, cw))
        c0 += cw
    return out


def _mix_ffn_kernel(x_ref, a_ref, wo_ref, *rest, tm, n_lat, pool, final):
    rest = list(rest)
    ps_ref = rest.pop(0) if pool else None
    mod_ref, nw_ref, win_ref, wout_ref = rest[:4]
    rest = rest[4:]
    fw_ref = rest.pop(0) if final else None
    (o_ref,) = rest
    is_ctx = _is_ctx_rows(pl.program_id(0) * tm, tm, n_lat)
    if pool:
        parts = [_dot(a_ref[:, g * POOL_GROUP:(g + 1) * POOL_GROUP], wo_ref[g]) for g in range(len(POOL_WINDOWS))]
        y = jnp.concatenate(parts, axis=-1) * ps_ref[...]
    else:
        y = _dot(a_ref[...], wo_ref[...])
    x1 = x_ref[...] + _mod_rows(mod_ref, G1, is_ctx) * y
    h = _modnorm(x1, nw_ref[...], mod_ref, SH2, SC2, is_ctx).astype(BF16)
    acc = None
    for c0, cw in _ffn_chunks():
        gate = _dot(h, win_ref[:, c0:c0 + cw])
        up = _dot(h, win_ref[:, FFN_HIDDEN + c0:FFN_HIDDEN + c0 + cw])
        part = _dot((_silu(gate) * up).astype(BF16), wout_ref[c0:c0 + cw, :])
        acc = part if acc is None else acc + part
    x2 = x1 + _mod_rows(mod_ref, G2, is_ctx) * acc
    o_ref[...] = _rms(x2) * fw_ref[...] if final else x2


def _mix_ffn(x, a, wo, mods, layer, nw2, w_in, w_out, n_lat, rows, pool_scale=None, final_w=None):
    pool = pool_scale is not None
    final = final_w is not None
    tm = _row_tile(rows, ROW_TILE)
    row = lambda w: pl.BlockSpec((tm, w), lambda i: (i, 0))
    in_specs = [row(D_MODEL), row(a.shape[1]), _resident(wo.shape)]
    args = [x, a, wo]
    if pool:
        in_specs.append(_resident((1, D_MODEL)))
        args.append(pool_scale)
    in_specs += [_mod_spec(layer), _resident((1, D_MODEL)), _resident(w_in.shape), _resident(w_out.shape)]
    args += [mods, nw2, w_in, w_out]
    if final:
        in_specs.append(_resident((1, D_MODEL)))
        args.append(final_w)
    return pl.pallas_call(
        functools.partial(_mix_ffn_kernel, tm=tm, n_lat=n_lat, pool=pool, final=final),
        grid=(rows // tm,),
        in_specs=in_specs,
        out_specs=row(D_MODEL),
        out_shape=jax.ShapeDtypeStruct((rows, D_MODEL), F32),
        compiler_params=_cparams(("parallel",)),
    )(*args)


def _scan_block(i, nb, rev):
    if rev:
        return jnp.where(i == 0, nb - 1, nb - 1 - i)
    return jnp.where(i == 0, nb - 1, i - 1)


def _hgrn_kernel(q_ref, cum_ref, k_ref, v_ref, *rest, rev):
    if rev:
        g_ref, of_ref, gn_ref, o_ref, st_ref = rest
    else:
        o_ref, st_ref = rest
    L = HG_CHUNK
    nch = TOKEN_BLOCK // L

    @pl.when(pl.program_id(0) == 0)
    def _():
        st_ref[...] = jnp.zeros_like(st_ref)

    r_i = lax.broadcasted_iota(jnp.int32, (L, L), 0)
    c_i = lax.broadcasted_iota(jnp.int32, (L, L), 1)
    keep = (r_i <= c_i) if rev else (r_i >= c_i)
    mid = L // 2 if rev else L // 2 - 1
    end = 0 if rev else L - 1

    heads = [slice(h * HG_DK, (h + 1) * HG_DK) for h in range(HG_HEADS)]
    order = [(c, slice(c * L, (c + 1) * L)) for c in (reversed(range(nch)) if rev else range(nch))]
    att_l, ds_l = {}, {}
    for c, rows in order:
        for h, sl in enumerate(heads):
            q = q_ref[rows, sl].astype(F32)
            k = k_ref[rows, sl].astype(F32)
            cum = cum_ref[rows, sl]
            ref = cum[mid:mid + 1]
            last = cum[end:end + 1]
            att_l[c, h] = _dot_nt((q * jnp.exp(cum - ref)).astype(BF16), (k * jnp.exp(ref - cum)).astype(BF16))
            ds_l[c, h] = _dot_tn(v_ref[rows, sl], (k * jnp.exp(last - cum)).astype(BF16))
    for c, rows in order:
        for h, sl in enumerate(heads):
            cum = cum_ref[rows, sl]
            st = st_ref[h]
            inter = _dot_nt((q_ref[rows, sl].astype(F32) * jnp.exp(cum)).astype(BF16), st.astype(BF16))
            att = jnp.where(keep, att_l[c, h], 0.0).astype(BF16)
            o = inter + _dot(att, v_ref[rows, sl])
            st_ref[h] = st * jnp.exp(cum[end:end + 1]) + ds_l[c, h]
            if rev:
                o = o + of_ref[rows, sl].astype(F32)
                y = _rms(o) * gn_ref[...]
                o_ref[rows, sl] = (y * g_ref[rows, sl].astype(F32)).astype(o_ref.dtype)
            else:
                o_ref[rows, sl] = o.astype(o_ref.dtype)


def _hgrn_scan(qs, cum, k, v, rev, gs=None, o_f=None, gn_w=None):
    m = qs.shape[0]
    nb = m // TOKEN_BLOCK
    blk = functools.partial(_scan_block, nb=nb, rev=rev)
    spec = pl.BlockSpec((TOKEN_BLOCK, D_MODEL), lambda i: (blk(i), 0))
    in_specs = [spec] * 4
    args = [qs, cum, k, v]
    if rev:
        in_specs += [spec, spec, _resident((1, HG_DK))]
        args += [gs, o_f, gn_w]
    return pl.pallas_call(
        functools.partial(_hgrn_kernel, rev=rev),
        grid=(nb,),
        in_specs=in_specs,
        out_specs=spec,
        out_shape=jax.ShapeDtypeStruct((m, D_MODEL), BF16),
        scratch_shapes=[pltpu.VMEM((HG_HEADS, HG_DK, HG_DK), F32)],
        compiler_params=_cparams(("arbitrary",)),
    )(*args)


def _ssd_kernel(xc_ref, dt_ref, dtb_ref, alog_ref, *rest, rev):
    if rev:
        zs_ref, yf_ref, dsk_ref, nw_ref, o_ref, st_ref, stage_ref = rest
    else:
        y_ref, st_ref = rest
    T = TOKEN_BLOCK

    @pl.when(pl.program_id(0) == 0)
    def _():
        st_ref[...] = jnp.zeros_like(st_ref)

    dt = jax.nn.softplus(dt_ref[...] + dtb_ref[...])
    dta = dt * (-jnp.exp(alog_ref[...]))
    r_i = lax.broadcasted_iota(jnp.int32, (T, T), 0)
    c_i = lax.broadcasted_iota(jnp.int32, (T, T), 1)
    keep = (r_i <= c_i) if rev else (r_i >= c_i)
    tri = jnp.where(keep, 1.0, 0.0).astype(BF16)
    cum = sum(_dot(tri, p) for p in _split3(dta))
    end = 0 if rev else T - 1
    total = cum[end:end + 1]
    log2e = 1.4426950408889634
    col2 = cum * log2e
    row2_t = (col2 - jnp.log2(dt)).T

    half = LANES // 2
    e_h = lax.broadcasted_iota(jnp.int32, (LANES, SSM_D_INNER), 0)
    e_c = lax.broadcasted_iota(jnp.int32, (LANES, SSM_D_INNER), 1)
    expand_m = jnp.where((e_c // SSM_HEADDIM == e_h % half) & (e_h % half < SSM_HEADS), 1.0, 0.0).astype(BF16)

    def expand(f):
        hi = f.astype(BF16).astype(F32)
        lane_f = lax.broadcasted_iota(jnp.int32, f.shape, 1)
        return _dot(jnp.where(lane_f < half, hi, pltpu.roll(f - hi, half, axis=1)).astype(BF16), expand_m)

    ecum_x = expand(jnp.exp(cum))
    fac_x = expand(dt * jnp.exp(total - cum))
    dec_x = expand(jnp.broadcast_to(jnp.exp(total), (SUBLANES, LANES)))[0:1]

    lane = lax.broadcasted_iota(jnp.int32, (T, LANES), 1)
    first = lane < SSM_HEADDIM
    gw = SSM_HPG * SSM_HEADDIM
    groups = [slice(g * gw, (g + 1) * gw) for g in range(SSM_GROUPS)]
    cb_l, yoff_l = [], []
    for g, gl in enumerate(groups):
        b_g = xc_ref[:, SSM_D_INNER + g * SSM_STATE:SSM_D_INNER + (g + 1) * SSM_STATE]
        c_off = SSM_D_INNER + SSM_GROUPS * SSM_STATE + g * SSM_STATE
        c_g = xc_ref[:, c_off:c_off + SSM_STATE]
        cb_l.append(_dot_nt(c_g, b_g))
        st = st_ref[g]
        yoff_l.append(_dot(c_g, st.astype(BF16)))
        xs_fac = (xc_ref[:, gl].astype(F32) * fac_x[:, gl]).astype(BF16)
        st_ref[g] = st * dec_x[:, gl] + _dot_tn(b_g, xs_fac)
    hb = T // 2
    keep_h = keep[0:hb, 0:hb]
    lo_, hi_ = slice(0, hb), slice(hb, T)
    for g, gl in enumerate(groups):
        cb = cb_l[g]
        for pr in range(SSM_HPG // 2):
            h0 = g * SSM_HPG + 2 * pr
            ol = slice(g * gw + pr * LANES, g * gw + (pr + 1) * LANES)
            xs_p = xc_ref[:, ol]
            ys = []
            for hh in (h0, h0 + 1):
                col = jnp.broadcast_to(col2[:, hh:hh + 1], (T, hb))
                row = row2_t[hh:hh + 1, :]

                def tile(rs, cs, masked):
                    e = jnp.exp2(col[rs] - row[:, cs])
                    if masked:
                        e = jnp.where(keep_h, e, 0.0)
                    return (cb[rs, cs] * e).astype(BF16)

                if rev:
                    top = _dot(jnp.concatenate([tile(lo_, lo_, True), tile(lo_, hi_, False)], axis=1), xs_p)
                    bot = _dot(tile(hi_, hi_, True), xs_p[hb:])
                else:
                    top = _dot(tile(lo_, lo_, True), xs_p[:hb])
                    bot = _dot(jnp.concatenate([tile(hi_, lo_, False), tile(hi_, hi_, True)], axis=1), xs_p)
                ys.append(jnp.concatenate([top, bot], axis=0))
            yp = yoff_l[g][:, pr * LANES:(pr + 1) * LANES] * ecum_x[:, ol] + jnp.where(first, ys[0], ys[1])
            if rev:
                yt = yp + yf_ref[:, ol].astype(F32) + dsk_ref[:, ol] * xc_ref[:, ol].astype(F32)
                stage_ref[:, ol] = yt * zs_ref[:, ol].astype(F32)
            else:
                y_ref[:, ol] = yp.astype(y_ref.dtype)
        if rev:
            o_ref[:, gl] = (_rms(stage_ref[:, gl]) * nw_ref[:, gl]).astype(o_ref.dtype)


def _ssd_scan(rev, xc, dt_raw, dt_bias, a_log, zs=None, y_f=None, d_skip=None, norm_w=None):
    m = dt_raw.shape[0]
    T = TOKEN_BLOCK
    nb = m // T
    blk = functools.partial(_scan_block, nb=nb, rev=rev)
    rows = lambda w: pl.BlockSpec((T, w), lambda i: (blk(i), 0))
    in_specs = [rows(SSM_XBC), pl.BlockSpec((T, LANES), lambda i: (blk(i), 1 if rev else 0)),
                _resident((1, LANES)), _resident((1, LANES))]
    args = [xc, dt_raw, dt_bias, a_log]
    scratch = [pltpu.VMEM((SSM_GROUPS, SSM_STATE, SSM_HPG * SSM_HEADDIM), F32)]
    if rev:
        in_specs += [rows(SSM_D_INNER), rows(SSM_D_INNER), _resident((1, SSM_D_INNER)), _resident((1, SSM_D_INNER))]
        args += [zs, y_f, d_skip, norm_w]
        scratch.append(pltpu.VMEM((T, SSM_D_INNER), F32))
    return pl.pallas_call(
        functools.partial(_ssd_kernel, rev=rev),
        grid=(nb,),
        in_specs=in_specs,
        out_specs=rows(SSM_D_INNER),
        out_shape=jax.ShapeDtypeStruct((m, SSM_D_INNER), BF16),
        scratch_shapes=scratch,
        compiler_params=_cparams(("arbitrary",)),
    )(*args)


def _pool_kernel(x_ref, xp_ref, xn_ref, nw_ref, mod_ref, o_ref, ext_ref, *, nb, n_lat, n_ctx):
    T = TOKEN_BLOCK
    b = pl.program_id(0)
    is_ctx = b == nb - 1
    vp = jnp.where((b > 0) & (b < nb - 1), 1.0, 0.0)
    vn = jnp.where(b < nb - 2, 1.0, 0.0)
    norm = lambda x: _modnorm(x, nw_ref[...], mod_ref, SH1, SC1, is_ctx)
    ext_ref[0:SUBLANES] = norm(xp_ref[...]) * vp
    ext_ref[SUBLANES:SUBLANES + T] = norm(x_ref[...])
    ext_ref[SUBLANES + T:] = norm(xn_ref[...]) * vn
    t = jnp.where(is_ctx, 0, b * T) + lax.broadcasted_iota(jnp.int32, (T, 1), 0)
    seq = jnp.where(is_ctx, n_ctx, n_lat)
    for gi, win in enumerate(POOL_WINDOWS):
        sl = slice(gi * POOL_GROUP, (gi + 1) * POOL_GROUP)
        s0 = SUBLANES - win // 2
        acc = ext_ref[s0:s0 + T, sl]
        for jw in range(1, win):
            acc = acc + ext_ref[s0 + jw:s0 + jw + T, sl]
        cnt = jnp.minimum(t + (win - win // 2), seq) - jnp.maximum(t - win // 2, 0)
        o_ref[:, sl] = (acc / cnt.astype(F32) - ext_ref[SUBLANES:SUBLANES + T, sl]).astype(o_ref.dtype)


def _pool_delta(x, nw, mods, layer, n_lat, n_ctx):
    m = x.shape[0]
    T = TOKEN_BLOCK
    nb = m // T
    row8 = T // SUBLANES
    return pl.pallas_call(
        functools.partial(_pool_kernel, nb=nb, n_lat=n_lat, n_ctx=n_ctx),
        grid=(nb,),
        in_specs=[pl.BlockSpec((T, D_MODEL), lambda i: (i, 0)),
                  pl.BlockSpec((SUBLANES, D_MODEL), lambda i: (jnp.maximum(i * row8 - 1, 0), 0)),
                  pl.BlockSpec((SUBLANES, D_MODEL), lambda i: (jnp.minimum((i + 1) * row8, m // SUBLANES - 1), 0)),
                  _resident((1, D_MODEL)), _mod_spec(layer)],
        out_specs=pl.BlockSpec((T, D_MODEL), lambda i: (i, 0)),
        out_shape=jax.ShapeDtypeStruct((m, D_MODEL), BF16),
        scratch_shapes=[pltpu.VMEM((T + 2 * SUBLANES, D_MODEL), F32)],
        compiler_params=_cparams(("parallel",)),
    )(x, x, x, nw, mods)


def _attn_kernel(q_ref, kc_ref, kp_ref, kn_ref, kx_ref, sink_ref, o_ref, *, n_lat):
    T = TOKEN_BLOCK
    W = ATT_WINDOW
    i = pl.program_id(0)
    kv = jnp.concatenate([kp_ref[...], kc_ref[...], kn_ref[...], kx_ref[...]], axis=0)
    span = T + 2 * W
    nk = kv.shape[0]
    rows = ATT_GRP * T
    qpos = i * T + lax.broadcasted_iota(jnp.int32, (rows, nk), 0) % T
    col = lax.broadcasted_iota(jnp.int32, (rows, nk), 1)
    kpos = i * T - W + col
    valid = (col >= span) | ((jnp.abs(kpos - qpos) <= W) & (kpos >= 0) & (kpos < n_lat))
    head_row = lax.broadcasted_iota(jnp.int32, (rows, 1), 0) // T
    kd = ATT_KV_HEADS * ATT_HEAD_DIM
    s_l = []
    for kh in range(ATT_KV_HEADS):
        k = kv[:, kh * ATT_HEAD_DIM:(kh + 1) * ATT_HEAD_DIM]
        q = jnp.concatenate([q_ref[:, (kh * ATT_GRP + g) * ATT_HEAD_DIM:(kh * ATT_GRP + g + 1) * ATT_HEAD_DIM]
                             for g in range(ATT_GRP)], axis=0)
        s_l.append(_dot_nt(q, k))
    for kh in range(ATT_KV_HEADS):
        v = kv[:, kd + kh * ATT_HEAD_DIM:kd + (kh + 1) * ATT_HEAD_DIM]
        s = jnp.where(valid, s_l[kh], NEG_INF)
        sink = jnp.zeros((rows, 1), F32)
        for g in range(ATT_GRP):
            sink = jnp.where(head_row == g, sink_ref[kh * ATT_GRP + g], sink)
        mx = jnp.maximum(jnp.max(s, axis=-1, keepdims=True), sink)
        p = jnp.exp(s - mx)
        den = jnp.sum(p, axis=-1, keepdims=True) + jnp.exp(sink - mx)
        o = _dot(p.astype(BF16), v) / den
        for g in range(0, ATT_GRP, 2):
            pair = jnp.concatenate([o[g * T:(g + 1) * T], o[(g + 1) * T:(g + 2) * T]], axis=-1)
            c0 = (kh * ATT_GRP + g) * ATT_HEAD_DIM
            o_ref[:, c0:c0 + 2 * ATT_HEAD_DIM] = pair.astype(o_ref.dtype)


def _attention(qkv, sink, n_lat, n_ctx):
    T = TOKEN_BLOCK
    W = ATT_WINDOW
    nq = n_lat // T
    kvw = 2 * ATT_KV_HEADS * ATT_HEAD_DIM
    kv_col = D_MODEL // kvw
    wpb = T // W
    return pl.pallas_call(
        functools.partial(_attn_kernel, n_lat=n_lat),
        grid=(nq,),
        in_specs=[pl.BlockSpec((T, D_MODEL), lambda i: (i, 0)),
                  pl.BlockSpec((T, kvw), lambda i: (i, kv_col)),
                  pl.BlockSpec((W, kvw), lambda i: (jnp.maximum(i * wpb - 1, 0), kv_col)),
                  pl.BlockSpec((W, kvw), lambda i: (jnp.minimum((i + 1) * wpb, n_lat // W - 1), kv_col)),
                  pl.BlockSpec((n_ctx, kvw), lambda i: (n_lat // n_ctx, kv_col)),
                  pl.BlockSpec(memory_space=pltpu.SMEM)],
        out_specs=pl.BlockSpec((T, D_MODEL), lambda i: (i, 0)),
        out_shape=jax.ShapeDtypeStruct((n_lat, D_MODEL), BF16),
        compiler_params=_cparams(("parallel",)),
    )(qkv, qkv, qkv, qkv, qkv, sink)


def _rope_tables(n_lat, n_ctx):
    half = ATT_HEAD_DIM // 2
    inv = ROPE_THETA ** (-jnp.arange(0, half, 2, dtype=F32) / half)
    grid_h = n_lat // GRID_W
    ang_r = jnp.arange(grid_h, dtype=F32)[:, None] * inv[None]
    ang_c = jnp.arange(GRID_W, dtype=F32)[:, None] * inv[None]
    by_row = lambda a: jnp.repeat(jnp.concatenate([a, a], axis=-1), GRID_W, axis=0)
    by_col = lambda a: jnp.tile(jnp.concatenate([a, a], axis=-1), (grid_h, 1))
    cos = jnp.concatenate([by_row(jnp.cos(ang_r)), by_col(jnp.cos(ang_c))], axis=-1)
    sin = jnp.concatenate([by_row(jnp.sin(ang_r)), by_col(jnp.sin(ang_c))], axis=-1)
    first = (jnp.arange(ATT_HEAD_DIM) % half) < half // 2
    tabs = [cos, jnp.where(first, -sin, 0.0), jnp.where(first, 0.0, sin)]
    ident = [jnp.ones((n_ctx, ATT_HEAD_DIM), F32), jnp.zeros((n_ctx, ATT_HEAD_DIM), F32), jnp.zeros((n_ctx, ATT_HEAD_DIM), F32)]
    reps = LANES // ATT_HEAD_DIM
    return [jnp.tile(jnp.concatenate([a, b], axis=0), (1, reps)) for a, b in zip(tabs, ident)]


def kernel(x, c, ctx, c_ctx, ada_w, ada_b, norm_w, ffn_w_in, ffn_w_out, final_norm_w, hg_w_in, hg_lb, hg_norm_w, hg_w_out, ssm_w_in, ssm_conv_w, ssm_conv_b, ssm_dt_bias, ssm_a_log, ssm_d, ssm_norm_w, ssm_w_out, pool_w, pool_scale, att_w_qkv, att_sink, att_w_out):
    assert x.shape[0] == 1 and DEPTH == 4
    n_lat, n_ctx = x.shape[1], ctx.shape[1]
    assert n_lat % TOKEN_BLOCK == 0 and n_ctx == TOKEN_BLOCK
    xs = jnp.concatenate([x[0], ctx[0]], axis=0)
    m = xs.shape[0]
    mods = _ada_table(c, c_ctx, ada_w, ada_b)
    lb_tab = jnp.cumsum(jax.nn.softmax(hg_lb.astype(F32), axis=1), axis=1)
    row = lambda v: v.reshape(1, -1)
    ffn = lambda l: (ffn_w_in[l].astype(BF16), ffn_w_out[l].astype(BF16))
    sds = lambda w, dt: jax.ShapeDtypeStruct((m, w), dt)
    tile = lambda w: (lambda tm: pl.BlockSpec((tm, w), lambda i: (i, 0)))

    w = hg_w_in[0].astype(BF16)
    qs, cumf, kf, cumb, kb, v, gs = _proj_call(
        _proj_hgrn_kernel, xs, row(norm_w[0, 0]), mods, 0, [w, lb_tab[:, 0]], [_resident(w.shape), _resident((2, D_MODEL))],
        [sds(D_MODEL, BF16), sds(D_MODEL, F32), sds(D_MODEL, BF16), sds(D_MODEL, F32), sds(D_MODEL, BF16),
         sds(D_MODEL, BF16), sds(D_MODEL, BF16)], n_lat)
    o_f = _hgrn_scan(qs, cumf, kf, v, False)
    a = _hgrn_scan(qs, cumb, kb, v, True, gs=gs, o_f=o_f, gn_w=row(hg_norm_w[0]))
    xs = _mix_ffn(xs, a, hg_w_out[0].astype(BF16), mods, 0, row(norm_w[0, 1]), *ffn(0), n_lat, m)

    zx = SSM_D_INNER + SSM_XBC
    w = ssm_w_in[0][:, :zx].astype(BF16)
    pad = jnp.zeros((D_MODEL, LANES - SSM_HEADS), F32)
    wdt = jnp.concatenate([ssm_w_in[0][:, zx:zx + SSM_HEADS], pad, ssm_w_in[0][:, zx + SSM_HEADS:], pad], axis=1).astype(BF16)
    zs, xc, dt_raw = _proj_call(
        _proj_ssd_kernel, xs, row(norm_w[1, 0]), mods, 1, [w, wdt, ssm_conv_w[0], row(ssm_conv_b[0])],
        [_resident(w.shape), _resident(wdt.shape), _resident((SSM_CONV, SSM_XBC)), _resident((1, SSM_XBC))],
        [sds(SSM_D_INNER, BF16), sds(SSM_XBC, BF16), sds(2 * LANES, F32)], n_lat, halo=True, n_rows=m,
        scratch=lambda tm: [pltpu.VMEM((tm + 2 * SUBLANES, SSM_XBC), F32)])
    lane_pad = lambda v: jnp.concatenate([v, jnp.zeros((LANES - SSM_HEADS,), F32)]).reshape(1, LANES)
    y_f = _ssd_scan(False, xc, dt_raw, lane_pad(ssm_dt_bias[0, 0]), lane_pad(ssm_a_log[0, 0]))
    a = _ssd_scan(True, xc, dt_raw, lane_pad(ssm_dt_bias[0, 1]), lane_pad(ssm_a_log[0, 1]), zs=zs, y_f=y_f,
                  d_skip=row(jnp.repeat(ssm_d[0], SSM_HEADDIM)), norm_w=row(ssm_norm_w[0]))
    xs = _mix_ffn(xs, a, ssm_w_out[0].astype(BF16), mods, 1, row(norm_w[1, 1]), *ffn(1), n_lat, m)

    a = _pool_delta(xs, row(norm_w[2, 0]), mods, 2, n_lat, n_ctx)
    xs = _mix_ffn(xs, a, pool_w[0].astype(BF16), mods, 2, row(norm_w[2, 1]), *ffn(2), n_lat, m,
                  pool_scale=row(pool_scale[0]))

    rope = _rope_tables(n_lat, n_ctx)
    qd = ATT_Q_HEADS * ATT_HEAD_DIM
    kd = ATT_KV_HEADS * ATT_HEAD_DIM
    wq = jnp.concatenate([att_w_qkv[0][:, :qd] * ATT_HEAD_DIM ** -0.5, att_w_qkv[0][:, qd:]], axis=1).astype(BF16)
    (qkv,) = _proj_call(
        _proj_attn_kernel, xs, row(norm_w[3, 0]), mods, 3, [wq] + rope, [_resident(wq.shape)] + [tile(LANES)] * 3,
        [sds(qd + 2 * kd, BF16)], n_lat, rope_cols=qd + kd)
    a = _attention(qkv, att_sink[0], n_lat, n_ctx)
    out = _mix_ffn(xs, a, att_w_out[0].astype(BF16), mods, 3, row(norm_w[3, 1]), *ffn(3), n_lat, n_lat,
                   final_w=row(final_norm_w))
    return out[None]
```

```python
import functools

import jax
import jax.numpy as jnp
from jax import lax
from jax.experimental import pallas as pl
from jax.experimental.pallas import tpu as pltpu

F32 = jnp.float32
BF16 = jnp.bfloat16

D_MODEL = 1024
DEPTH = 4
GRID_W = 64
EPS = 1e-6
NEG_INF = -1e30
LOG2E = 1.4426950408889634

HG_HEADS = 8
HG_DK = 128
HG_CHUNK = 64

SSM_D_INNER = 2 * D_MODEL
SSM_HEADDIM = 64
SSM_HEADS = SSM_D_INNER // SSM_HEADDIM
SSM_GROUPS = 4
SSM_STATE = 128
SSM_CONV = 5
SSM_BC = 2 * SSM_GROUPS * SSM_STATE
SSM_XBC = SSM_D_INNER + SSM_BC
SSM_HPG = SSM_HEADS // SSM_GROUPS

POOL_WINDOWS = (2, 4, 8, 16)
POOL_GROUP = D_MODEL // len(POOL_WINDOWS)

ATT_HEAD_DIM = 64
ATT_Q_HEADS = D_MODEL // ATT_HEAD_DIM
ATT_KV_HEADS = 4
ATT_GRP = ATT_Q_HEADS // ATT_KV_HEADS
ATT_WINDOW = 128
ROPE_THETA = 10000.0

FFN_HIDDEN = -(-8 * D_MODEL // (3 * 256)) * 256

LANES = 128
SUBLANES = 8
BF16_ROWS = 16
MXU_COLS = 256
TOKEN_BLOCK = 256
ROW_TILE = 640
VMEM_LIMIT = 56 * 1024 * 1024

SH1, SC1, G1, SH2, SC2, G2 = range(6)


def _cparams(sem):
    return pltpu.CompilerParams(dimension_semantics=sem, vmem_limit_bytes=VMEM_LIMIT)


def _row_tile(m, target):
    best = None
    for t in range(HG_CHUNK, target + 1, HG_CHUNK):
        if m % t == 0:
            best = t
    assert best is not None
    return best


def _resident(shape):
    zeros = (0,) * len(shape)
    return pl.BlockSpec(shape, lambda *_: zeros, pipeline_mode=pl.Buffered(1))


def _dot(a, b):
    return jnp.dot(a, b, preferred_element_type=F32)


def _dot_nt(a, b):
    return lax.dot_general(a, b, (((1,), (1,)), ((), ())), preferred_element_type=F32)


def _dot_tn(a, b):
    return lax.dot_general(a, b, (((0,), (0,)), ((), ())), preferred_element_type=F32)


def _split2(x):
    hi = x.astype(BF16)
    lo = (x - hi.astype(F32)).astype(BF16)
    return hi, lo


def _split3(x):
    hi = x.astype(BF16)
    r = x - hi.astype(F32)
    mid = r.astype(BF16)
    lo = (r - mid.astype(F32)).astype(BF16)
    return hi, mid, lo


def _silu(x):
    return x * jax.nn.sigmoid(x)


def _mod_rows(mod_ref, slot, is_ctx):
    lo, hi = slot * D_MODEL, (slot + 1) * D_MODEL
    return jnp.where(is_ctx, mod_ref[1:2, lo:hi], mod_ref[0:1, lo:hi])


def _rms(x):
    return x * lax.rsqrt(jnp.mean(x * x, axis=-1, keepdims=True) + EPS)


def _modnorm(x, nw, mod_ref, sh_slot, sc_slot, is_ctx):
    y = _rms(x) * nw
    return y * (1.0 + _mod_rows(mod_ref, sc_slot, is_ctx)) + _mod_rows(mod_ref, sh_slot, is_ctx)


def _is_ctx_rows(row0, rows, n_lat):
    return (row0 + lax.broadcasted_iota(jnp.int32, (rows, 1), 0)) >= n_lat


def _mod_spec(layer):
    return pl.BlockSpec((None, SUBLANES, 6 * D_MODEL), lambda *_: (layer, 0, 0))


def _ada_kernel(c_ref, w_ref, b_ref, o_ref):
    a = _silu(c_ref[...])
    a_hi, a_lo = _split2(a)
    w_hi, w_lo = _split2(w_ref[...])
    o_ref[...] = _dot(a_hi, w_hi) + _dot(a_hi, w_lo) + _dot(a_lo, w_hi) + b_ref[...]


def _ada_table(c, c_ctx, ada_w, ada_b):
    cc = jnp.concatenate([c[0:1], c_ctx[None], jnp.zeros((SUBLANES - 2, D_MODEL), F32)], axis=0)
    n = 6 * D_MODEL
    tn = n // 4
    return pl.pallas_call(
        _ada_kernel,
        grid=(DEPTH, n // tn),
        in_specs=[pl.BlockSpec((SUBLANES, D_MODEL), lambda l, j: (0, 0)),
                  pl.BlockSpec((None, D_MODEL, tn), lambda l, j: (l, 0, j)),
                  pl.BlockSpec((None, 1, tn), lambda l, j: (l, 0, j))],
        out_specs=pl.BlockSpec((None, SUBLANES, tn), lambda l, j: (l, 0, j)),
        out_shape=jax.ShapeDtypeStruct((DEPTH, SUBLANES, n), F32),
        compiler_params=_cparams(("parallel", "parallel")),
    )(cc, ada_w, ada_b.reshape(DEPTH, 1, n))


def _cumsum_rows(x, rev):
    n = x.shape[0]
    row = lax.broadcasted_iota(jnp.int32, x.shape, 0)
    k = 1
    while k < n:
        if rev:
            x = x + jnp.where(row < n - k, pltpu.roll(x, n - k, axis=0), 0.0)
        else:
            x = x + jnp.where(row >= k, pltpu.roll(x, k, axis=0), 0.0)
        k *= 2
    return x


def _stream_rows(x_ref, ctx_ref, is_ctx):
    tm = x_ref.shape[0]
    off = tm - ctx_ref.shape[0]
    ctx_rows = jnp.concatenate([jnp.zeros((off, D_MODEL), F32), ctx_ref[...]], axis=0)
    return jnp.where(is_ctx, ctx_rows, x_ref[...])


def _normed_rows(x_ref, nw_ref, mod_ref, tm, n_lat, ctx_ref=None):
    is_ctx = _is_ctx_rows(pl.program_id(0) * tm, tm, n_lat)
    x = x_ref[...] if ctx_ref is None else _stream_rows(x_ref, ctx_ref, is_ctx)
    return _modnorm(x, nw_ref[...], mod_ref, SH1, SC1, is_ctx).astype(BF16)


def _proj_hgrn_kernel(x_ref, ctx_ref, nw_ref, mod_ref, w_ref, lb_ref, qs_ref, cumf_ref, kf_ref, cumb_ref, kb_ref, v_ref,
                      gs_ref, *, tm, n_lat):
    h = _normed_rows(x_ref, nw_ref, mod_ref, tm, n_lat, ctx_ref)
    col = lambda k: slice(k * D_MODEL, (k + 1) * D_MODEL)
    qs_ref[...] = _silu(_dot(h, w_ref[:, col(0)])).astype(qs_ref.dtype)
    for d, (cum_ref, k_ref) in enumerate(((cumf_ref, kf_ref), (cumb_ref, kb_ref))):
        lb = lb_ref[d:d + 1, :]
        f = lb + (1.0 - lb) * jax.nn.sigmoid(_dot(h, w_ref[:, col(1 + d)]))
        k_ref[...] = (1.0 - f).astype(k_ref.dtype)
        lf = jnp.log(f)
        for c in range(tm // HG_CHUNK):
            rows = slice(c * HG_CHUNK, (c + 1) * HG_CHUNK)
            cum_ref[rows, :] = _cumsum_rows(lf[rows], rev=bool(d))
    v_ref[...] = _dot(h, w_ref[:, col(3)]).astype(v_ref.dtype)
    gs_ref[...] = _silu(_dot(h, w_ref[:, col(4)])).astype(gs_ref.dtype)


def _conv_silu(taps, cw_ref, cb_ref, cols):
    acc = cb_ref[:, cols] + cw_ref[0:1, cols] * taps[0]
    for j in range(1, SSM_CONV):
        acc = acc + cw_ref[j:j + 1, cols] * taps[j]
    return _silu(acc)


def _proj_ssd_kernel(x_ref, xp_ref, xn_ref, nw_ref, mod_ref, w_ref, wdt_ref, cw_ref, cb_ref, zs_ref, xc_ref, dt_ref,
                     ext_ref, *, tm, n_lat, n_rows):
    i = pl.program_id(0)
    nt = n_rows // tm
    halo = SUBLANES
    half = SSM_CONV // 2
    norm = lambda ref, row0, rows: _modnorm(ref[...], nw_ref[...], mod_ref, SH1, SC1, _is_ctx_rows(row0, rows, n_lat))
    hf = norm(x_ref, i * tm, tm)
    h = hf.astype(BF16)
    for c in range(SSM_D_INNER // D_MODEL):
        sl = slice(c * D_MODEL, (c + 1) * D_MODEL)
        zs_ref[:, sl] = _silu(_dot(h, w_ref[:, sl])).astype(zs_ref.dtype)
    dt_ref[...] = _dot(h, wdt_ref[...])
    h_ext = jnp.concatenate([norm(xp_ref, i * tm - halo, halo), hf, norm(xn_ref, (i + 1) * tm, halo)], axis=0).astype(BF16)
    vp = jnp.where(i > 0, 1.0, 0.0)
    vn = jnp.where(i < nt - 1, 1.0, 0.0)
    chunks = [slice(c * D_MODEL, (c + 1) * D_MODEL) for c in range(SSM_XBC // D_MODEL)]
    for cols in chunks:
        y = _dot(h_ext, w_ref[:, SSM_D_INNER + cols.start:SSM_D_INNER + cols.stop])
        ext_ref[0:halo, cols] = y[0:halo] * vp
        ext_ref[halo:halo + tm, cols] = y[halo:halo + tm]
        ext_ref[halo + tm:, cols] = y[halo + tm:] * vn
    for cols in chunks:
        taps = [ext_ref[halo - half + j:halo - half + j + tm, cols] for j in range(SSM_CONV)]
        xc_ref[:, cols] = _conv_silu(taps, cw_ref, cb_ref, cols).astype(xc_ref.dtype)
    if n_lat % tm:
        bt, off = divmod(n_lat, tm)
        fix = BF16_ROWS
        assert off % fix == 0 and fix <= off <= tm - fix

        @pl.when(i == bt)
        def _():
            e0 = halo + off
            is_lat_row = lax.broadcasted_iota(jnp.int32, (2 * fix + 2 * halo, 1), 0) < fix + halo
            for cols in chunks:
                win = ext_ref[e0 - fix - halo:e0 + fix + halo, cols]
                lat_w = jnp.where(is_lat_row, win, 0.0)
                ctx_w = jnp.where(is_lat_row, 0.0, win)
                lat_taps = [lat_w[halo - half + j:halo - half + j + fix] for j in range(SSM_CONV)]
                ctx_taps = [ctx_w[halo + fix - half + j:halo + fix - half + j + fix] for j in range(SSM_CONV)]
                xc_ref[off - fix:off, cols] = _conv_silu(lat_taps, cw_ref, cb_ref, cols).astype(xc_ref.dtype)
                xc_ref[off:off + fix, cols] = _conv_silu(ctx_taps, cw_ref, cb_ref, cols).astype(xc_ref.dtype)


def _proj_attn_kernel(x_ref, nw_ref, mod_ref, w_ref, cos_ref, sa_ref, sb_ref, o_ref, *, tm, n_lat, q_cols, rope_cols):
    h = _normed_rows(x_ref, nw_ref, mod_ref, tm, n_lat)
    n = w_ref.shape[1]
    for c in range(n // MXU_COLS):
        y = _dot(h, w_ref[:, c * MXU_COLS:(c + 1) * MXU_COLS])
        for s_ in range(MXU_COLS // LANES):
            c0 = c * MXU_COLS + s_ * LANES
            ys = y[:, s_ * LANES:(s_ + 1) * LANES]
            if c0 < rope_cols:
                ys = (ys * cos_ref[...] + pltpu.roll(ys, LANES - 16, axis=1) * sa_ref[...]
                      + pltpu.roll(ys, 16, axis=1) * sb_ref[...])
            if c0 < q_cols:
                ys = ys * LOG2E
            o_ref[:, c0:c0 + LANES] = ys.astype(o_ref.dtype)


def _proj_call(kernel_fn, x, nw, mods, layer, extra_in, extra_specs, out_shapes, n_lat, scratch=None, halo=False, ctx=None,
               **kw):
    m = out_shapes[0].shape[0]
    tm = _row_tile(m, ROW_TILE)
    row = lambda w: pl.BlockSpec((tm, w), lambda i: (i, 0))
    lead, lead_specs = [x], [row(D_MODEL)]
    if ctx is not None:
        assert n_lat % tm + ctx.shape[0] == tm
        lead.append(ctx)
        lead_specs.append(_resident(ctx.shape))
    if halo:
        tpb = tm // SUBLANES
        lead += [x, x]
        lead_specs += [pl.BlockSpec((SUBLANES, D_MODEL), lambda i: (jnp.maximum(i * tpb - 1, 0), 0)),
                       pl.BlockSpec((SUBLANES, D_MODEL), lambda i: (jnp.minimum((i + 1) * tpb, m // SUBLANES - 1), 0))]
    return pl.pallas_call(
        functools.partial(kernel_fn, tm=tm, n_lat=n_lat, **kw),
        grid=(m // tm,),
        in_specs=lead_specs + [_resident((1, D_MODEL)), _mod_spec(layer)] + [s(tm) if callable(s) else s for s in extra_specs],
        out_specs=[row(s.shape[1]) for s in out_shapes],
        out_shape=out_shapes,
        scratch_shapes=scratch(tm) if scratch else [],
        compiler_params=_cparams(("parallel",)),
    )(*lead, nw, mods, *extra_in)


def _ffn_chunks():
    out, c0 = [], 0
    while c0 < FFN_HIDDEN:
        cw = min(D_MODEL, FFN_HIDDEN - c0)
        assert cw % MXU_COLS == 0
        out.append((c0, cw))
        c0 += cw
    return out


def _mix_ffn_kernel(x_ref, a_ref, wo_ref, *rest, tm, n_lat, pool, final, split):
    rest = list(rest)
    ctx_ref = rest.pop(0) if split else None
    ps_ref = rest.pop(0) if pool else None
    mod_ref, nw_ref, win_ref, wout_ref = rest[:4]
    rest = rest[4:]
    fw_ref = rest.pop(0) if final else None
    (o_ref,) = rest
    is_ctx = _is_ctx_rows(pl.program_id(0) * tm, tm, n_lat)
    if pool:
        parts = [_dot(a_ref[:, g * POOL_GROUP:(g + 1) * POOL_GROUP], wo_ref[g]) for g in range(len(POOL_WINDOWS))]
        y = jnp.concatenate(parts, axis=-1) * ps_ref[...]
    else:
        y = _dot(a_ref[...], wo_ref[...])
    x0 = _stream_rows(x_ref, ctx_ref, is_ctx) if split else x_ref[...]
    x1 = x0 + _mod_rows(mod_ref, G1, is_ctx) * y
    h = _modnorm(x1, nw_ref[...], mod_ref, SH2, SC2, is_ctx).astype(BF16)
    acc = None
    for c0, cw in _ffn_chunks():
        gate = _dot(h, win_ref[:, c0:c0 + cw])
        up = _dot(h, win_ref[:, FFN_HIDDEN + c0:FFN_HIDDEN + c0 + cw])
        part = _dot((_silu(gate) * up).astype(BF16), wout_ref[c0:c0 + cw, :])
        acc = part if acc is None else acc + part
    x2 = x1 + _mod_rows(mod_ref, G2, is_ctx) * acc
    o_ref[...] = _rms(x2) * fw_ref[...] if final else x2


def _mix_ffn(x, a, wo, mods, layer, nw2, w_in, w_out, n_lat, rows, pool_scale=None, final_w=None, ctx=None):
    pool = pool_scale is not None
    final = final_w is not None
    tm = _row_tile(rows, ROW_TILE)
    row = lambda w: pl.BlockSpec((tm, w), lambda i: (i, 0))
    in_specs = [row(D_MODEL), row(a.shape[1]), _resident(wo.shape)]
    args = [x, a, wo]
    if ctx is not None:
        assert n_lat % tm + ctx.shape[0] == tm
        in_specs.append(_resident(ctx.shape))
        args.append(ctx)
    if pool:
        in_specs.append(_resident((1, D_MODEL)))
        args.append(pool_scale)
    in_specs += [_mod_spec(layer), _resident((1, D_MODEL)), _resident(w_in.shape), _resident(w_out.shape)]
    args += [mods, nw2, w_in, w_out]
    if final:
        in_specs.append(_resident((1, D_MODEL)))
        args.append(final_w)
    return pl.pallas_call(
        functools.partial(_mix_ffn_kernel, tm=tm, n_lat=n_lat, pool=pool, final=final, split=ctx is not None),
        grid=(rows // tm,),
        in_specs=in_specs,
        out_specs=row(D_MODEL),
        out_shape=jax.ShapeDtypeStruct((rows, D_MODEL), F32),
        compiler_params=_cparams(("parallel",)),
    )(*args)


def _scan_block(i, nb, rev):
    if rev:
        return jnp.where(i == 0, nb - 1, nb - 1 - i)
    return jnp.where(i == 0, nb - 1, i - 1)


def _hgrn_kernel(q_ref, cum_ref, k_ref, v_ref, *rest, rev):
    if rev:
        g_ref, of_ref, gn_ref, o_ref, st_ref = rest
    else:
        o_ref, st_ref = rest
    L = HG_CHUNK
    nch = TOKEN_BLOCK // L

    @pl.when(pl.program_id(0) == 0)
    def _():
        st_ref[...] = jnp.zeros_like(st_ref)

    r_i = lax.broadcasted_iota(jnp.int32, (L, L), 0)
    c_i = lax.broadcasted_iota(jnp.int32, (L, L), 1)
    keep = (r_i <= c_i) if rev else (r_i >= c_i)
    mid = L // 2 if rev else L // 2 - 1
    end = 0 if rev else L - 1

    heads = [slice(h * HG_DK, (h + 1) * HG_DK) for h in range(HG_HEADS)]
    order = [(c, slice(c * L, (c + 1) * L)) for c in (reversed(range(nch)) if rev else range(nch))]
    att_l, ds_l = {}, {}
    for c, rows in order:
        for h, sl in enumerate(heads):
            q = q_ref[rows, sl].astype(F32)
            k = k_ref[rows, sl].astype(F32)
            cum = cum_ref[rows, sl]
            ref = cum[mid:mid + 1]
            last = cum[end:end + 1]
            att_l[c, h] = _dot_nt((q * jnp.exp(cum - ref)).astype(BF16), (k * jnp.exp(ref - cum)).astype(BF16))
            ds_l[c, h] = _dot_tn(v_ref[rows, sl], (k * jnp.exp(last - cum)).astype(BF16))
    for c, rows in order:
        for h, sl in enumerate(heads):
            cum = cum_ref[rows, sl]
            st = st_ref[h]
            inter = _dot_nt((q_ref[rows, sl].astype(F32) * jnp.exp(cum)).astype(BF16), st.astype(BF16))
            att = jnp.where(keep, att_l[c, h], 0.0).astype(BF16)
            o = inter + _dot(att, v_ref[rows, sl])
            st_ref[h] = st * jnp.exp(cum[end:end + 1]) + ds_l[c, h]
            if rev:
                o = o + of_ref[rows, sl].astype(F32)
                y = _rms(o) * gn_ref[...]
                o_ref[rows, sl] = (y * g_ref[rows, sl].astype(F32)).astype(o_ref.dtype)
            else:
                o_ref[rows, sl] = o.astype(o_ref.dtype)


def _hgrn_scan(qs, cum, k, v, rev, gs=None, o_f=None, gn_w=None):
    m = qs.shape[0]
    nb = m // TOKEN_BLOCK
    blk = functools.partial(_scan_block, nb=nb, rev=rev)
    spec = pl.BlockSpec((TOKEN_BLOCK, D_MODEL), lambda i: (blk(i), 0))
    in_specs = [spec] * 4
    args = [qs, cum, k, v]
    if rev:
        in_specs += [spec, spec, _resident((1, HG_DK))]
        args += [gs, o_f, gn_w]
    return pl.pallas_call(
        functools.partial(_hgrn_kernel, rev=rev),
        grid=(nb,),
        in_specs=in_specs,
        out_specs=spec,
        out_shape=jax.ShapeDtypeStruct((m, D_MODEL), BF16),
        scratch_shapes=[pltpu.VMEM((HG_HEADS, HG_DK, HG_DK), F32)],
        compiler_params=_cparams(("arbitrary",)),
    )(*args)


def _ssd_kernel(xc_ref, dt_ref, dtb_ref, alog_ref, *rest, rev):
    if rev:
        zs_ref, yf_ref, dsk_ref, nw_ref, o_ref, st_ref, stage_ref = rest
    else:
        y_ref, st_ref = rest
    T = TOKEN_BLOCK

    @pl.when(pl.program_id(0) == 0)
    def _():
        st_ref[...] = jnp.zeros_like(st_ref)

    gw = SSM_HPG * SSM_HEADDIM
    groups = [slice(g * gw, (g + 1) * gw) for g in range(SSM_GROUPS)]
    b_l, cb_l, yoff_l = [], [], []
    for g in range(SSM_GROUPS):
        b_l.append(xc_ref[:, SSM_D_INNER + g * SSM_STATE:SSM_D_INNER + (g + 1) * SSM_STATE])
        c_off = SSM_D_INNER + SSM_GROUPS * SSM_STATE + g * SSM_STATE
        c_g = xc_ref[:, c_off:c_off + SSM_STATE]
        cb_l.append(_dot_nt(c_g, b_l[g]))
        yoff_l.append(_dot(c_g, st_ref[g].astype(BF16)))

    dt = jax.nn.softplus(dt_ref[...] + dtb_ref[...])
    dta = dt * (-jnp.exp(alog_ref[...]))
    r_i = lax.broadcasted_iota(jnp.int32, (T, T), 0)
    c_i = lax.broadcasted_iota(jnp.int32, (T, T), 1)
    keep = (r_i <= c_i) if rev else (r_i >= c_i)
    tri = jnp.where(keep, 1.0, 0.0).astype(BF16)
    cum = sum(_dot(tri, p) for p in _split3(dta))
    end = 0 if rev else T - 1
    total = cum[end:end + 1]
    col2 = cum * LOG2E
    row2_t = (col2 - jnp.log2(dt)).T

    half = LANES // 2
    e_h = lax.broadcasted_iota(jnp.int32, (LANES, SSM_D_INNER), 0)
    e_c = lax.broadcasted_iota(jnp.int32, (LANES, SSM_D_INNER), 1)
    expand_m = jnp.where((e_c // SSM_HEADDIM == e_h % half) & (e_h % half < SSM_HEADS), 1.0, 0.0).astype(BF16)

    def expand(f):
        hi = f.astype(BF16).astype(F32)
        lane_f = lax.broadcasted_iota(jnp.int32, f.shape, 1)
        return _dot(jnp.where(lane_f < half, hi, pltpu.roll(f - hi, half, axis=1)).astype(BF16), expand_m)

    ecum_x = expand(jnp.exp(cum))
    fac_x = expand(dt * jnp.exp(total - cum))
    dec_x = expand(jnp.broadcast_to(jnp.exp(total), (SUBLANES, LANES)))[0:1]

    for g, gl in enumerate(groups):
        xs_fac = (xc_ref[:, gl].astype(F32) * fac_x[:, gl]).astype(BF16)
        st_ref[g] = st_ref[g] * dec_x[:, gl] + _dot_tn(b_l[g], xs_fac)
    lane = lax.broadcasted_iota(jnp.int32, (T, LANES), 1)
    first = lane < SSM_HEADDIM
    hb = T // 2
    keep_h = keep[0:hb, 0:hb]
    lo_, hi_ = slice(0, hb), slice(hb, T)
    for g, gl in enumerate(groups):
        cb = cb_l[g]
        for pr in range(SSM_HPG // 2):
            h0 = g * SSM_HPG + 2 * pr
            ol = slice(g * gw + pr * LANES, g * gw + (pr + 1) * LANES)
            xs_p = xc_ref[:, ol]
            ys = []
            for hh in (h0, h0 + 1):
                col = jnp.broadcast_to(col2[:, hh:hh + 1], (T, hb))
                row = row2_t[hh:hh + 1, :]

                def tile(rs, cs, masked):
                    e = jnp.exp2(col[rs] - row[:, cs])
                    if masked:
                        e = jnp.where(keep_h, e, 0.0)
                    return (cb[rs, cs] * e).astype(BF16)

                if rev:
                    top = _dot(jnp.concatenate([tile(lo_, lo_, True), tile(lo_, hi_, False)], axis=1), xs_p)
                    bot = _dot(tile(hi_, hi_, True), xs_p[hb:])
                else:
                    top = _dot(tile(lo_, lo_, True), xs_p[:hb])
                    bot = _dot(jnp.concatenate([tile(hi_, lo_, False), tile(hi_, hi_, True)], axis=1), xs_p)
                ys.append(jnp.concatenate([top, bot], axis=0))
            yp = yoff_l[g][:, pr * LANES:(pr + 1) * LANES] * ecum_x[:, ol] + jnp.where(first, ys[0], ys[1])
            if rev:
                yt = yp + yf_ref[:, ol].astype(F32) + dsk_ref[:, ol] * xc_ref[:, ol].astype(F32)
                stage_ref[:, ol] = yt * zs_ref[:, ol].astype(F32)
            else:
                y_ref[:, ol] = yp.astype(y_ref.dtype)
        if rev:
            o_ref[:, gl] = (_rms(stage_ref[:, gl]) * nw_ref[:, gl]).astype(o_ref.dtype)


def _ssd_scan(rev, xc, dt_raw, dt_bias, a_log, zs=None, y_f=None, d_skip=None, norm_w=None):
    m = dt_raw.shape[0]
    T = TOKEN_BLOCK
    nb = m // T
    blk = functools.partial(_scan_block, nb=nb, rev=rev)
    rows = lambda w: pl.BlockSpec((T, w), lambda i: (blk(i), 0))
    in_specs = [rows(SSM_XBC), pl.BlockSpec((T, LANES), lambda i: (blk(i), 1 if rev else 0)),
                _resident((1, LANES)), _resident((1, LANES))]
    args = [xc, dt_raw, dt_bias, a_log]
    scratch = [pltpu.VMEM((SSM_GROUPS, SSM_STATE, SSM_HPG * SSM_HEADDIM), F32)]
    if rev:
        in_specs += [rows(SSM_D_INNER), rows(SSM_D_INNER), _resident((1, SSM_D_INNER)), _resident((1, SSM_D_INNER))]
        args += [zs, y_f, d_skip, norm_w]
        scratch.append(pltpu.VMEM((T, SSM_D_INNER), F32))
    return pl.pallas_call(
        functools.partial(_ssd_kernel, rev=rev),
        grid=(nb,),
        in_specs=in_specs,
        out_specs=rows(SSM_D_INNER),
        out_shape=jax.ShapeDtypeStruct((m, SSM_D_INNER), BF16),
        scratch_shapes=scratch,
        compiler_params=_cparams(("arbitrary",)),
    )(*args)


def _pool_kernel(x_ref, xp_ref, xn_ref, nw_ref, mod_ref, o_ref, ext_ref, *, nb, n_lat, n_ctx):
    T = TOKEN_BLOCK
    b = pl.program_id(0)
    is_ctx = b == nb - 1
    vp = jnp.where((b > 0) & (b < nb - 1), 1.0, 0.0)
    vn = jnp.where(b < nb - 2, 1.0, 0.0)
    norm = lambda x: _modnorm(x, nw_ref[...], mod_ref, SH1, SC1, is_ctx)
    ext_ref[0:SUBLANES] = norm(xp_ref[...]) * vp
    ext_ref[SUBLANES:SUBLANES + T] = norm(x_ref[...])
    ext_ref[SUBLANES + T:] = norm(xn_ref[...]) * vn
    t = jnp.where(is_ctx, 0, b * T) + lax.broadcasted_iota(jnp.int32, (T, 1), 0)
    seq = jnp.where(is_ctx, n_ctx, n_lat)
    for gi, win in enumerate(POOL_WINDOWS):
        sl = slice(gi * POOL_GROUP, (gi + 1) * POOL_GROUP)
        s0 = SUBLANES - win // 2
        acc = ext_ref[s0:s0 + T, sl]
        for jw in range(1, win):
            acc = acc + ext_ref[s0 + jw:s0 + jw + T, sl]
        cnt = jnp.minimum(t + (win - win // 2), seq) - jnp.maximum(t - win // 2, 0)
        o_ref[:, sl] = (acc / cnt.astype(F32) - ext_ref[SUBLANES:SUBLANES + T, sl]).astype(o_ref.dtype)


def _pool_delta(x, nw, mods, layer, n_lat, n_ctx):
    m = x.shape[0]
    T = TOKEN_BLOCK
    nb = m // T
    row8 = T // SUBLANES
    return pl.pallas_call(
        functools.partial(_pool_kernel, nb=nb, n_lat=n_lat, n_ctx=n_ctx),
        grid=(nb,),
        in_specs=[pl.BlockSpec((T, D_MODEL), lambda i: (i, 0)),
                  pl.BlockSpec((SUBLANES, D_MODEL), lambda i: (jnp.maximum(i * row8 - 1, 0), 0)),
                  pl.BlockSpec((SUBLANES, D_MODEL), lambda i: (jnp.minimum((i + 1) * row8, m // SUBLANES - 1), 0)),
                  _resident((1, D_MODEL)), _mod_spec(layer)],
        out_specs=pl.BlockSpec((T, D_MODEL), lambda i: (i, 0)),
        out_shape=jax.ShapeDtypeStruct((m, D_MODEL), BF16),
        scratch_shapes=[pltpu.VMEM((T + 2 * SUBLANES, D_MODEL), F32)],
        compiler_params=_cparams(("parallel",)),
    )(x, x, x, nw, mods)


def _attn_kernel(q_ref, kc_ref, kp_ref, kn_ref, kx_ref, band_ref, sink_ref, o_ref, *, n_lat):
    T = TOKEN_BLOCK
    W = ATT_WINDOW
    i = pl.program_id(0)
    kv = jnp.concatenate([kp_ref[...], kc_ref[...], kn_ref[...], kx_ref[...]], axis=0)
    span = T + 2 * W
    rows = ATT_GRP * T
    kpos = i * T - W + lax.broadcasted_iota(jnp.int32, (1, span), 1)
    bias = band_ref[...] + jnp.where((kpos >= 0) & (kpos < n_lat), 0.0, NEG_INF)
    bias = jnp.concatenate([bias] * ATT_GRP, axis=0)
    head_row = lax.broadcasted_iota(jnp.int32, (rows, 1), 0) // T
    kd = ATT_KV_HEADS * ATT_HEAD_DIM
    s_l = []
    for kh in range(ATT_KV_HEADS):
        k = kv[:, kh * ATT_HEAD_DIM:(kh + 1) * ATT_HEAD_DIM]
        q = jnp.concatenate([q_ref[:, (kh * ATT_GRP + g) * ATT_HEAD_DIM:(kh * ATT_GRP + g + 1) * ATT_HEAD_DIM]
                             for g in range(ATT_GRP)], axis=0)
        s_l.append(_dot_nt(q, k))
    for kh in range(ATT_KV_HEADS):
        v = kv[:, kd + kh * ATT_HEAD_DIM:kd + (kh + 1) * ATT_HEAD_DIM]
        s = jnp.concatenate([s_l[kh][:, :span] + bias, s_l[kh][:, span:]], axis=1)
        sink = jnp.zeros((rows, 1), F32)
        for g in range(ATT_GRP):
            sink = jnp.where(head_row == g, sink_ref[kh * ATT_GRP + g] * LOG2E, sink)
        mx = jnp.maximum(jnp.max(s, axis=-1, keepdims=True), sink)
        p = jnp.exp2(s - mx)
        den = jnp.sum(p, axis=-1, keepdims=True) + jnp.exp2(sink - mx)
        o = _dot(p.astype(BF16), v) / den
        for g in range(0, ATT_GRP, 2):
            pair = jnp.concatenate([o[g * T:(g + 1) * T], o[(g + 1) * T:(g + 2) * T]], axis=-1)
            c0 = (kh * ATT_GRP + g) * ATT_HEAD_DIM
            o_ref[:, c0:c0 + 2 * ATT_HEAD_DIM] = pair.astype(o_ref.dtype)


def _attention(qkv, sink, n_lat, n_ctx):
    T = TOKEN_BLOCK
    W = ATT_WINDOW
    nq = n_lat // T
    kvw = 2 * ATT_KV_HEADS * ATT_HEAD_DIM
    kv_col = D_MODEL // kvw
    wpb = T // W
    off = jnp.arange(T + 2 * W)[None, :] - jnp.arange(T)[:, None]
    band = jnp.where((off >= 0) & (off <= 2 * W), 0.0, NEG_INF).astype(F32)
    return pl.pallas_call(
        functools.partial(_attn_kernel, n_lat=n_lat),
        grid=(nq,),
        in_specs=[pl.BlockSpec((T, D_MODEL), lambda i: (i, 0)),
                  pl.BlockSpec((T, kvw), lambda i: (i, kv_col)),
                  pl.BlockSpec((W, kvw), lambda i: (jnp.maximum(i * wpb - 1, 0), kv_col)),
                  pl.BlockSpec((W, kvw), lambda i: (jnp.minimum((i + 1) * wpb, n_lat // W - 1), kv_col)),
                  pl.BlockSpec((n_ctx, kvw), lambda i: (n_lat // n_ctx, kv_col)),
                  _resident(band.shape),
                  pl.BlockSpec(memory_space=pltpu.SMEM)],
        out_specs=pl.BlockSpec((T, D_MODEL), lambda i: (i, 0)),
        out_shape=jax.ShapeDtypeStruct((n_lat, D_MODEL), BF16),
        compiler_params=_cparams(("parallel",)),
    )(qkv, qkv, qkv, qkv, qkv, band, sink)


def _rope_tables(n_lat, n_ctx):
    half = ATT_HEAD_DIM // 2
    inv = ROPE_THETA ** (-jnp.arange(0, half, 2, dtype=F32) / half)
    grid_h = n_lat // GRID_W
    ang_r = jnp.arange(grid_h, dtype=F32)[:, None] * inv[None]
    ang_c = jnp.arange(GRID_W, dtype=F32)[:, None] * inv[None]
    by_row = lambda a: jnp.repeat(jnp.concatenate([a, a], axis=-1), GRID_W, axis=0)
    by_col = lambda a: jnp.tile(jnp.concatenate([a, a], axis=-1), (grid_h, 1))
    cos = jnp.concatenate([by_row(jnp.cos(ang_r)), by_col(jnp.cos(ang_c))], axis=-1)
    sin = jnp.concatenate([by_row(jnp.sin(ang_r)), by_col(jnp.sin(ang_c))], axis=-1)
    first = (jnp.arange(ATT_HEAD_DIM) % half) < half // 2
    tabs = [cos, jnp.where(first, -sin, 0.0), jnp.where(first, 0.0, sin)]
    ident = [jnp.ones((n_ctx, ATT_HEAD_DIM), F32), jnp.zeros((n_ctx, ATT_HEAD_DIM), F32), jnp.zeros((n_ctx, ATT_HEAD_DIM), F32)]
    reps = LANES // ATT_HEAD_DIM
    return [jnp.tile(jnp.concatenate([a, b], axis=0), (1, reps)) for a, b in zip(tabs, ident)]


def kernel(x, c, ctx, c_ctx, ada_w, ada_b, norm_w, ffn_w_in, ffn_w_out, final_norm_w, hg_w_in, hg_lb, hg_norm_w, hg_w_out, ssm_w_in, ssm_conv_w, ssm_conv_b, ssm_dt_bias, ssm_a_log, ssm_d, ssm_norm_w, ssm_w_out, pool_w, pool_scale, att_w_qkv, att_sink, att_w_out):
    assert x.shape[0] == 1 and DEPTH == 4
    n_lat, n_ctx = x.shape[1], ctx.shape[1]
    assert n_lat % TOKEN_BLOCK == 0 and n_ctx == TOKEN_BLOCK
    m = n_lat + n_ctx
    mods = _ada_table(c, c_ctx, ada_w, ada_b)
    lb_tab = jnp.cumsum(jax.nn.softmax(hg_lb.astype(F32), axis=1), axis=1)
    row = lambda v: v.reshape(1, -1)
    ffn = lambda l: (ffn_w_in[l].astype(BF16), ffn_w_out[l].astype(BF16))
    sds = lambda w, dt: jax.ShapeDtypeStruct((m, w), dt)
    tile = lambda w: (lambda tm: pl.BlockSpec((tm, w), lambda i: (i, 0)))

    w = hg_w_in[0].astype(BF16)
    qs, cumf, kf, cumb, kb, v, gs = _proj_call(
        _proj_hgrn_kernel, x[0], row(norm_w[0, 0]), mods, 0, [w, lb_tab[:, 0]], [_resident(w.shape), _resident((2, D_MODEL))],
        [sds(D_MODEL, BF16), sds(D_MODEL, F32), sds(D_MODEL, BF16), sds(D_MODEL, F32), sds(D_MODEL, BF16),
         sds(D_MODEL, BF16), sds(D_MODEL, BF16)], n_lat, ctx=ctx[0])
    o_f = _hgrn_scan(qs, cumf, kf, v, False)
    a = _hgrn_scan(qs, cumb, kb, v, True, gs=gs, o_f=o_f, gn_w=row(hg_norm_w[0]))
    xs = _mix_ffn(x[0], a, hg_w_out[0].astype(BF16), mods, 0, row(norm_w[0, 1]), *ffn(0), n_lat, m, ctx=ctx[0])

    zx = SSM_D_INNER + SSM_XBC
    w = ssm_w_in[0][:, :zx].astype(BF16)
    pad = jnp.zeros((D_MODEL, LANES - SSM_HEADS), F32)
    wdt = jnp.concatenate([ssm_w_in[0][:, zx:zx + SSM_HEADS], pad, ssm_w_in[0][:, zx + SSM_HEADS:], pad], axis=1).astype(BF16)
    zs, xc, dt_raw = _proj_call(
        _proj_ssd_kernel, xs, row(norm_w[1, 0]), mods, 1, [w, wdt, ssm_conv_w[0], row(ssm_conv_b[0])],
        [_resident(w.shape), _resident(wdt.shape), _resident((SSM_CONV, SSM_XBC)), _resident((1, SSM_XBC))],
        [sds(SSM_D_INNER, BF16), sds(SSM_XBC, BF16), sds(2 * LANES, F32)], n_lat, halo=True, n_rows=m,
        scratch=lambda tm: [pltpu.VMEM((tm + 2 * SUBLANES, SSM_XBC), F32)])
    lane_pad = lambda v: jnp.concatenate([v, jnp.zeros((LANES - SSM_HEADS,), F32)]).reshape(1, LANES)
    y_f = _ssd_scan(False, xc, dt_raw, lane_pad(ssm_dt_bias[0, 0]), lane_pad(ssm_a_log[0, 0]))
    a = _ssd_scan(True, xc, dt_raw, lane_pad(ssm_dt_bias[0, 1]), lane_pad(ssm_a_log[0, 1]), zs=zs, y_f=y_f,
                  d_skip=row(jnp.repeat(ssm_d[0], SSM_HEADDIM)), norm_w=row(ssm_norm_w[0]))
    xs = _mix_ffn(xs, a, ssm_w_out[0].astype(BF16), mods, 1, row(norm_w[1, 1]), *ffn(1), n_lat, m)

    a = _pool_delta(xs, row(norm_w[2, 0]), mods, 2, n_lat, n_ctx)
    xs = _mix_ffn(xs, a, pool_w[0].astype(BF16), mods, 2, row(norm_w[2, 1]), *ffn(2), n_lat, m,
                  pool_scale=row(pool_scale[0]))

    rope = _rope_tables(n_lat, n_ctx)
    qd = ATT_Q_HEADS * ATT_HEAD_DIM
    kd = ATT_KV_HEADS * ATT_HEAD_DIM
    wq = jnp.concatenate([att_w_qkv[0][:, :qd] * ATT_HEAD_DIM ** -0.5, att_w_qkv[0][:, qd:]], axis=1).astype(BF16)
    (qkv,) = _proj_call(
        _proj_attn_kernel, xs, row(norm_w[3, 0]), mods, 3, [wq] + rope, [_resident(wq.shape)] + [tile(LANES)] * 3,
        [sds(qd + 2 * kd, BF16)], n_lat, q_cols=qd, rope_cols=qd + kd)
    a = _attention(qkv, att_sink[0], n_lat, n_ctx)
    out = _mix_ffn(xs, a, att_w_out[0].astype(BF16), mods, 3, row(norm_w[3, 1]), *ffn(3), n_lat, n_lat,
                   final_w=row(final_norm_w))
    return out[None]
```

```python
import functools

import jax
import jax.numpy as jnp
from jax import lax
from jax.experimental import pallas as pl
from jax.experimental.pallas import tpu as pltpu

F32 = jnp.float32
BF16 = jnp.bfloat16

D_MODEL = 1024
DEPTH = 4
GRID_W = 64
EPS = 1e-6
NEG_INF = -1e30
LOG2E = 1.4426950408889634

HG_HEADS = 8
HG_DK = 128
HG_CHUNK = 64

SSM_D_INNER = 2 * D_MODEL
SSM_HEADDIM = 64
SSM_HEADS = SSM_D_INNER // SSM_HEADDIM
SSM_GROUPS = 4
SSM_STATE = 128
SSM_CONV = 5
SSM_BC = 2 * SSM_GROUPS * SSM_STATE
SSM_XBC = SSM_D_INNER + SSM_BC
SSM_HPG = SSM_HEADS // SSM_GROUPS

POOL_WINDOWS = (2, 4, 8, 16)
POOL_GROUP = D_MODEL // len(POOL_WINDOWS)

ATT_HEAD_DIM = 64
ATT_Q_HEADS = D_MODEL // ATT_HEAD_DIM
ATT_KV_HEADS = 4
ATT_GRP = ATT_Q_HEADS // ATT_KV_HEADS
ATT_WINDOW = 128
ROPE_THETA = 10000.0

FFN_HIDDEN = -(-8 * D_MODEL // (3 * 256)) * 256

LANES = 128
SUBLANES = 8
BF16_ROWS = 16
MXU_COLS = 256
TOKEN_BLOCK = 256
ROW_TILE = 640
VMEM_LIMIT = 56 * 1024 * 1024

SH1, SC1, G1, SH2, SC2, G2 = range(6)


def _cparams(sem):
    return pltpu.CompilerParams(dimension_semantics=sem, vmem_limit_bytes=VMEM_LIMIT)


def _row_tile(m, target):
    best = None
    for t in range(HG_CHUNK, target + 1, HG_CHUNK):
        if m % t == 0:
            best = t
    assert best is not None
    return best


def _resident(shape):
    zeros = (0,) * len(shape)
    return pl.BlockSpec(shape, lambda *_: zeros, pipeline_mode=pl.Buffered(1))


def _dot(a, b):
    return jnp.dot(a, b, preferred_element_type=F32)


def _dot_nt(a, b):
    return lax.dot_general(a, b, (((1,), (1,)), ((), ())), preferred_element_type=F32)


def _dot_tn(a, b):
    return lax.dot_general(a, b, (((0,), (0,)), ((), ())), preferred_element_type=F32)


def _split2(x):
    hi = x.astype(BF16)
    lo = (x - hi.astype(F32)).astype(BF16)
    return hi, lo


def _split3(x):
    hi = x.astype(BF16)
    r = x - hi.astype(F32)
    mid = r.astype(BF16)
    lo = (r - mid.astype(F32)).astype(BF16)
    return hi, mid, lo


def _silu(x):
    return x * jax.nn.sigmoid(x)


def _mod_rows(mod_ref, slot, is_ctx):
    lo, hi = slot * D_MODEL, (slot + 1) * D_MODEL
    return jnp.where(is_ctx, mod_ref[1:2, lo:hi], mod_ref[0:1, lo:hi])


def _rms(x):
    return x * lax.rsqrt(jnp.mean(x * x, axis=-1, keepdims=True) + EPS)


def _modnorm(x, nw, mod_ref, sh_slot, sc_slot, is_ctx):
    y = _rms(x) * nw
    return y * (1.0 + _mod_rows(mod_ref, sc_slot, is_ctx)) + _mod_rows(mod_ref, sh_slot, is_ctx)


def _is_ctx_rows(row0, rows, n_lat):
    return (row0 + lax.broadcasted_iota(jnp.int32, (rows, 1), 0)) >= n_lat


def _mod_spec(layer):
    return pl.BlockSpec((None, SUBLANES, 6 * D_MODEL), lambda *_: (layer, 0, 0))


def _ada_kernel(c_ref, w_ref, b_ref, o_ref):
    a = _silu(c_ref[...])
    a_hi, a_lo = _split2(a)
    w_hi, w_lo = _split2(w_ref[...])
    o_ref[...] = _dot(a_hi, w_hi) + _dot(a_hi, w_lo) + _dot(a_lo, w_hi) + b_ref[...]


def _ada_table(c, c_ctx, ada_w, ada_b):
    cc = jnp.concatenate([c[0:1], c_ctx[None], jnp.zeros((SUBLANES - 2, D_MODEL), F32)], axis=0)
    n = 6 * D_MODEL
    tn = n // 4
    return pl.pallas_call(
        _ada_kernel,
        grid=(DEPTH, n // tn),
        in_specs=[pl.BlockSpec((SUBLANES, D_MODEL), lambda l, j: (0, 0)),
                  pl.BlockSpec((None, D_MODEL, tn), lambda l, j: (l, 0, j)),
                  pl.BlockSpec((None, 1, tn), lambda l, j: (l, 0, j))],
        out_specs=pl.BlockSpec((None, SUBLANES, tn), lambda l, j: (l, 0, j)),
        out_shape=jax.ShapeDtypeStruct((DEPTH, SUBLANES, n), F32),
        compiler_params=_cparams(("parallel", "parallel")),
    )(cc, ada_w, ada_b.reshape(DEPTH, 1, n))


def _cumsum_rows(x, rev):
    n = x.shape[0]
    row = lax.broadcasted_iota(jnp.int32, x.shape, 0)
    k = 1
    while k < n:
        if rev:
            x = x + jnp.where(row < n - k, pltpu.roll(x, n - k, axis=0), 0.0)
        else:
            x = x + jnp.where(row >= k, pltpu.roll(x, k, axis=0), 0.0)
        k *= 2
    return x


def _stream_rows(x_ref, ctx_ref, is_ctx):
    tm = x_ref.shape[0]
    off = tm - ctx_ref.shape[0]
    ctx_rows = jnp.concatenate([jnp.zeros((off, D_MODEL), F32), ctx_ref[...]], axis=0)
    return jnp.where(is_ctx, ctx_rows, x_ref[...])


def _normed_rows(x_ref, nw_ref, mod_ref, tm, n_lat, ctx_ref=None):
    is_ctx = _is_ctx_rows(pl.program_id(0) * tm, tm, n_lat)
    x = x_ref[...] if ctx_ref is None else _stream_rows(x_ref, ctx_ref, is_ctx)
    return _modnorm(x, nw_ref[...], mod_ref, SH1, SC1, is_ctx).astype(BF16)


def _proj_hgrn_kernel(x_ref, ctx_ref, nw_ref, mod_ref, w_ref, lb_ref, qs_ref, cumf_ref, kf_ref, cumb_ref, kb_ref, v_ref,
                      gs_ref, *, tm, n_lat):
    h = _normed_rows(x_ref, nw_ref, mod_ref, tm, n_lat, ctx_ref)
    zf, zb, q, v, g = (_dot(h, w_ref[:, k * D_MODEL:(k + 1) * D_MODEL]) for k in (1, 2, 0, 3, 4))
    for d, (z, cum_ref, k_ref) in enumerate(((zf, cumf_ref, kf_ref), (zb, cumb_ref, kb_ref))):
        lb = lb_ref[d:d + 1, :]
        f = lb + (1.0 - lb) * jax.nn.sigmoid(z)
        k_ref[...] = (1.0 - f).astype(k_ref.dtype)
        lf = jnp.log(f)
        for c in range(tm // HG_CHUNK):
            rows = slice(c * HG_CHUNK, (c + 1) * HG_CHUNK)
            cum_ref[rows, :] = _cumsum_rows(lf[rows], rev=bool(d))
    qs_ref[...] = _silu(q).astype(qs_ref.dtype)
    v_ref[...] = v.astype(v_ref.dtype)
    gs_ref[...] = _silu(g).astype(gs_ref.dtype)


def _conv_silu(taps, cw_ref, cb_ref, cols):
    acc = cb_ref[:, cols] + cw_ref[0:1, cols] * taps[0]
    for j in range(1, SSM_CONV):
        acc = acc + cw_ref[j:j + 1, cols] * taps[j]
    return _silu(acc)


def _proj_ssd_kernel(x_ref, xp_ref, xn_ref, nw_ref, mod_ref, w_ref, cw_ref, cb_ref, zs_ref, xc_ref, dt_ref,
                     ext_ref, *, tm, n_lat, n_rows):
    i = pl.program_id(0)
    nt = n_rows // tm
    halo = SUBLANES
    half = SSM_CONV // 2
    norm = lambda ref, row0, rows: _modnorm(ref[...], nw_ref[...], mod_ref, SH1, SC1, _is_ctx_rows(row0, rows, n_lat))
    hf = norm(x_ref, i * tm, tm)
    h = hf.astype(BF16)
    for c in range(SSM_D_INNER // D_MODEL):
        sl = slice(c * D_MODEL, (c + 1) * D_MODEL)
        zs_ref[:, sl] = _silu(_dot(h, w_ref[:, sl])).astype(zs_ref.dtype)
    dt2 = _dot(h, w_ref[:, SSM_D_INNER + SSM_XBC:])
    lane_pad = jnp.zeros((tm, LANES - SSM_HEADS), F32)
    dt_ref[...] = jnp.concatenate([dt2[:, :SSM_HEADS], lane_pad, dt2[:, SSM_HEADS:], lane_pad], axis=1)
    h_ext = jnp.concatenate([norm(xp_ref, i * tm - halo, halo), hf, norm(xn_ref, (i + 1) * tm, halo)], axis=0).astype(BF16)
    vp = jnp.where(i > 0, 1.0, 0.0)
    vn = jnp.where(i < nt - 1, 1.0, 0.0)
    chunks = [slice(c * D_MODEL, (c + 1) * D_MODEL) for c in range(SSM_XBC // D_MODEL)]
    for cols in chunks:
        y = _dot(h_ext, w_ref[:, SSM_D_INNER + cols.start:SSM_D_INNER + cols.stop])
        ext_ref[0:halo, cols] = y[0:halo] * vp
        ext_ref[halo:halo + tm, cols] = y[halo:halo + tm]
        ext_ref[halo + tm:, cols] = y[halo + tm:] * vn
    for cols in chunks:
        taps = [ext_ref[halo - half + j:halo - half + j + tm, cols] for j in range(SSM_CONV)]
        xc_ref[:, cols] = _conv_silu(taps, cw_ref, cb_ref, cols).astype(xc_ref.dtype)
    if n_lat % tm:
        bt, off = divmod(n_lat, tm)
        fix = BF16_ROWS
        assert off % fix == 0 and fix <= off <= tm - fix

        @pl.when(i == bt)
        def _():
            e0 = halo + off
            is_lat_row = lax.broadcasted_iota(jnp.int32, (2 * fix + 2 * halo, 1), 0) < fix + halo
            for cols in chunks:
                win = ext_ref[e0 - fix - halo:e0 + fix + halo, cols]
                lat_w = jnp.where(is_lat_row, win, 0.0)
                ctx_w = jnp.where(is_lat_row, 0.0, win)
                lat_taps = [lat_w[halo - half + j:halo - half + j + fix] for j in range(SSM_CONV)]
                ctx_taps = [ctx_w[halo + fix - half + j:halo + fix - half + j + fix] for j in range(SSM_CONV)]
                xc_ref[off - fix:off, cols] = _conv_silu(lat_taps, cw_ref, cb_ref, cols).astype(xc_ref.dtype)
                xc_ref[off:off + fix, cols] = _conv_silu(ctx_taps, cw_ref, cb_ref, cols).astype(xc_ref.dtype)


def _proj_attn_kernel(x_ref, nw_ref, mod_ref, w_ref, ta_ref, tb_ref, o_ref, *, tm, n_lat, q_cols, rope_cols):
    h = _normed_rows(x_ref, nw_ref, mod_ref, tm, n_lat)
    n = w_ref.shape[1]
    gpt = tm // GRID_W
    g0 = pl.program_id(0) * gpt
    tabs = []
    for k in range(ta_ref.shape[0]):
        rows = [ta_ref[k, pl.ds(g0 + g, 1), :] + jnp.where(g0 + g < n_lat // GRID_W, tb_ref[k], 0.0) for g in range(gpt)]
        tabs.append(jnp.concatenate(rows, axis=0))
    cos, sa, sb = tabs
    for c in range(n // MXU_COLS):
        y = _dot(h, w_ref[:, c * MXU_COLS:(c + 1) * MXU_COLS])
        for s_ in range(MXU_COLS // LANES):
            c0 = c * MXU_COLS + s_ * LANES
            ys = y[:, s_ * LANES:(s_ + 1) * LANES]
            if c0 < rope_cols:
                ys = ys * cos + pltpu.roll(ys, LANES - 16, axis=1) * sa + pltpu.roll(ys, 16, axis=1) * sb
            if c0 < q_cols:
                ys = ys * LOG2E
            o_ref[:, c0:c0 + LANES] = ys.astype(o_ref.dtype)


def _proj_call(kernel_fn, x, nw, mods, layer, extra_in, extra_specs, out_shapes, n_lat, scratch=None, halo=False, ctx=None,
               **kw):
    m = out_shapes[0].shape[0]
    tm = _row_tile(m, ROW_TILE)
    row = lambda w: pl.BlockSpec((tm, w), lambda i: (i, 0))
    lead, lead_specs = [x], [row(D_MODEL)]
    if ctx is not None:
        assert n_lat % tm + ctx.shape[0] == tm
        lead.append(ctx)
        lead_specs.append(_resident(ctx.shape))
    if halo:
        tpb = tm // SUBLANES
        lead += [x, x]
        lead_specs += [pl.BlockSpec((SUBLANES, D_MODEL), lambda i: (jnp.maximum(i * tpb - 1, 0), 0)),
                       pl.BlockSpec((SUBLANES, D_MODEL), lambda i: (jnp.minimum((i + 1) * tpb, m // SUBLANES - 1), 0))]
    return pl.pallas_call(
        functools.partial(kernel_fn, tm=tm, n_lat=n_lat, **kw),
        grid=(m // tm,),
        in_specs=lead_specs + [_resident((1, D_MODEL)), _mod_spec(layer)] + [s(tm) if callable(s) else s for s in extra_specs],
        out_specs=[row(s.shape[1]) for s in out_shapes],
        out_shape=out_shapes,
        scratch_shapes=scratch(tm) if scratch else [],
        compiler_params=_cparams(("parallel",)),
    )(*lead, nw, mods, *extra_in)


def _ffn_chunks():
    out, c0 = [], 0
    while c0 < FFN_HIDDEN:
        cw = min(D_MODEL, FFN_HIDDEN - c0)
        assert cw % MXU_COLS == 0
        out.append((c0, cw))
        c0 += cw
    return out


def _mix_ffn_kernel(x_ref, a_ref, wo_ref, *rest, tm, n_lat, pool, final, split):
    rest = list(rest)
    ctx_ref = rest.pop(0) if split else None
    ps_ref = rest.pop(0) if pool else None
    mod_ref, nw_ref, win_ref, wout_ref = rest[:4]
    rest = rest[4:]
    fw_ref = rest.pop(0) if final else None
    (o_ref,) = rest
    is_ctx = _is_ctx_rows(pl.program_id(0) * tm, tm, n_lat)
    if pool:
        parts = [_dot(a_ref[:, g * POOL_GROUP:(g + 1) * POOL_GROUP], wo_ref[g]) for g in range(len(POOL_WINDOWS))]
        y = jnp.concatenate(parts, axis=-1) * ps_ref[...]
    else:
        y = _dot(a_ref[...], wo_ref[...])
    x0 = _stream_rows(x_ref, ctx_ref, is_ctx) if split else x_ref[...]
    x1 = x0 + _mod_rows(mod_ref, G1, is_ctx) * y
    h = _modnorm(x1, nw_ref[...], mod_ref, SH2, SC2, is_ctx).astype(BF16)
    acc = None
    for c0, cw in _ffn_chunks():
        gate = _dot(h, win_ref[:, c0:c0 + cw])
        up = _dot(h, win_ref[:, FFN_HIDDEN + c0:FFN_HIDDEN + c0 + cw])
        part = _dot((_silu(gate) * up).astype(BF16), wout_ref[c0:c0 + cw, :])
        acc = part if acc is None else acc + part
    x2 = x1 + _mod_rows(mod_ref, G2, is_ctx) * acc
    o_ref[...] = _rms(x2) * fw_ref[...] if final else x2


def _mix_ffn(x, a, wo, mods, layer, nw2, w_in, w_out, n_lat, rows, pool_scale=None, final_w=None, ctx=None):
    layer_slab = lambda w: pl.BlockSpec((None,) + w.shape[1:], lambda i: (layer, 0, 0), pipeline_mode=pl.Buffered(1))
    pool = pool_scale is not None
    final = final_w is not None
    tm = _row_tile(rows, ROW_TILE)
    row = lambda w: pl.BlockSpec((tm, w), lambda i: (i, 0))
    in_specs = [row(D_MODEL), row(a.shape[1]), _resident(wo.shape)]
    args = [x, a, wo]
    if ctx is not None:
        assert n_lat % tm + ctx.shape[0] == tm
        in_specs.append(_resident(ctx.shape))
        args.append(ctx)
    if pool:
        in_specs.append(_resident((1, D_MODEL)))
        args.append(pool_scale)
    in_specs += [_mod_spec(layer), _resident((1, D_MODEL)), layer_slab(w_in), layer_slab(w_out)]
    args += [mods, nw2, w_in, w_out]
    if final:
        in_specs.append(_resident((1, D_MODEL)))
        args.append(final_w)
    return pl.pallas_call(
        functools.partial(_mix_ffn_kernel, tm=tm, n_lat=n_lat, pool=pool, final=final, split=ctx is not None),
        grid=(rows // tm,),
        in_specs=in_specs,
        out_specs=row(D_MODEL),
        out_shape=jax.ShapeDtypeStruct((rows, D_MODEL), F32),
        compiler_params=_cparams(("parallel",)),
    )(*args)


def _scan_block(i, nb, rev):
    if rev:
        return jnp.where(i == 0, nb - 1, nb - 1 - i)
    return jnp.where(i == 0, nb - 1, i - 1)


def _hgrn_kernel(q_ref, cum_ref, k_ref, v_ref, *rest, rev):
    if rev:
        g_ref, of_ref, gn_ref, o_ref, st_ref = rest
    else:
        o_ref, st_ref = rest
    L = HG_CHUNK
    nch = TOKEN_BLOCK // L

    @pl.when(pl.program_id(0) == 0)
    def _():
        st_ref[...] = jnp.zeros_like(st_ref)

    r_i = lax.broadcasted_iota(jnp.int32, (L, L), 0)
    c_i = lax.broadcasted_iota(jnp.int32, (L, L), 1)
    keep = (r_i <= c_i) if rev else (r_i >= c_i)
    mid = L // 2 if rev else L // 2 - 1
    end = 0 if rev else L - 1

    heads = [slice(h * HG_DK, (h + 1) * HG_DK) for h in range(HG_HEADS)]
    order = [(c, slice(c * L, (c + 1) * L)) for c in (reversed(range(nch)) if rev else range(nch))]
    att_l, ds_l = {}, {}
    for c, rows in order:
        for h, sl in enumerate(heads):
            q = q_ref[rows, sl].astype(F32)
            k = k_ref[rows, sl].astype(F32)
            cum = cum_ref[rows, sl]
            ref = cum[mid:mid + 1]
            last = cum[end:end + 1]
            att_l[c, h] = _dot_nt((q * jnp.exp(cum - ref)).astype(BF16), (k * jnp.exp(ref - cum)).astype(BF16))
            ds_l[c, h] = _dot_tn(v_ref[rows, sl], (k * jnp.exp(last - cum)).astype(BF16))
    for c, rows in order:
        for h, sl in enumerate(heads):
            cum = cum_ref[rows, sl]
            st = st_ref[h]
            inter = _dot_nt((q_ref[rows, sl].astype(F32) * jnp.exp(cum)).astype(BF16), st.astype(BF16))
            att = jnp.where(keep, att_l[c, h], 0.0).astype(BF16)
            o = inter + _dot(att, v_ref[rows, sl])
            st_ref[h] = st * jnp.exp(cum[end:end + 1]) + ds_l[c, h]
            if rev:
                o = o + of_ref[rows, sl].astype(F32)
                y = _rms(o) * gn_ref[...]
                o_ref[rows, sl] = (y * g_ref[rows, sl].astype(F32)).astype(o_ref.dtype)
            else:
                o_ref[rows, sl] = o.astype(o_ref.dtype)


def _hgrn_scan(qs, cum, k, v, rev, gs=None, o_f=None, gn_w=None):
    m = qs.shape[0]
    nb = m // TOKEN_BLOCK
    blk = functools.partial(_scan_block, nb=nb, rev=rev)
    spec = pl.BlockSpec((TOKEN_BLOCK, D_MODEL), lambda i: (blk(i), 0))
    in_specs = [spec] * 4
    args = [qs, cum, k, v]
    if rev:
        in_specs += [spec, spec, _resident((1, HG_DK))]
        args += [gs, o_f, gn_w]
    return pl.pallas_call(
        functools.partial(_hgrn_kernel, rev=rev),
        grid=(nb,),
        in_specs=in_specs,
        out_specs=spec,
        out_shape=jax.ShapeDtypeStruct((m, D_MODEL), BF16),
        scratch_shapes=[pltpu.VMEM((HG_HEADS, HG_DK, HG_DK), F32)],
        compiler_params=_cparams(("arbitrary",)),
    )(*args)


def _ssd_kernel(xc_ref, dt_ref, dtb_ref, alog_ref, *rest, rev):
    if rev:
        zs_ref, yf_ref, dsk_ref, nw_ref, o_ref, st_ref, stage_ref = rest
    else:
        y_ref, st_ref = rest
    T = TOKEN_BLOCK

    @pl.when(pl.program_id(0) == 0)
    def _():
        st_ref[...] = jnp.zeros_like(st_ref)

    gw = SSM_HPG * SSM_HEADDIM
    groups = [slice(g * gw, (g + 1) * gw) for g in range(SSM_GROUPS)]
    b_l, cb_l, yoff_l = [], [], []
    for g in range(SSM_GROUPS):
        b_l.append(xc_ref[:, SSM_D_INNER + g * SSM_STATE:SSM_D_INNER + (g + 1) * SSM_STATE])
        c_off = SSM_D_INNER + SSM_GROUPS * SSM_STATE + g * SSM_STATE
        c_g = xc_ref[:, c_off:c_off + SSM_STATE]
        cb_l.append(_dot_nt(c_g, b_l[g]))
        yoff_l.append(_dot(c_g, st_ref[g].astype(BF16)))

    dt = jax.nn.softplus(dt_ref[...] + dtb_ref[...])
    dta = dt * (-jnp.exp(alog_ref[...]))
    r_i = lax.broadcasted_iota(jnp.int32, (T, T), 0)
    c_i = lax.broadcasted_iota(jnp.int32, (T, T), 1)
    keep = (r_i <= c_i) if rev else (r_i >= c_i)
    tri = jnp.where(keep, 1.0, 0.0).astype(BF16)
    cum = sum(_dot(tri, p) for p in _split3(dta))
    end = 0 if rev else T - 1
    total = cum[end:end + 1]
    col2 = cum * LOG2E
    row2_t = (col2 - jnp.log2(dt)).T

    half = LANES // 2
    e_h = lax.broadcasted_iota(jnp.int32, (LANES, SSM_D_INNER), 0)
    e_c = lax.broadcasted_iota(jnp.int32, (LANES, SSM_D_INNER), 1)
    expand_m = jnp.where((e_c // SSM_HEADDIM == e_h % half) & (e_h % half < SSM_HEADS), 1.0, 0.0).astype(BF16)

    def expand(f):
        hi = f.astype(BF16).astype(F32)
        lane_f = lax.broadcasted_iota(jnp.int32, f.shape, 1)
        return _dot(jnp.where(lane_f < half, hi, pltpu.roll(f - hi, half, axis=1)).astype(BF16), expand_m)

    ecum_x = expand(jnp.exp(cum))
    fac_x = expand(dt * jnp.exp(total - cum))
    dec_x = expand(jnp.broadcast_to(jnp.exp(total), (SUBLANES, LANES)))[0:1]

    for g, gl in enumerate(groups):
        xs_fac = (xc_ref[:, gl].astype(F32) * fac_x[:, gl]).astype(BF16)
        st_ref[g] = st_ref[g] * dec_x[:, gl] + _dot_tn(b_l[g], xs_fac)
    lane = lax.broadcasted_iota(jnp.int32, (T, LANES), 1)
    first = lane < SSM_HEADDIM
    hb = T // 2
    keep_h = keep[0:hb, 0:hb]
    lo_, hi_ = slice(0, hb), slice(hb, T)
    for g, gl in enumerate(groups):
        cb = cb_l[g]
        for pr in range(SSM_HPG // 2):
            h0 = g * SSM_HPG + 2 * pr
            ol = slice(g * gw + pr * LANES, g * gw + (pr + 1) * LANES)
            xs_p = xc_ref[:, ol]
            ys = []
            for hh in (h0, h0 + 1):
                col = jnp.broadcast_to(col2[:, hh:hh + 1], (T, hb))
                row = row2_t[hh:hh + 1, :]

                def tile(rs, cs, masked):
                    e = jnp.exp2(col[rs] - row[:, cs])
                    if masked:
                        e = jnp.where(keep_h, e, 0.0)
                    return (cb[rs, cs] * e).astype(BF16)

                if rev:
                    top = _dot(jnp.concatenate([tile(lo_, lo_, True), tile(lo_, hi_, False)], axis=1), xs_p)
                    bot = _dot(tile(hi_, hi_, True), xs_p[hb:])
                else:
                    top = _dot(tile(lo_, lo_, True), xs_p[:hb])
                    bot = _dot(jnp.concatenate([tile(hi_, lo_, False), tile(hi_, hi_, True)], axis=1), xs_p)
                ys.append(jnp.concatenate([top, bot], axis=0))
            yp = yoff_l[g][:, pr * LANES:(pr + 1) * LANES] * ecum_x[:, ol] + jnp.where(first, ys[0], ys[1])
            if rev:
                yt = yp + yf_ref[:, ol].astype(F32) + dsk_ref[:, ol] * xc_ref[:, ol].astype(F32)
                stage_ref[:, ol] = yt * zs_ref[:, ol].astype(F32)
            else:
                y_ref[:, ol] = yp.astype(y_ref.dtype)
        if rev:
            o_ref[:, gl] = (_rms(stage_ref[:, gl]) * nw_ref[:, gl]).astype(o_ref.dtype)


def _ssd_scan(rev, xc, dt_raw, dt_bias, a_log, zs=None, y_f=None, d_skip=None, norm_w=None):
    m = dt_raw.shape[0]
    T = TOKEN_BLOCK
    nb = m // T
    blk = functools.partial(_scan_block, nb=nb, rev=rev)
    rows = lambda w: pl.BlockSpec((T, w), lambda i: (blk(i), 0))
    in_specs = [rows(SSM_XBC), pl.BlockSpec((T, LANES), lambda i: (blk(i), 1 if rev else 0)),
                _resident((1, LANES)), _resident((1, LANES))]
    args = [xc, dt_raw, dt_bias, a_log]
    scratch = [pltpu.VMEM((SSM_GROUPS, SSM_STATE, SSM_HPG * SSM_HEADDIM), F32)]
    if rev:
        in_specs += [rows(SSM_D_INNER), rows(SSM_D_INNER), _resident((1, SSM_D_INNER)), _resident((1, SSM_D_INNER))]
        args += [zs, y_f, d_skip, norm_w]
        scratch.append(pltpu.VMEM((T, SSM_D_INNER), F32))
    return pl.pallas_call(
        functools.partial(_ssd_kernel, rev=rev),
        grid=(nb,),
        in_specs=in_specs,
        out_specs=rows(SSM_D_INNER),
        out_shape=jax.ShapeDtypeStruct((m, SSM_D_INNER), BF16),
        scratch_shapes=scratch,
        compiler_params=_cparams(("arbitrary",)),
    )(*args)


def _pool_kernel(x_ref, xp_ref, xn_ref, nw_ref, mod_ref, o_ref, ext_ref, *, nb, n_lat, n_ctx):
    T = TOKEN_BLOCK
    b = pl.program_id(0)
    is_ctx = b == nb - 1
    vp = jnp.where((b > 0) & (b < nb - 1), 1.0, 0.0)
    vn = jnp.where(b < nb - 2, 1.0, 0.0)
    norm = lambda x: _modnorm(x, nw_ref[...], mod_ref, SH1, SC1, is_ctx)
    ext_ref[0:SUBLANES] = norm(xp_ref[...]) * vp
    ext_ref[SUBLANES:SUBLANES + T] = norm(x_ref[...])
    ext_ref[SUBLANES + T:] = norm(xn_ref[...]) * vn
    t = jnp.where(is_ctx, 0, b * T) + lax.broadcasted_iota(jnp.int32, (T, 1), 0)
    seq = jnp.where(is_ctx, n_ctx, n_lat)
    for gi, win in enumerate(POOL_WINDOWS):
        sl = slice(gi * POOL_GROUP, (gi + 1) * POOL_GROUP)
        run = ext_ref[:, sl]
        n_ext = run.shape[0]
        k = 1
        while k < win:
            run = run + pltpu.roll(run, n_ext - k, axis=0)
            k *= 2
        s0 = SUBLANES - win // 2
        acc = run[s0:s0 + T]
        cnt = jnp.minimum(t + (win - win // 2), seq) - jnp.maximum(t - win // 2, 0)
        o_ref[:, sl] = (acc / cnt.astype(F32) - ext_ref[SUBLANES:SUBLANES + T, sl]).astype(o_ref.dtype)


def _pool_delta(x, nw, mods, layer, n_lat, n_ctx):
    m = x.shape[0]
    T = TOKEN_BLOCK
    nb = m // T
    row8 = T // SUBLANES
    return pl.pallas_call(
        functools.partial(_pool_kernel, nb=nb, n_lat=n_lat, n_ctx=n_ctx),
        grid=(nb,),
        in_specs=[pl.BlockSpec((T, D_MODEL), lambda i: (i, 0)),
                  pl.BlockSpec((SUBLANES, D_MODEL), lambda i: (jnp.maximum(i * row8 - 1, 0), 0)),
                  pl.BlockSpec((SUBLANES, D_MODEL), lambda i: (jnp.minimum((i + 1) * row8, m // SUBLANES - 1), 0)),
                  _resident((1, D_MODEL)), _mod_spec(layer)],
        out_specs=pl.BlockSpec((T, D_MODEL), lambda i: (i, 0)),
        out_shape=jax.ShapeDtypeStruct((m, D_MODEL), BF16),
        scratch_shapes=[pltpu.VMEM((T + 2 * SUBLANES, D_MODEL), F32)],
        compiler_params=_cparams(("parallel",)),
    )(x, x, x, nw, mods)


def _attn_kernel(q_ref, kc_ref, kp_ref, kn_ref, kx_ref, band_ref, sink_ref, o_ref, *, n_lat):
    T = TOKEN_BLOCK
    W = ATT_WINDOW
    i = pl.program_id(0)
    kv = jnp.concatenate([kp_ref[...], kc_ref[...], kn_ref[...], kx_ref[...]], axis=0)
    span = T + 2 * W
    rows = ATT_GRP * T
    kpos = i * T - W + lax.broadcasted_iota(jnp.int32, (1, span), 1)
    bias = band_ref[...] + jnp.where((kpos >= 0) & (kpos < n_lat), 0.0, NEG_INF)
    bias = jnp.concatenate([bias] * ATT_GRP, axis=0)
    head_row = lax.broadcasted_iota(jnp.int32, (rows, 1), 0) // T
    kd = ATT_KV_HEADS * ATT_HEAD_DIM
    s_l = []
    for kh in range(ATT_KV_HEADS):
        k = kv[:, kh * ATT_HEAD_DIM:(kh + 1) * ATT_HEAD_DIM]
        q = jnp.concatenate([q_ref[:, (kh * ATT_GRP + g) * ATT_HEAD_DIM:(kh * ATT_GRP + g + 1) * ATT_HEAD_DIM]
                             for g in range(ATT_GRP)], axis=0)
        s_l.append(_dot_nt(q, k))
    for kh in range(ATT_KV_HEADS):
        v = kv[:, kd + kh * ATT_HEAD_DIM:kd + (kh + 1) * ATT_HEAD_DIM]
        s = jnp.concatenate([s_l[kh][:, :span] + bias, s_l[kh][:, span:]], axis=1)
        sink = jnp.zeros((rows, 1), F32)
        for g in range(ATT_GRP):
            sink = jnp.where(head_row == g, sink_ref[kh * ATT_GRP + g] * LOG2E, sink)
        mx = jnp.maximum(jnp.max(s, axis=-1, keepdims=True), sink)
        p = jnp.exp2(s - mx)
        den = jnp.sum(p, axis=-1, keepdims=True) + jnp.exp2(sink - mx)
        o = _dot(p.astype(BF16), v) / den
        for g in range(0, ATT_GRP, 2):
            pair = jnp.concatenate([o[g * T:(g + 1) * T], o[(g + 1) * T:(g + 2) * T]], axis=-1)
            c0 = (kh * ATT_GRP + g) * ATT_HEAD_DIM
            o_ref[:, c0:c0 + 2 * ATT_HEAD_DIM] = pair.astype(o_ref.dtype)


def _attention(qkv, sink, n_lat, n_ctx):
    T = TOKEN_BLOCK
    W = ATT_WINDOW
    nq = n_lat // T
    kvw = 2 * ATT_KV_HEADS * ATT_HEAD_DIM
    kv_col = D_MODEL // kvw
    wpb = T // W
    off = jnp.arange(T + 2 * W)[None, :] - jnp.arange(T)[:, None]
    band = jnp.where((off >= 0) & (off <= 2 * W), 0.0, NEG_INF).astype(F32)
    return pl.pallas_call(
        functools.partial(_attn_kernel, n_lat=n_lat),
        grid=(nq,),
        in_specs=[pl.BlockSpec((T, D_MODEL), lambda i: (i, 0)),
                  pl.BlockSpec((T, kvw), lambda i: (i, kv_col)),
                  pl.BlockSpec((W, kvw), lambda i: (jnp.maximum(i * wpb - 1, 0), kv_col)),
                  pl.BlockSpec((W, kvw), lambda i: (jnp.minimum((i + 1) * wpb, n_lat // W - 1), kv_col)),
                  pl.BlockSpec((n_ctx, kvw), lambda i: (n_lat // n_ctx, kv_col)),
                  _resident(band.shape),
                  pl.BlockSpec(memory_space=pltpu.SMEM)],
        out_specs=pl.BlockSpec((T, D_MODEL), lambda i: (i, 0)),
        out_shape=jax.ShapeDtypeStruct((n_lat, D_MODEL), BF16),
        compiler_params=_cparams(("parallel",)),
    )(qkv, qkv, qkv, qkv, qkv, band, sink)


def _rope_tables(n_lat, n_ctx):
    half = ATT_HEAD_DIM // 2
    inv = ROPE_THETA ** (-jnp.arange(0, half, 2, dtype=F32) / half)
    grid_h = n_lat // GRID_W
    ang_r = jnp.arange(grid_h, dtype=F32)[:, None] * inv[None]
    ang_c = jnp.arange(GRID_W, dtype=F32)[:, None] * inv[None]
    first = (jnp.arange(half) < half // 2)
    signed = lambda sn: (jnp.where(first, -sn, 0.0), jnp.where(first, 0.0, sn))
    reps = LANES // ATT_HEAD_DIM

    def lanes(row_part, col_part):
        return jnp.tile(jnp.concatenate([row_part, col_part], axis=-1), (1, reps))

    def parts(ang, is_row):
        two = jnp.concatenate([ang, ang], axis=-1)
        zero = jnp.zeros_like(two)
        out = []
        for t in (jnp.cos(two),) + signed(jnp.sin(two)):
            out.append(lanes(t, zero) if is_row else lanes(zero, t))
        return out

    ident = [jnp.ones((n_ctx // GRID_W, LANES), F32)] + [jnp.zeros((n_ctx // GRID_W, LANES), F32)] * 2
    ta = jnp.stack([jnp.concatenate([a, i_], axis=0) for a, i_ in zip(parts(ang_r, True), ident)])
    tb = jnp.stack(parts(ang_c, False))
    return ta, tb


def kernel(x, c, ctx, c_ctx, ada_w, ada_b, norm_w, ffn_w_in, ffn_w_out, final_norm_w, hg_w_in, hg_lb, hg_norm_w, hg_w_out, ssm_w_in, ssm_conv_w, ssm_conv_b, ssm_dt_bias, ssm_a_log, ssm_d, ssm_norm_w, ssm_w_out, pool_w, pool_scale, att_w_qkv, att_sink, att_w_out):
    assert x.shape[0] == 1 and DEPTH == 4
    n_lat, n_ctx = x.shape[1], ctx.shape[1]
    assert n_lat % TOKEN_BLOCK == 0 and n_ctx == TOKEN_BLOCK
    m = n_lat + n_ctx
    mods = _ada_table(c, c_ctx, ada_w, ada_b)
    lb_tab = jnp.cumsum(jax.nn.softmax(hg_lb.astype(F32), axis=1), axis=1)
    row = lambda v: v.reshape(1, -1)
    ffn_w = (ffn_w_in.astype(BF16), ffn_w_out.astype(BF16))
    ffn = lambda l: ffn_w
    sds = lambda w, dt: jax.ShapeDtypeStruct((m, w), dt)

    w = hg_w_in[0].astype(BF16)
    qs, cumf, kf, cumb, kb, v, gs = _proj_call(
        _proj_hgrn_kernel, x[0], row(norm_w[0, 0]), mods, 0, [w, lb_tab[:, 0]], [_resident(w.shape), _resident((2, D_MODEL))],
        [sds(D_MODEL, BF16), sds(D_MODEL, F32), sds(D_MODEL, BF16), sds(D_MODEL, F32), sds(D_MODEL, BF16),
         sds(D_MODEL, BF16), sds(D_MODEL, BF16)], n_lat, ctx=ctx[0])
    o_f = _hgrn_scan(qs, cumf, kf, v, False)
    a = _hgrn_scan(qs, cumb, kb, v, True, gs=gs, o_f=o_f, gn_w=row(hg_norm_w[0]))
    xs = _mix_ffn(x[0], a, hg_w_out[0].astype(BF16), mods, 0, row(norm_w[0, 1]), *ffn(0), n_lat, m, ctx=ctx[0])

    w = ssm_w_in[0].astype(BF16)
    zs, xc, dt_raw = _proj_call(
        _proj_ssd_kernel, xs, row(norm_w[1, 0]), mods, 1, [w, ssm_conv_w[0], row(ssm_conv_b[0])],
        [_resident(w.shape), _resident((SSM_CONV, SSM_XBC)), _resident((1, SSM_XBC))],
        [sds(SSM_D_INNER, BF16), sds(SSM_XBC, BF16), sds(2 * LANES, F32)], n_lat, halo=True, n_rows=m,
        scratch=lambda tm: [pltpu.VMEM((tm + 2 * SUBLANES, SSM_XBC), F32)])
    lane_pad = lambda v: jnp.concatenate([v, jnp.zeros((LANES - SSM_HEADS,), F32)]).reshape(1, LANES)
    y_f = _ssd_scan(False, xc, dt_raw, lane_pad(ssm_dt_bias[0, 0]), lane_pad(ssm_a_log[0, 0]))
    a = _ssd_scan(True, xc, dt_raw, lane_pad(ssm_dt_bias[0, 1]), lane_pad(ssm_a_log[0, 1]), zs=zs, y_f=y_f,
                  d_skip=row(jnp.repeat(ssm_d[0], SSM_HEADDIM)), norm_w=row(ssm_norm_w[0]))
    xs = _mix_ffn(xs, a, ssm_w_out[0].astype(BF16), mods, 1, row(norm_w[1, 1]), *ffn(1), n_lat, m)

    a = _pool_delta(xs, row(norm_w[2, 0]), mods, 2, n_lat, n_ctx)
    xs = _mix_ffn(xs, a, pool_w[0].astype(BF16), mods, 2, row(norm_w[2, 1]), *ffn(2), n_lat, m,
                  pool_scale=row(pool_scale[0]))

    ta, tb = _rope_tables(n_lat, n_ctx)
    qd = ATT_Q_HEADS * ATT_HEAD_DIM
    kd = ATT_KV_HEADS * ATT_HEAD_DIM
    wq = jnp.concatenate([att_w_qkv[0][:, :qd] * ATT_HEAD_DIM ** -0.5, att_w_qkv[0][:, qd:]], axis=1).astype(BF16)
    (qkv,) = _proj_call(
        _proj_attn_kernel, xs, row(norm_w[3, 0]), mods, 3, [wq, ta, tb], [_resident(wq.shape), _resident(ta.shape), _resident(tb.shape)],
        [sds(qd + 2 * kd, BF16)], n_lat, q_cols=qd, rope_cols=qd + kd)
    a = _attention(qkv, att_sink[0], n_lat, n_ctx)
    out = _mix_ffn(xs, a, att_w_out[0].astype(BF16), mods, 3, row(norm_w[3, 1]), *ffn(3), n_lat, n_lat,
                   final_w=row(final_norm_w))
    return out[None]
```

```python
import functools

import jax
import jax.numpy as jnp
from jax import lax
from jax.experimental import pallas as pl
from jax.experimental.pallas import tpu as pltpu

F32 = jnp.float32
BF16 = jnp.bfloat16

D_MODEL = 1024
DEPTH = 4
GRID_W = 64
EPS = 1e-6
NEG_INF = -1e30
LOG2E = 1.4426950408889634

HG_HEADS = 8
HG_DK = 128
HG_CHUNK = 64

SSM_D_INNER = 2 * D_MODEL
SSM_HEADDIM = 64
SSM_HEADS = SSM_D_INNER // SSM_HEADDIM
SSM_GROUPS = 4
SSM_STATE = 128
SSM_CONV = 5
SSM_BC = 2 * SSM_GROUPS * SSM_STATE
SSM_XBC = SSM_D_INNER + SSM_BC
SSM_HPG = SSM_HEADS // SSM_GROUPS

POOL_WINDOWS = (2, 4, 8, 16)
POOL_GROUP = D_MODEL // len(POOL_WINDOWS)

ATT_HEAD_DIM = 64
ATT_Q_HEADS = D_MODEL // ATT_HEAD_DIM
ATT_KV_HEADS = 4
ATT_GRP = ATT_Q_HEADS // ATT_KV_HEADS
ATT_WINDOW = 128
ROPE_THETA = 10000.0

FFN_HIDDEN = -(-8 * D_MODEL // (3 * 256)) * 256

LANES = 128
SUBLANES = 8
BF16_ROWS = 16
MXU_COLS = 256
TOKEN_BLOCK = 256
ROW_TILE = 640
VMEM_LIMIT = 56 * 1024 * 1024

SH1, SC1, G1, SH2, SC2, G2 = range(6)


def _cparams(sem):
    return pltpu.CompilerParams(dimension_semantics=sem, vmem_limit_bytes=VMEM_LIMIT)


def _row_tile(m, target):
    best = None
    for t in range(HG_CHUNK, target + 1, HG_CHUNK):
        if m % t == 0:
            best = t
    assert best is not None
    return best


def _resident(shape):
    zeros = (0,) * len(shape)
    return pl.BlockSpec(shape, lambda *_: zeros, pipeline_mode=pl.Buffered(1))


def _dot(a, b):
    return jnp.dot(a, b, preferred_element_type=F32)


def _dot_nt(a, b):
    return lax.dot_general(a, b, (((1,), (1,)), ((), ())), preferred_element_type=F32)


def _dot_tn(a, b):
    return lax.dot_general(a, b, (((0,), (0,)), ((), ())), preferred_element_type=F32)


def _split2(x):
    hi = x.astype(BF16)
    lo = (x - hi.astype(F32)).astype(BF16)
    return hi, lo


def _split3(x):
    hi = x.astype(BF16)
    r = x - hi.astype(F32)
    mid = r.astype(BF16)
    lo = (r - mid.astype(F32)).astype(BF16)
    return hi, mid, lo


def _silu(x):
    return x * jax.nn.sigmoid(x)


def _mod_vec(mod_ref, slot, is_ctx):
    lo, hi = slot * D_MODEL, (slot + 1) * D_MODEL
    return jnp.where(is_ctx, mod_ref[1:2, lo:hi], mod_ref[0:1, lo:hi])


def _rms(x):
    return x * lax.rsqrt(jnp.mean(x * x, axis=-1, keepdims=True) + EPS)


def _modnorm(x, nw, mod_ref, sh_slot, sc_slot, is_ctx):
    gain = nw * (1.0 + _mod_vec(mod_ref, sc_slot, is_ctx))
    return _rms(x) * gain + _mod_vec(mod_ref, sh_slot, is_ctx)


def _per_segment(tile, tm, n_lat, fn):
    bt, off = divmod(n_lat, tm)
    if off == 0:
        return fn(slice(0, tm), tile >= bt)
    return jnp.concatenate([fn(slice(0, off), tile > bt), fn(slice(off, tm), tile >= bt)], axis=0)


def _mod_spec(layer):
    return pl.BlockSpec((None, SUBLANES, 6 * D_MODEL), lambda *_: (layer, 0, 0))


def _ada_kernel(c_ref, w_ref, b_ref, o_ref):
    a = _silu(c_ref[...])
    a_hi, a_lo = _split2(a)
    w_hi, w_lo = _split2(w_ref[...])
    o_ref[...] = _dot(a_hi, w_hi) + _dot(a_hi, w_lo) + _dot(a_lo, w_hi) + b_ref[...]


def _ada_table(c, c_ctx, ada_w, ada_b):
    cc = jnp.concatenate([c[0:1], c_ctx[None], jnp.zeros((SUBLANES - 2, D_MODEL), F32)], axis=0)
    n = 6 * D_MODEL
    tn = n // 4
    return pl.pallas_call(
        _ada_kernel,
        grid=(DEPTH, n // tn),
        in_specs=[pl.BlockSpec((SUBLANES, D_MODEL), lambda l, j: (0, 0)),
                  pl.BlockSpec((None, D_MODEL, tn), lambda l, j: (l, 0, j)),
                  pl.BlockSpec((None, 1, tn), lambda l, j: (l, 0, j))],
        out_specs=pl.BlockSpec((None, SUBLANES, tn), lambda l, j: (l, 0, j)),
        out_shape=jax.ShapeDtypeStruct((DEPTH, SUBLANES, n), F32),
        compiler_params=_cparams(("parallel", "parallel")),
    )(cc, ada_w, ada_b.reshape(DEPTH, 1, n))


def _cumsum_rows(x, rev):
    n = x.shape[0]
    row = lax.broadcasted_iota(jnp.int32, x.shape, 0)
    k = 1
    while k < n:
        if rev:
            x = x + jnp.where(row < n - k, pltpu.roll(x, n - k, axis=0), 0.0)
        else:
            x = x + jnp.where(row >= k, pltpu.roll(x, k, axis=0), 0.0)
        k *= 2
    return x


def _stream_rows(x_ref, ctx_ref, tile, n_lat):
    tm = x_ref.shape[0]
    off = tm - ctx_ref.shape[0]
    tail = jnp.where(tile >= n_lat // tm, ctx_ref[...], x_ref[off:tm, :])
    return jnp.concatenate([x_ref[0:off, :], tail], axis=0)


def _normed_rows(x_ref, nw_ref, mod_ref, tm, n_lat, ctx_ref=None):
    tile = pl.program_id(0)
    x = x_ref[...] if ctx_ref is None else _stream_rows(x_ref, ctx_ref, tile, n_lat)
    nw = nw_ref[...]
    return _per_segment(tile, tm, n_lat, lambda rows, c: _modnorm(x[rows], nw, mod_ref, SH1, SC1, c)).astype(BF16)


def _proj_hgrn_kernel(x_ref, ctx_ref, nw_ref, mod_ref, w_ref, lb_ref, qs_ref, cumf_ref, kf_ref, cumb_ref, kb_ref, v_ref,
                      gs_ref, *, tm, n_lat):
    h = _normed_rows(x_ref, nw_ref, mod_ref, tm, n_lat, ctx_ref)
    zf, zb, q, v, g = (_dot(h, w_ref[:, k * D_MODEL:(k + 1) * D_MODEL]) for k in (1, 2, 0, 3, 4))
    for d, (z, cum_ref, k_ref) in enumerate(((zf, cumf_ref, kf_ref), (zb, cumb_ref, kb_ref))):
        lb = lb_ref[d:d + 1, :]
        f = lb + (1.0 - lb) * jax.nn.sigmoid(z)
        k_ref[...] = (1.0 - f).astype(k_ref.dtype)
        lf = jnp.log(f)
        for c in range(tm // HG_CHUNK):
            rows = slice(c * HG_CHUNK, (c + 1) * HG_CHUNK)
            cum_ref[rows, :] = _cumsum_rows(lf[rows], rev=bool(d))
    qs_ref[...] = _silu(q).astype(qs_ref.dtype)
    v_ref[...] = v.astype(v_ref.dtype)
    gs_ref[...] = _silu(g).astype(gs_ref.dtype)


def _conv_silu(taps, cw_ref, cb_ref, cols):
    acc = cb_ref[:, cols] + cw_ref[0:1, cols] * taps[0]
    for j in range(1, SSM_CONV):
        acc = acc + cw_ref[j:j + 1, cols] * taps[j]
    return _silu(acc)


def _proj_ssd_kernel(x_ref, xp_ref, xn_ref, nw_ref, mod_ref, w_ref, cw_ref, cb_ref, zs_ref, xc_ref, dt_ref,
                     ext_ref, *, tm, n_lat, n_rows):
    i = pl.program_id(0)
    nt = n_rows // tm
    halo = SUBLANES
    half = SSM_CONV // 2
    nw = nw_ref[...]
    hf = _per_segment(i, tm, n_lat, lambda rows, c: _modnorm(x_ref[rows, :], nw, mod_ref, SH1, SC1, c))
    halo_norm = lambda ref, row0: _modnorm(ref[...], nw, mod_ref, SH1, SC1, row0 >= n_lat)
    h = hf.astype(BF16)
    for c in range(SSM_D_INNER // D_MODEL):
        sl = slice(c * D_MODEL, (c + 1) * D_MODEL)
        zs_ref[:, sl] = _silu(_dot(h, w_ref[:, sl])).astype(zs_ref.dtype)
    dt2 = _dot(h, w_ref[:, SSM_D_INNER + SSM_XBC:])
    lane_pad = jnp.zeros((tm, LANES - SSM_HEADS), F32)
    dt_ref[...] = jnp.concatenate([dt2[:, :SSM_HEADS], lane_pad, dt2[:, SSM_HEADS:], lane_pad], axis=1)
    h_ext = jnp.concatenate([halo_norm(xp_ref, i * tm - halo), hf, halo_norm(xn_ref, (i + 1) * tm)], axis=0).astype(BF16)
    vp = jnp.where(i > 0, 1.0, 0.0)
    vn = jnp.where(i < nt - 1, 1.0, 0.0)
    chunks = [slice(c * D_MODEL, (c + 1) * D_MODEL) for c in range(SSM_XBC // D_MODEL)]
    for cols in chunks:
        y = _dot(h_ext, w_ref[:, SSM_D_INNER + cols.start:SSM_D_INNER + cols.stop])
        ext_ref[0:halo, cols] = y[0:halo] * vp
        ext_ref[halo:halo + tm, cols] = y[halo:halo + tm]
        ext_ref[halo + tm:, cols] = y[halo + tm:] * vn
    for cols in chunks:
        taps = [ext_ref[halo - half + j:halo - half + j + tm, cols] for j in range(SSM_CONV)]
        xc_ref[:, cols] = _conv_silu(taps, cw_ref, cb_ref, cols).astype(xc_ref.dtype)
    if n_lat % tm:
        bt, off = divmod(n_lat, tm)
        fix = BF16_ROWS
        assert off % fix == 0 and fix <= off <= tm - fix

        @pl.when(i == bt)
        def _():
            e0 = halo + off
            is_lat_row = lax.broadcasted_iota(jnp.int32, (2 * fix + 2 * halo, 1), 0) < fix + halo
            for cols in chunks:
                win = ext_ref[e0 - fix - halo:e0 + fix + halo, cols]
                lat_w = jnp.where(is_lat_row, win, 0.0)
                ctx_w = jnp.where(is_lat_row, 0.0, win)
                lat_taps = [lat_w[halo - half + j:halo - half + j + fix] for j in range(SSM_CONV)]
                ctx_taps = [ctx_w[halo + fix - half + j:halo + fix - half + j + fix] for j in range(SSM_CONV)]
                xc_ref[off - fix:off, cols] = _conv_silu(lat_taps, cw_ref, cb_ref, cols).astype(xc_ref.dtype)
                xc_ref[off:off + fix, cols] = _conv_silu(ctx_taps, cw_ref, cb_ref, cols).astype(xc_ref.dtype)


def _proj_attn_kernel(x_ref, nw_ref, mod_ref, w_ref, ta_ref, tb_ref, o_ref, *, tm, n_lat, q_cols, rope_cols):
    h = _normed_rows(x_ref, nw_ref, mod_ref, tm, n_lat)
    n = w_ref.shape[1]
    gpt = tm // GRID_W
    g0 = pl.program_id(0) * gpt
    tabs = []
    for k in range(ta_ref.shape[0]):
        rows = [ta_ref[k, pl.ds(g0 + g, 1), :] + jnp.where(g0 + g < n_lat // GRID_W, tb_ref[k], 0.0) for g in range(gpt)]
        tabs.append(jnp.concatenate(rows, axis=0))
    cos, sa, sb = tabs
    for c in range(n // MXU_COLS):
        y = _dot(h, w_ref[:, c * MXU_COLS:(c + 1) * MXU_COLS])
        for s_ in range(MXU_COLS // LANES):
            c0 = c * MXU_COLS + s_ * LANES
            ys = y[:, s_ * LANES:(s_ + 1) * LANES]
            if c0 < rope_cols:
                ys = ys * cos + pltpu.roll(ys, LANES - 16, axis=1) * sa + pltpu.roll(ys, 16, axis=1) * sb
            if c0 < q_cols:
                ys = ys * LOG2E
            o_ref[:, c0:c0 + LANES] = ys.astype(o_ref.dtype)


def _proj_call(kernel_fn, x, nw, mods, layer, extra_in, extra_specs, out_shapes, n_lat, scratch=None, halo=False, ctx=None,
               **kw):
    m = out_shapes[0].shape[0]
    tm = _row_tile(m, ROW_TILE)
    row = lambda w: pl.BlockSpec((tm, w), lambda i: (i, 0))
    lead, lead_specs = [x], [row(D_MODEL)]
    if ctx is not None:
        assert n_lat % tm + ctx.shape[0] == tm
        lead.append(ctx)
        lead_specs.append(_resident(ctx.shape))
    if halo:
        tpb = tm // SUBLANES
        lead += [x, x]
        lead_specs += [pl.BlockSpec((SUBLANES, D_MODEL), lambda i: (jnp.maximum(i * tpb - 1, 0), 0)),
                       pl.BlockSpec((SUBLANES, D_MODEL), lambda i: (jnp.minimum((i + 1) * tpb, m // SUBLANES - 1), 0))]
    return pl.pallas_call(
        functools.partial(kernel_fn, tm=tm, n_lat=n_lat, **kw),
        grid=(m // tm,),
        in_specs=lead_specs + [_resident((1, D_MODEL)), _mod_spec(layer)] + [s(tm) if callable(s) else s for s in extra_specs],
        out_specs=[row(s.shape[1]) for s in out_shapes],
        out_shape=out_shapes,
        scratch_shapes=scratch(tm) if scratch else [],
        compiler_params=_cparams(("parallel",)),
    )(*lead, nw, mods, *extra_in)


def _ffn_chunks():
    out, c0 = [], 0
    while c0 < FFN_HIDDEN:
        cw = min(D_MODEL, FFN_HIDDEN - c0)
        assert cw % MXU_COLS == 0
        out.append((c0, cw))
        c0 += cw
    return out


def _mix_ffn_kernel(x_ref, a_ref, wo_ref, *rest, tm, n_lat, pool, final, split):
    rest = list(rest)
    ctx_ref = rest.pop(0) if split else None
    ps_ref = rest.pop(0) if pool else None
    mod_ref, nw_ref, win_ref, wout_ref = rest[:4]
    rest = rest[4:]
    fw_ref = rest.pop(0) if final else None
    (o_ref,) = rest
    tile = pl.program_id(0)
    seg = functools.partial(_per_segment, tile, tm, n_lat)
    if pool:
        parts = [_dot(a_ref[:, g * POOL_GROUP:(g + 1) * POOL_GROUP], wo_ref[g]) for g in range(len(POOL_WINDOWS))]
        y = jnp.concatenate(parts, axis=-1) * ps_ref[...]
    else:
        y = _dot(a_ref[...], wo_ref[...])
    x0 = _stream_rows(x_ref, ctx_ref, tile, n_lat) if split else x_ref[...]
    nw = nw_ref[...]
    x1 = seg(lambda rows, c: x0[rows] + _mod_vec(mod_ref, G1, c) * y[rows])
    h = seg(lambda rows, c: _modnorm(x1[rows], nw, mod_ref, SH2, SC2, c)).astype(BF16)
    acc = None
    for c0, cw in _ffn_chunks():
        gate = _dot(h, win_ref[:, c0:c0 + cw])
        up = _dot(h, win_ref[:, FFN_HIDDEN + c0:FFN_HIDDEN + c0 + cw])
        part = _dot((_silu(gate) * up).astype(BF16), wout_ref[c0:c0 + cw, :])
        acc = part if acc is None else acc + part
    x2 = seg(lambda rows, c: x1[rows] + _mod_vec(mod_ref, G2, c) * acc[rows])
    o_ref[...] = _rms(x2) * fw_ref[...] if final else x2


def _mix_ffn(x, a, wo, mods, layer, nw2, w_in, w_out, n_lat, rows, pool_scale=None, final_w=None, ctx=None):
    layer_slab = lambda w: pl.BlockSpec((None,) + w.shape[1:], lambda i: (layer, 0, 0), pipeline_mode=pl.Buffered(1))
    pool = pool_scale is not None
    final = final_w is not None
    tm = _row_tile(rows, ROW_TILE)
    row = lambda w: pl.BlockSpec((tm, w), lambda i: (i, 0))
    in_specs = [row(D_MODEL), row(a.shape[1]), _resident(wo.shape)]
    args = [x, a, wo]
    if ctx is not None:
        assert n_lat % tm + ctx.shape[0] == tm
        in_specs.append(_resident(ctx.shape))
        args.append(ctx)
    if pool:
        in_specs.append(_resident((1, D_MODEL)))
        args.append(pool_scale)
    in_specs += [_mod_spec(layer), _resident((1, D_MODEL)), layer_slab(w_in), layer_slab(w_out)]
    args += [mods, nw2, w_in, w_out]
    if final:
        in_specs.append(_resident((1, D_MODEL)))
        args.append(final_w)
    return pl.pallas_call(
        functools.partial(_mix_ffn_kernel, tm=tm, n_lat=n_lat, pool=pool, final=final, split=ctx is not None),
        grid=(rows // tm,),
        in_specs=in_specs,
        out_specs=row(D_MODEL),
        out_shape=jax.ShapeDtypeStruct((rows, D_MODEL), F32),
        compiler_params=_cparams(("parallel",)),
    )(*args)


def _scan_block(i, nb, rev):
    if rev:
        return jnp.where(i == 0, nb - 1, nb - 1 - i)
    return jnp.where(i == 0, nb - 1, i - 1)


def _hgrn_kernel(q_ref, cum_ref, k_ref, v_ref, *rest, rev):
    if rev:
        g_ref, of_ref, gn_ref, o_ref, st_ref = rest
    else:
        o_ref, st_ref = rest
    L = HG_CHUNK
    nch = TOKEN_BLOCK // L

    @pl.when(pl.program_id(0) == 0)
    def _():
        st_ref[...] = jnp.zeros_like(st_ref)

    r_i = lax.broadcasted_iota(jnp.int32, (L, L), 0)
    c_i = lax.broadcasted_iota(jnp.int32, (L, L), 1)
    keep = (r_i <= c_i) if rev else (r_i >= c_i)
    mid = L // 2 if rev else L // 2 - 1
    end = 0 if rev else L - 1

    heads = [slice(h * HG_DK, (h + 1) * HG_DK) for h in range(HG_HEADS)]
    order = [(c, slice(c * L, (c + 1) * L)) for c in (reversed(range(nch)) if rev else range(nch))]
    att_l, ds_l = {}, {}
    for c, rows in order:
        for h, sl in enumerate(heads):
            q = q_ref[rows, sl].astype(F32)
            k = k_ref[rows, sl].astype(F32)
            cum = cum_ref[rows, sl]
            ref = cum[mid:mid + 1]
            last = cum[end:end + 1]
            att_l[c, h] = _dot_nt((q * jnp.exp(cum - ref)).astype(BF16), (k * jnp.exp(ref - cum)).astype(BF16))
            ds_l[c, h] = _dot_tn(v_ref[rows, sl], (k * jnp.exp(last - cum)).astype(BF16))
    for c, rows in order:
        for h, sl in enumerate(heads):
            cum = cum_ref[rows, sl]
            st = st_ref[h]
            inter = _dot_nt((q_ref[rows, sl].astype(F32) * jnp.exp(cum)).astype(BF16), st.astype(BF16))
            att = jnp.where(keep, att_l[c, h], 0.0).astype(BF16)
            o = inter + _dot(att, v_ref[rows, sl])
            st_ref[h] = st * jnp.exp(cum[end:end + 1]) + ds_l[c, h]
            if rev:
                o = o + of_ref[rows, sl].astype(F32)
                y = _rms(o) * gn_ref[...]
                o_ref[rows, sl] = (y * g_ref[rows, sl].astype(F32)).astype(o_ref.dtype)
            else:
                o_ref[rows, sl] = o.astype(o_ref.dtype)


def _hgrn_scan(qs, cum, k, v, rev, gs=None, o_f=None, gn_w=None):
    m = qs.shape[0]
    nb = m // TOKEN_BLOCK
    blk = functools.partial(_scan_block, nb=nb, rev=rev)
    spec = pl.BlockSpec((TOKEN_BLOCK, D_MODEL), lambda i: (blk(i), 0))
    in_specs = [spec] * 4
    args = [qs, cum, k, v]
    if rev:
        in_specs += [spec, spec, _resident((1, HG_DK))]
        args += [gs, o_f, gn_w]
    return pl.pallas_call(
        functools.partial(_hgrn_kernel, rev=rev),
        grid=(nb,),
        in_specs=in_specs,
        out_specs=spec,
        out_shape=jax.ShapeDtypeStruct((m, D_MODEL), BF16),
        scratch_shapes=[pltpu.VMEM((HG_HEADS, HG_DK, HG_DK), F32)],
        compiler_params=_cparams(("arbitrary",)),
    )(*args)


def _ssd_kernel(xc_ref, dt_ref, dtb_ref, alog_ref, *rest, rev):
    if rev:
        zs_ref, yf_ref, dsk_ref, nw_ref, o_ref, st_ref, stage_ref = rest
    else:
        y_ref, st_ref = rest
    T = TOKEN_BLOCK

    @pl.when(pl.program_id(0) == 0)
    def _():
        st_ref[...] = jnp.zeros_like(st_ref)

    gw = SSM_HPG * SSM_HEADDIM
    groups = [slice(g * gw, (g + 1) * gw) for g in range(SSM_GROUPS)]
    b_l, cb_l, yoff_l = [], [], []
    for g in range(SSM_GROUPS):
        b_l.append(xc_ref[:, SSM_D_INNER + g * SSM_STATE:SSM_D_INNER + (g + 1) * SSM_STATE])
        c_off = SSM_D_INNER + SSM_GROUPS * SSM_STATE + g * SSM_STATE
        c_g = xc_ref[:, c_off:c_off + SSM_STATE]
        cb_l.append(_dot_nt(c_g, b_l[g]))
        yoff_l.append(_dot(c_g, st_ref[g].astype(BF16)))

    dt = jax.nn.softplus(dt_ref[...] + dtb_ref[...])
    dta = dt * (-jnp.exp(alog_ref[...]))
    r_i = lax.broadcasted_iota(jnp.int32, (T, T), 0)
    c_i = lax.broadcasted_iota(jnp.int32, (T, T), 1)
    keep = (r_i <= c_i) if rev else (r_i >= c_i)
    tri = jnp.where(keep, 1.0, 0.0).astype(BF16)
    cum = sum(_dot(tri, p) for p in _split3(dta))
    end = 0 if rev else T - 1
    total = cum[end:end + 1]
    col2 = cum * LOG2E
    row2_t = (col2 - jnp.log2(dt)).T

    half = LANES // 2
    e_h = lax.broadcasted_iota(jnp.int32, (LANES, SSM_D_INNER), 0)
    e_c = lax.broadcasted_iota(jnp.int32, (LANES, SSM_D_INNER), 1)
    expand_m = jnp.where((e_c // SSM_HEADDIM == e_h % half) & (e_h % half < SSM_HEADS), 1.0, 0.0).astype(BF16)

    def expand(f):
        hi = f.astype(BF16).astype(F32)
        lane_f = lax.broadcasted_iota(jnp.int32, f.shape, 1)
        return _dot(jnp.where(lane_f < half, hi, pltpu.roll(f - hi, half, axis=1)).astype(BF16), expand_m)

    ecum_x = expand(jnp.exp(cum))
    fac_x = expand(dt * jnp.exp(total - cum))
    dec_x = expand(jnp.broadcast_to(jnp.exp(total), (SUBLANES, LANES)))[0:1]

    for g, gl in enumerate(groups):
        xs_fac = (xc_ref[:, gl].astype(F32) * fac_x[:, gl]).astype(BF16)
        st_ref[g] = st_ref[g] * dec_x[:, gl] + _dot_tn(b_l[g], xs_fac)
    lane = lax.broadcasted_iota(jnp.int32, (T, LANES), 1)
    first = lane < SSM_HEADDIM
    hb = T // 2
    keep_h = keep[0:hb, 0:hb]
    lo_, hi_ = slice(0, hb), slice(hb, T)
    for g, gl in enumerate(groups):
        cb = cb_l[g]
        for pr in range(SSM_HPG // 2):
            h0 = g * SSM_HPG + 2 * pr
            ol = slice(g * gw + pr * LANES, g * gw + (pr + 1) * LANES)
            xs_p = xc_ref[:, ol]
            ys = []
            for hh in (h0, h0 + 1):
                col = jnp.broadcast_to(col2[:, hh:hh + 1], (T, hb))
                row = row2_t[hh:hh + 1, :]

                def tile(rs, cs, masked):
                    e = jnp.exp2(col[rs] - row[:, cs])
                    if masked:
                        e = jnp.where(keep_h, e, 0.0)
                    return (cb[rs, cs] * e).astype(BF16)

                if rev:
                    top = _dot(jnp.concatenate([tile(lo_, lo_, True), tile(lo_, hi_, False)], axis=1), xs_p)
                    bot = _dot(tile(hi_, hi_, True), xs_p[hb:])
                else:
                    top = _dot(tile(lo_, lo_, True), xs_p[:hb])
                    bot = _dot(jnp.concatenate([tile(hi_, lo_, False), tile(hi_, hi_, True)], axis=1), xs_p)
                ys.append(jnp.concatenate([top, bot], axis=0))
            yp = yoff_l[g][:, pr * LANES:(pr + 1) * LANES] * ecum_x[:, ol] + jnp.where(first, ys[0], ys[1])
            if rev:
                yt = yp + yf_ref[:, ol].astype(F32) + dsk_ref[:, ol] * xc_ref[:, ol].astype(F32)
                stage_ref[:, ol] = yt * zs_ref[:, ol].astype(F32)
            else:
                y_ref[:, ol] = yp.astype(y_ref.dtype)
        if rev:
            o_ref[:, gl] = (_rms(stage_ref[:, gl]) * nw_ref[:, gl]).astype(o_ref.dtype)


def _ssd_scan(rev, xc, dt_raw, dt_bias, a_log, zs=None, y_f=None, d_skip=None, norm_w=None):
    m = dt_raw.shape[0]
    T = TOKEN_BLOCK
    nb = m // T
    blk = functools.partial(_scan_block, nb=nb, rev=rev)
    rows = lambda w: pl.BlockSpec((T, w), lambda i: (blk(i), 0))
    in_specs = [rows(SSM_XBC), pl.BlockSpec((T, LANES), lambda i: (blk(i), 1 if rev else 0)),
                _resident((1, LANES)), _resident((1, LANES))]
    args = [xc, dt_raw, dt_bias, a_log]
    scratch = [pltpu.VMEM((SSM_GROUPS, SSM_STATE, SSM_HPG * SSM_HEADDIM), F32)]
    if rev:
        in_specs += [rows(SSM_D_INNER), rows(SSM_D_INNER), _resident((1, SSM_D_INNER)), _resident((1, SSM_D_INNER))]
        args += [zs, y_f, d_skip, norm_w]
        scratch.append(pltpu.VMEM((T, SSM_D_INNER), F32))
    return pl.pallas_call(
        functools.partial(_ssd_kernel, rev=rev),
        grid=(nb,),
        in_specs=in_specs,
        out_specs=rows(SSM_D_INNER),
        out_shape=jax.ShapeDtypeStruct((m, SSM_D_INNER), BF16),
        scratch_shapes=scratch,
        compiler_params=_cparams(("arbitrary",)),
    )(*args)


def _pool_kernel(x_ref, xp_ref, xn_ref, nw_ref, mod_ref, o_ref, ext_ref, *, nb, n_lat, n_ctx):
    T = TOKEN_BLOCK
    b = pl.program_id(0)
    is_ctx = b == nb - 1
    vp = jnp.where((b > 0) & (b < nb - 1), 1.0, 0.0)
    vn = jnp.where(b < nb - 2, 1.0, 0.0)
    norm = lambda x: _modnorm(x, nw_ref[...], mod_ref, SH1, SC1, is_ctx)
    ext_ref[0:SUBLANES] = norm(xp_ref[...]) * vp
    ext_ref[SUBLANES:SUBLANES + T] = norm(x_ref[...])
    ext_ref[SUBLANES + T:] = norm(xn_ref[...]) * vn
    t = jnp.where(is_ctx, 0, b * T) + lax.broadcasted_iota(jnp.int32, (T, 1), 0)
    seq = jnp.where(is_ctx, n_ctx, n_lat)
    for gi, win in enumerate(POOL_WINDOWS):
        sl = slice(gi * POOL_GROUP, (gi + 1) * POOL_GROUP)
        run = ext_ref[:, sl]
        n_ext = run.shape[0]
        k = 1
        while k < win:
            run = run + pltpu.roll(run, n_ext - k, axis=0)
            k *= 2
        s0 = SUBLANES - win // 2
        acc = run[s0:s0 + T]
        cnt = jnp.minimum(t + (win - win // 2), seq) - jnp.maximum(t - win // 2, 0)
        o_ref[:, sl] = (acc / cnt.astype(F32) - ext_ref[SUBLANES:SUBLANES + T, sl]).astype(o_ref.dtype)


def _pool_delta(x, nw, mods, layer, n_lat, n_ctx):
    m = x.shape[0]
    T = TOKEN_BLOCK
    nb = m // T
    row8 = T // SUBLANES
    return pl.pallas_call(
        functools.partial(_pool_kernel, nb=nb, n_lat=n_lat, n_ctx=n_ctx),
        grid=(nb,),
        in_specs=[pl.BlockSpec((T, D_MODEL), lambda i: (i, 0)),
                  pl.BlockSpec((SUBLANES, D_MODEL), lambda i: (jnp.maximum(i * row8 - 1, 0), 0)),
                  pl.BlockSpec((SUBLANES, D_MODEL), lambda i: (jnp.minimum((i + 1) * row8, m // SUBLANES - 1), 0)),
                  _resident((1, D_MODEL)), _mod_spec(layer)],
        out_specs=pl.BlockSpec((T, D_MODEL), lambda i: (i, 0)),
        out_shape=jax.ShapeDtypeStruct((m, D_MODEL), BF16),
        scratch_shapes=[pltpu.VMEM((T + 2 * SUBLANES, D_MODEL), F32)],
        compiler_params=_cparams(("parallel",)),
    )(x, x, x, nw, mods)


def _attn_kernel(q_ref, kc_ref, kp_ref, kn_ref, kx_ref, band_ref, sink_ref, o_ref, *, n_lat):
    T = TOKEN_BLOCK
    W = ATT_WINDOW
    i = pl.program_id(0)
    kv = jnp.concatenate([kp_ref[...], kc_ref[...], kn_ref[...], kx_ref[...]], axis=0)
    span = T + 2 * W
    rows = ATT_GRP * T
    kpos = i * T - W + lax.broadcasted_iota(jnp.int32, (1, span), 1)
    bias = band_ref[...] + jnp.where((kpos >= 0) & (kpos < n_lat), 0.0, NEG_INF)
    bias = jnp.concatenate([bias] * ATT_GRP, axis=0)
    head_row = lax.broadcasted_iota(jnp.int32, (rows, 1), 0) // T
    kd = ATT_KV_HEADS * ATT_HEAD_DIM
    s_l = []
    for kh in range(ATT_KV_HEADS):
        k = kv[:, kh * ATT_HEAD_DIM:(kh + 1) * ATT_HEAD_DIM]
        q = jnp.concatenate([q_ref[:, (kh * ATT_GRP + g) * ATT_HEAD_DIM:(kh * ATT_GRP + g + 1) * ATT_HEAD_DIM]
                             for g in range(ATT_GRP)], axis=0)
        s_l.append(_dot_nt(q, k))
    for kh in range(ATT_KV_HEADS):
        v = kv[:, kd + kh * ATT_HEAD_DIM:kd + (kh + 1) * ATT_HEAD_DIM]
        s = jnp.concatenate([s_l[kh][:, :span] + bias, s_l[kh][:, span:]], axis=1)
        sink = jnp.zeros((rows, 1), F32)
        for g in range(ATT_GRP):
            sink = jnp.where(head_row == g, sink_ref[kh * ATT_GRP + g] * LOG2E, sink)
        mx = jnp.maximum(jnp.max(s, axis=-1, keepdims=True), sink)
        p = jnp.exp2(s - mx)
        den = jnp.sum(p, axis=-1, keepdims=True) + jnp.exp2(sink - mx)
        o = _dot(p.astype(BF16), v) / den
        for g in range(0, ATT_GRP, 2):
            pair = jnp.concatenate([o[g * T:(g + 1) * T], o[(g + 1) * T:(g + 2) * T]], axis=-1)
            c0 = (kh * ATT_GRP + g) * ATT_HEAD_DIM
            o_ref[:, c0:c0 + 2 * ATT_HEAD_DIM] = pair.astype(o_ref.dtype)


def _attention(qkv, sink, n_lat, n_ctx):
    T = TOKEN_BLOCK
    W = ATT_WINDOW
    nq = n_lat // T
    kvw = 2 * ATT_KV_HEADS * ATT_HEAD_DIM
    kv_col = D_MODEL // kvw
    wpb = T // W
    off = jnp.arange(T + 2 * W)[None, :] - jnp.arange(T)[:, None]
    band = jnp.where((off >= 0) & (off <= 2 * W), 0.0, NEG_INF).astype(F32)
    return pl.pallas_call(
        functools.partial(_attn_kernel, n_lat=n_lat),
        grid=(nq,),
        in_specs=[pl.BlockSpec((T, D_MODEL), lambda i: (i, 0)),
                  pl.BlockSpec((T, kvw), lambda i: (i, kv_col)),
                  pl.BlockSpec((W, kvw), lambda i: (jnp.maximum(i * wpb - 1, 0), kv_col)),
                  pl.BlockSpec((W, kvw), lambda i: (jnp.minimum((i + 1) * wpb, n_lat // W - 1), kv_col)),
                  pl.BlockSpec((n_ctx, kvw), lambda i: (n_lat // n_ctx, kv_col)),
                  _resident(band.shape),
                  pl.BlockSpec(memory_space=pltpu.SMEM)],
        out_specs=pl.BlockSpec((T, D_MODEL), lambda i: (i, 0)),
        out_shape=jax.ShapeDtypeStruct((n_lat, D_MODEL), BF16),
        compiler_params=_cparams(("parallel",)),
    )(qkv, qkv, qkv, qkv, qkv, band, sink)


def _rope_tables(n_lat, n_ctx):
    half = ATT_HEAD_DIM // 2
    inv = ROPE_THETA ** (-jnp.arange(0, half, 2, dtype=F32) / half)
    grid_h = n_lat // GRID_W
    ang_r = jnp.arange(grid_h, dtype=F32)[:, None] * inv[None]
    ang_c = jnp.arange(GRID_W, dtype=F32)[:, None] * inv[None]
    first = (jnp.arange(half) < half // 2)
    signed = lambda sn: (jnp.where(first, -sn, 0.0), jnp.where(first, 0.0, sn))
    reps = LANES // ATT_HEAD_DIM

    def lanes(row_part, col_part):
        return jnp.tile(jnp.concatenate([row_part, col_part], axis=-1), (1, reps))

    def parts(ang, is_row):
        two = jnp.concatenate([ang, ang], axis=-1)
        zero = jnp.zeros_like(two)
        out = []
        for t in (jnp.cos(two),) + signed(jnp.sin(two)):
            out.append(lanes(t, zero) if is_row else lanes(zero, t))
        return out

    ident = [jnp.ones((n_ctx // GRID_W, LANES), F32)] + [jnp.zeros((n_ctx // GRID_W, LANES), F32)] * 2
    ta = jnp.stack([jnp.concatenate([a, i_], axis=0) for a, i_ in zip(parts(ang_r, True), ident)])
    tb = jnp.stack(parts(ang_c, False))
    return ta, tb


def kernel(x, c, ctx, c_ctx, ada_w, ada_b, norm_w, ffn_w_in, ffn_w_out, final_norm_w, hg_w_in, hg_lb, hg_norm_w, hg_w_out, ssm_w_in, ssm_conv_w, ssm_conv_b, ssm_dt_bias, ssm_a_log, ssm_d, ssm_norm_w, ssm_w_out, pool_w, pool_scale, att_w_qkv, att_sink, att_w_out):
    assert x.shape[0] == 1 and DEPTH == 4
    n_lat, n_ctx = x.shape[1], ctx.shape[1]
    assert n_lat % TOKEN_BLOCK == 0 and n_ctx == TOKEN_BLOCK
    m = n_lat + n_ctx
    mods = _ada_table(c, c_ctx, ada_w, ada_b)
    lb_tab = jnp.cumsum(jax.nn.softmax(hg_lb.astype(F32), axis=1), axis=1)
    row = lambda v: v.reshape(1, -1)
    ffn_w = (ffn_w_in.astype(BF16), ffn_w_out.astype(BF16))
    ffn = lambda l: ffn_w
    sds = lambda w, dt: jax.ShapeDtypeStruct((m, w), dt)

    w = hg_w_in[0].astype(BF16)
    qs, cumf, kf, cumb, kb, v, gs = _proj_call(
        _proj_hgrn_kernel, x[0], row(norm_w[0, 0]), mods, 0, [w, lb_tab[:, 0]], [_resident(w.shape), _resident((2, D_MODEL))],
        [sds(D_MODEL, BF16), sds(D_MODEL, F32), sds(D_MODEL, BF16), sds(D_MODEL, F32), sds(D_MODEL, BF16),
         sds(D_MODEL, BF16), sds(D_MODEL, BF16)], n_lat, ctx=ctx[0])
    o_f = _hgrn_scan(qs, cumf, kf, v, False)
    a = _hgrn_scan(qs, cumb, kb, v, True, gs=gs, o_f=o_f, gn_w=row(hg_norm_w[0]))
    xs = _mix_ffn(x[0], a, hg_w_out[0].astype(BF16), mods, 0, row(norm_w[0, 1]), *ffn(0), n_lat, m, ctx=ctx[0])

    w = ssm_w_in[0].astype(BF16)
    zs, xc, dt_raw = _proj_call(
        _proj_ssd_kernel, xs, row(norm_w[1, 0]), mods, 1, [w, ssm_conv_w[0], row(ssm_conv_b[0])],
        [_resident(w.shape), _resident((SSM_CONV, SSM_XBC)), _resident((1, SSM_XBC))],
        [sds(SSM_D_INNER, BF16), sds(SSM_XBC, BF16), sds(2 * LANES, F32)], n_lat, halo=True, n_rows=m,
        scratch=lambda tm: [pltpu.VMEM((tm + 2 * SUBLANES, SSM_XBC), F32)])
    lane_pad = lambda v: jnp.concatenate([v, jnp.zeros((LANES - SSM_HEADS,), F32)]).reshape(1, LANES)
    y_f = _ssd_scan(False, xc, dt_raw, lane_pad(ssm_dt_bias[0, 0]), lane_pad(ssm_a_log[0, 0]))
    a = _ssd_scan(True, xc, dt_raw, lane_pad(ssm_dt_bias[0, 1]), lane_pad(ssm_a_log[0, 1]), zs=zs, y_f=y_f,
                  d_skip=row(jnp.repeat(ssm_d[0], SSM_HEADDIM)), norm_w=row(ssm_norm_w[0]))
    xs = _mix_ffn(xs, a, ssm_w_out[0].astype(BF16), mods, 1, row(norm_w[1, 1]), *ffn(1), n_lat, m)

    a = _pool_delta(xs, row(norm_w[2, 0]), mods, 2, n_lat, n_ctx)
    xs = _mix_ffn(xs, a, pool_w[0].astype(BF16), mods, 2, row(norm_w[2, 1]), *ffn(2), n_lat, m,
                  pool_scale=row(pool_scale[0]))

    ta, tb = _rope_tables(n_lat, n_ctx)
    qd = ATT_Q_HEADS * ATT_HEAD_DIM
    kd = ATT_KV_HEADS * ATT_HEAD_DIM
    wq = jnp.concatenate([att_w_qkv[0][:, :qd] * ATT_HEAD_DIM ** -0.5, att_w_qkv[0][:, qd:]], axis=1).astype(BF16)
    (qkv,) = _proj_call(
        _proj_attn_kernel, xs, row(norm_w[3, 0]), mods, 3, [wq, ta, tb], [_resident(wq.shape), _resident(ta.shape), _resident(tb.shape)],
        [sds(qd + 2 * kd, BF16)], n_lat, q_cols=qd, rope_cols=qd + kd)
    a = _attention(qkv, att_sink[0], n_lat, n_ctx)
    out = _mix_ffn(xs, a, att_w_out[0].astype(BF16), mods, 3, row(norm_w[3, 1]), *ffn(3), n_lat, n_lat,
                   final_w=row(final_norm_w))
    return out[None]
```

```python
import functools

import jax
import jax.numpy as jnp
from jax import lax
from jax.experimental import pallas as pl
from jax.experimental.pallas import tpu as pltpu

F32 = jnp.float32
BF16 = jnp.bfloat16

D_MODEL = 1024
DEPTH = 4
GRID_W = 64
EPS = 1e-6
NEG_INF = -1e30
LOG2E = 1.4426950408889634

HG_HEADS = 8
HG_DK = 128
HG_CHUNK = 64

SSM_D_INNER = 2 * D_MODEL
SSM_HEADDIM = 64
SSM_HEADS = SSM_D_INNER // SSM_HEADDIM
SSM_GROUPS = 4
SSM_STATE = 128
SSM_CONV = 5
SSM_BC = 2 * SSM_GROUPS * SSM_STATE
SSM_XBC = SSM_D_INNER + SSM_BC
SSM_HPG = SSM_HEADS // SSM_GROUPS

POOL_WINDOWS = (2, 4, 8, 16)
POOL_GROUP = D_MODEL // len(POOL_WINDOWS)

ATT_HEAD_DIM = 64
ATT_Q_HEADS = D_MODEL // ATT_HEAD_DIM
ATT_KV_HEADS = 4
ATT_GRP = ATT_Q_HEADS // ATT_KV_HEADS
ATT_WINDOW = 128
ROPE_THETA = 10000.0

FFN_HIDDEN = -(-8 * D_MODEL // (3 * 256)) * 256

LANES = 128
SUBLANES = 8
BF16_ROWS = 16
MXU_COLS = 256
TOKEN_BLOCK = 256
ROW_TILE = 640
VMEM_LIMIT = 56 * 1024 * 1024

SH1, SC1, G1, SH2, SC2, G2 = range(6)


def _cparams(sem):
    return pltpu.CompilerParams(dimension_semantics=sem, vmem_limit_bytes=VMEM_LIMIT)


def _row_tile(m, target):
    best = None
    for t in range(HG_CHUNK, target + 1, HG_CHUNK):
        if m % t == 0:
            best = t
    assert best is not None
    return best


def _resident(shape):
    zeros = (0,) * len(shape)
    return pl.BlockSpec(shape, lambda *_: zeros, pipeline_mode=pl.Buffered(1))


def _dot(a, b):
    return jnp.dot(a, b, preferred_element_type=F32)


def _dot_nt(a, b):
    return lax.dot_general(a, b, (((1,), (1,)), ((), ())), preferred_element_type=F32)


def _dot_tn(a, b):
    return lax.dot_general(a, b, (((0,), (0,)), ((), ())), preferred_element_type=F32)


def _split2(x):
    hi = x.astype(BF16)
    lo = (x - hi.astype(F32)).astype(BF16)
    return hi, lo


def _split3(x):
    hi = x.astype(BF16)
    r = x - hi.astype(F32)
    mid = r.astype(BF16)
    lo = (r - mid.astype(F32)).astype(BF16)
    return hi, mid, lo


def _silu(x):
    return x * jax.nn.sigmoid(x)


def _mod_vec(mod_ref, slot, is_ctx):
    lo, hi = slot * D_MODEL, (slot + 1) * D_MODEL
    return jnp.where(is_ctx, mod_ref[1:2, lo:hi], mod_ref[0:1, lo:hi])


def _rms(x):
    return x * lax.rsqrt(jnp.mean(x * x, axis=-1, keepdims=True) + EPS)


def _modnorm(x, nw, mod_ref, sh_slot, sc_slot, is_ctx):
    gain = nw * (1.0 + _mod_vec(mod_ref, sc_slot, is_ctx))
    return _rms(x) * gain + _mod_vec(mod_ref, sh_slot, is_ctx)


def _per_segment(tile, tm, n_lat, fn):
    bt, off = divmod(n_lat, tm)
    if off == 0:
        return fn(slice(0, tm), tile >= bt)
    return jnp.concatenate([fn(slice(0, off), tile > bt), fn(slice(off, tm), tile >= bt)], axis=0)


def _mod_spec(layer):
    return pl.BlockSpec((None, SUBLANES, 6 * D_MODEL), lambda *_: (layer, 0, 0))


def _ada_kernel(c_ref, w_ref, b_ref, o_ref):
    a = _silu(c_ref[...])
    a_hi, a_lo = _split2(a)
    w_hi, w_lo = _split2(w_ref[...])
    o_ref[...] = _dot(a_hi, w_hi) + _dot(a_hi, w_lo) + _dot(a_lo, w_hi) + b_ref[...]


def _ada_table(c, c_ctx, ada_w, ada_b):
    cc = jnp.concatenate([c[0:1], c_ctx[None], jnp.zeros((SUBLANES - 2, D_MODEL), F32)], axis=0)
    n = 6 * D_MODEL
    tn = n // 4
    return pl.pallas_call(
        _ada_kernel,
        grid=(DEPTH, n // tn),
        in_specs=[pl.BlockSpec((SUBLANES, D_MODEL), lambda l, j: (0, 0)),
                  pl.BlockSpec((None, D_MODEL, tn), lambda l, j: (l, 0, j)),
                  pl.BlockSpec((None, 1, tn), lambda l, j: (l, 0, j))],
        out_specs=pl.BlockSpec((None, SUBLANES, tn), lambda l, j: (l, 0, j)),
        out_shape=jax.ShapeDtypeStruct((DEPTH, SUBLANES, n), F32),
        compiler_params=_cparams(("parallel", "parallel")),
    )(cc, ada_w, ada_b.reshape(DEPTH, 1, n))


def _cumsum_rows(x, rev):
    n, width = x.shape
    nt = n // SUBLANES
    t = x.reshape(nt, SUBLANES, width)
    row = lax.broadcasted_iota(jnp.int32, t.shape, 1)
    k = 1
    while k < SUBLANES:
        if rev:
            t = t + jnp.where(row < SUBLANES - k, pltpu.roll(t, SUBLANES - k, axis=1), 0.0)
        else:
            t = t + jnp.where(row >= k, pltpu.roll(t, k, axis=1), 0.0)
        k *= 2
    edge = 0 if rev else SUBLANES - 1
    order = range(nt - 1, -1, -1) if rev else range(nt)
    out, run = [None] * nt, None
    for j in order:
        out[j] = t[j] if run is None else t[j] + run
        run = out[j][edge:edge + 1]
    return jnp.concatenate(out, axis=0)


def _shift_rows(x, s):
    if s == 0:
        return x
    n, width = x.shape
    t = x.reshape(n // SUBLANES, SUBLANES, width)
    row = lax.broadcasted_iota(jnp.int32, t.shape, 1)
    rot = pltpu.roll(t, (-s) % SUBLANES, axis=1)
    if s > 0:
        y = jnp.where(row < SUBLANES - s, rot, jnp.concatenate([rot[1:], rot[:1]], axis=0))
    else:
        y = jnp.where(row >= -s, rot, jnp.concatenate([rot[-1:], rot[:-1]], axis=0))
    return y.reshape(n, width)


def _stream_rows(x_ref, ctx_ref, tile, n_lat):
    tm = x_ref.shape[0]
    off = tm - ctx_ref.shape[0]
    tail = jnp.where(tile >= n_lat // tm, ctx_ref[...], x_ref[off:tm, :])
    return jnp.concatenate([x_ref[0:off, :], tail], axis=0)


def _normed_rows(x_ref, nw_ref, mod_ref, tm, n_lat, ctx_ref=None):
    tile = pl.program_id(0)
    x = x_ref[...] if ctx_ref is None else _stream_rows(x_ref, ctx_ref, tile, n_lat)
    nw = nw_ref[...]
    return _per_segment(tile, tm, n_lat, lambda rows, c: _modnorm(x[rows], nw, mod_ref, SH1, SC1, c)).astype(BF16)


def _proj_hgrn_kernel(x_ref, ctx_ref, nw_ref, mod_ref, w_ref, lb_ref, qs_ref, cumf_ref, kf_ref, cumb_ref, kb_ref, v_ref,
                      gs_ref, *, tm, n_lat):
    h = _normed_rows(x_ref, nw_ref, mod_ref, tm, n_lat, ctx_ref)
    zf, zb, q, v, g = (_dot(h, w_ref[:, k * D_MODEL:(k + 1) * D_MODEL]) for k in (1, 2, 0, 3, 4))
    for d, (z, cum_ref, k_ref) in enumerate(((zf, cumf_ref, kf_ref), (zb, cumb_ref, kb_ref))):
        lb = lb_ref[d:d + 1, :]
        f = lb + (1.0 - lb) * jax.nn.sigmoid(z)
        k_ref[...] = (1.0 - f).astype(k_ref.dtype)
        lf = jnp.log2(f)
        for c in range(tm // HG_CHUNK):
            rows = slice(c * HG_CHUNK, (c + 1) * HG_CHUNK)
            cum_ref[rows, :] = _cumsum_rows(lf[rows], rev=bool(d))
    qs_ref[...] = _silu(q).astype(qs_ref.dtype)
    v_ref[...] = v.astype(v_ref.dtype)
    gs_ref[...] = _silu(g).astype(gs_ref.dtype)


def _conv_silu(taps, cw_ref, cb_ref, cols):
    acc = cb_ref[:, cols] + cw_ref[0:1, cols] * taps[0]
    for j in range(1, SSM_CONV):
        acc = acc + cw_ref[j:j + 1, cols] * taps[j]
    return _silu(acc)


def _proj_ssd_kernel(x_ref, xp_ref, xn_ref, nw_ref, mod_ref, w_ref, cw_ref, cb_ref, zs_ref, xc_ref, dt_ref,
                     ext_ref, *, tm, n_lat, n_rows):
    i = pl.program_id(0)
    nt = n_rows // tm
    halo = SUBLANES
    half = SSM_CONV // 2
    nw = nw_ref[...]
    hf = _per_segment(i, tm, n_lat, lambda rows, c: _modnorm(x_ref[rows, :], nw, mod_ref, SH1, SC1, c))
    halo_norm = lambda ref, row0: _modnorm(ref[...], nw, mod_ref, SH1, SC1, row0 >= n_lat)
    h = hf.astype(BF16)
    for c in range(SSM_D_INNER // D_MODEL):
        sl = slice(c * D_MODEL, (c + 1) * D_MODEL)
        zs_ref[:, sl] = _silu(_dot(h, w_ref[:, sl])).astype(zs_ref.dtype)
    dt2 = _dot(h, w_ref[:, SSM_D_INNER + SSM_XBC:])
    lane_pad = jnp.zeros((tm, LANES - SSM_HEADS), F32)
    dt_ref[...] = jnp.concatenate([dt2[:, :SSM_HEADS], lane_pad, dt2[:, SSM_HEADS:], lane_pad], axis=1)
    h_ext = jnp.concatenate([halo_norm(xp_ref, i * tm - halo), hf, halo_norm(xn_ref, (i + 1) * tm)], axis=0).astype(BF16)
    vp = jnp.where(i > 0, 1.0, 0.0)
    vn = jnp.where(i < nt - 1, 1.0, 0.0)
    chunks = [slice(c * D_MODEL, (c + 1) * D_MODEL) for c in range(SSM_XBC // D_MODEL)]
    for cols in chunks:
        y = _dot(h_ext, w_ref[:, SSM_D_INNER + cols.start:SSM_D_INNER + cols.stop])
        ext_ref[0:halo, cols] = y[0:halo] * vp
        ext_ref[halo:halo + tm, cols] = y[halo:halo + tm]
        ext_ref[halo + tm:, cols] = y[halo + tm:] * vn
    for cols in chunks:
        ext = ext_ref[:, cols]
        taps = [_shift_rows(ext, j - half)[halo:halo + tm] for j in range(SSM_CONV)]
        xc_ref[:, cols] = _conv_silu(taps, cw_ref, cb_ref, cols).astype(xc_ref.dtype)
    if n_lat % tm:
        bt, off = divmod(n_lat, tm)
        fix = BF16_ROWS
        assert off % fix == 0 and fix <= off <= tm - fix

        @pl.when(i == bt)
        def _():
            e0 = halo + off
            is_lat_row = lax.broadcasted_iota(jnp.int32, (2 * fix + 2 * halo, 1), 0) < fix + halo
            for cols in chunks:
                win = ext_ref[e0 - fix - halo:e0 + fix + halo, cols]
                lat_w = jnp.where(is_lat_row, win, 0.0)
                ctx_w = jnp.where(is_lat_row, 0.0, win)
                lat_taps = [lat_w[halo - half + j:halo - half + j + fix] for j in range(SSM_CONV)]
                ctx_taps = [ctx_w[halo + fix - half + j:halo + fix - half + j + fix] for j in range(SSM_CONV)]
                xc_ref[off - fix:off, cols] = _conv_silu(lat_taps, cw_ref, cb_ref, cols).astype(xc_ref.dtype)
                xc_ref[off:off + fix, cols] = _conv_silu(ctx_taps, cw_ref, cb_ref, cols).astype(xc_ref.dtype)


def _proj_attn_kernel(x_ref, nw_ref, mod_ref, w_ref, ta_ref, tb_ref, o_ref, *, tm, n_lat, q_cols, rope_cols):
    h = _normed_rows(x_ref, nw_ref, mod_ref, tm, n_lat)
    n = w_ref.shape[1]
    gpt = tm // GRID_W
    g0 = pl.program_id(0) * gpt
    tabs = []
    for k in range(ta_ref.shape[0]):
        rows = [ta_ref[k, pl.ds(g0 + g, 1), :] + jnp.where(g0 + g < n_lat // GRID_W, tb_ref[k], 0.0) for g in range(gpt)]
        tabs.append(jnp.concatenate(rows, axis=0))
    cos, sa, sb = tabs
    for c in range(n // MXU_COLS):
        y = _dot(h, w_ref[:, c * MXU_COLS:(c + 1) * MXU_COLS])
        for s_ in range(MXU_COLS // LANES):
            c0 = c * MXU_COLS + s_ * LANES
            ys = y[:, s_ * LANES:(s_ + 1) * LANES]
            if c0 < rope_cols:
                ys = ys * cos + pltpu.roll(ys, LANES - 16, axis=1) * sa + pltpu.roll(ys, 16, axis=1) * sb
            if c0 < q_cols:
                ys = ys * LOG2E
            o_ref[:, c0:c0 + LANES] = ys.astype(o_ref.dtype)


def _proj_call(kernel_fn, x, nw, mods, layer, extra_in, extra_specs, out_shapes, n_lat, scratch=None, halo=False, ctx=None,
               **kw):
    m = out_shapes[0].shape[0]
    tm = _row_tile(m, ROW_TILE)
    row = lambda w: pl.BlockSpec((tm, w), lambda i: (i, 0))
    lead, lead_specs = [x], [row(D_MODEL)]
    if ctx is not None:
        assert n_lat % tm + ctx.shape[0] == tm
        lead.append(ctx)
        lead_specs.append(_resident(ctx.shape))
    if halo:
        tpb = tm // SUBLANES
        lead += [x, x]
        lead_specs += [pl.BlockSpec((SUBLANES, D_MODEL), lambda i: (jnp.maximum(i * tpb - 1, 0), 0)),
                       pl.BlockSpec((SUBLANES, D_MODEL), lambda i: (jnp.minimum((i + 1) * tpb, m // SUBLANES - 1), 0))]
    return pl.pallas_call(
        functools.partial(kernel_fn, tm=tm, n_lat=n_lat, **kw),
        grid=(m // tm,),
        in_specs=lead_specs + [_resident((1, D_MODEL)), _mod_spec(layer)] + [s(tm) if callable(s) else s for s in extra_specs],
        out_specs=[row(s.shape[1]) for s in out_shapes],
        out_shape=out_shapes,
        scratch_shapes=scratch(tm) if scratch else [],
        compiler_params=_cparams(("parallel",)),
    )(*lead, nw, mods, *extra_in)


def _ffn_chunks():
    out, c0 = [], 0
    while c0 < FFN_HIDDEN:
        cw = min(D_MODEL, FFN_HIDDEN - c0)
        assert cw % MXU_COLS == 0
        out.append((c0, cw))
        c0 += cw
    return out


def _mix_ffn_kernel(x_ref, a_ref, wo_ref, *rest, tm, n_lat, pool, final, split):
    rest = list(rest)
    ctx_ref = rest.pop(0) if split else None
    ps_ref = rest.pop(0) if pool else None
    mod_ref, nw_ref, win_ref, wout_ref = rest[:4]
    rest = rest[4:]
    fw_ref = rest.pop(0) if final else None
    (o_ref,) = rest
    tile = pl.program_id(0)
    seg = functools.partial(_per_segment, tile, tm, n_lat)
    if pool:
        parts = [_dot(a_ref[:, g * POOL_GROUP:(g + 1) * POOL_GROUP], wo_ref[g]) for g in range(len(POOL_WINDOWS))]
        y = jnp.concatenate(parts, axis=-1) * ps_ref[...]
    else:
        y = _dot(a_ref[...], wo_ref[...])
    x0 = _stream_rows(x_ref, ctx_ref, tile, n_lat) if split else x_ref[...]
    nw = nw_ref[...]
    x1 = seg(lambda rows, c: x0[rows] + _mod_vec(mod_ref, G1, c) * y[rows])
    h = seg(lambda rows, c: _modnorm(x1[rows], nw, mod_ref, SH2, SC2, c)).astype(BF16)
    acc = None
    for c0, cw in _ffn_chunks():
        gate = _dot(h, win_ref[:, c0:c0 + cw])
        up = _dot(h, win_ref[:, FFN_HIDDEN + c0:FFN_HIDDEN + c0 + cw])
        part = _dot((_silu(gate) * up).astype(BF16), wout_ref[c0:c0 + cw, :])
        acc = part if acc is None else acc + part
    x2 = seg(lambda rows, c: x1[rows] + _mod_vec(mod_ref, G2, c) * acc[rows])
    o_ref[...] = _rms(x2) * fw_ref[...] if final else x2


def _mix_ffn(x, a, wo, mods, layer, nw2, w_in, w_out, n_lat, rows, pool_scale=None, final_w=None, ctx=None):
    layer_slab = lambda w: pl.BlockSpec((None,) + w.shape[1:], lambda i: (layer, 0, 0), pipeline_mode=pl.Buffered(1))
    pool = pool_scale is not None
    final = final_w is not None
    tm = _row_tile(rows, ROW_TILE)
    row = lambda w: pl.BlockSpec((tm, w), lambda i: (i, 0))
    in_specs = [row(D_MODEL), row(a.shape[1]), _resident(wo.shape)]
    args = [x, a, wo]
    if ctx is not None:
        assert n_lat % tm + ctx.shape[0] == tm
        in_specs.append(_resident(ctx.shape))
        args.append(ctx)
    if pool:
        in_specs.append(_resident((1, D_MODEL)))
        args.append(pool_scale)
    in_specs += [_mod_spec(layer), _resident((1, D_MODEL)), layer_slab(w_in), layer_slab(w_out)]
    args += [mods, nw2, w_in, w_out]
    if final:
        in_specs.append(_resident((1, D_MODEL)))
        args.append(final_w)
    return pl.pallas_call(
        functools.partial(_mix_ffn_kernel, tm=tm, n_lat=n_lat, pool=pool, final=final, split=ctx is not None),
        grid=(rows // tm,),
        in_specs=in_specs,
        out_specs=row(D_MODEL),
        out_shape=jax.ShapeDtypeStruct((rows, D_MODEL), F32),
        compiler_params=_cparams(("parallel",)),
    )(*args)


def _scan_block(i, nb, rev):
    if rev:
        return jnp.where(i == 0, nb - 1, nb - 1 - i)
    return jnp.where(i == 0, nb - 1, i - 1)


def _hgrn_kernel(q_ref, cum_ref, k_ref, v_ref, *rest, rev):
    if rev:
        g_ref, of_ref, gn_ref, o_ref, st_ref = rest
    else:
        o_ref, st_ref = rest
    L = HG_CHUNK
    nch = TOKEN_BLOCK // L

    @pl.when(pl.program_id(0) == 0)
    def _():
        st_ref[...] = jnp.zeros_like(st_ref)

    r_i = lax.broadcasted_iota(jnp.int32, (L, L), 0)
    c_i = lax.broadcasted_iota(jnp.int32, (L, L), 1)
    keep = (r_i <= c_i) if rev else (r_i >= c_i)
    mid = L // 2 if rev else L // 2 - 1
    end = 0 if rev else L - 1

    heads = [slice(h * HG_DK, (h + 1) * HG_DK) for h in range(HG_HEADS)]
    order = [(c, slice(c * L, (c + 1) * L)) for c in (reversed(range(nch)) if rev else range(nch))]
    att_l, ds_l = {}, {}
    for c, rows in order:
        for h, sl in enumerate(heads):
            q = q_ref[rows, sl].astype(F32)
            k = k_ref[rows, sl].astype(F32)
            cum = cum_ref[rows, sl]
            ref = cum[mid:mid + 1]
            last = cum[end:end + 1]
            att_l[c, h] = _dot_nt((q * jnp.exp2(cum - ref)).astype(BF16), (k * jnp.exp2(ref - cum)).astype(BF16))
            ds_l[c, h] = _dot_tn(v_ref[rows, sl], (k * jnp.exp2(last - cum)).astype(BF16))
    for c, rows in order:
        for h, sl in enumerate(heads):
            cum = cum_ref[rows, sl]
            st = st_ref[h]
            inter = _dot_nt((q_ref[rows, sl].astype(F32) * jnp.exp2(cum)).astype(BF16), st.astype(BF16))
            att = jnp.where(keep, att_l[c, h], 0.0).astype(BF16)
            o = inter + _dot(att, v_ref[rows, sl])
            st_ref[h] = st * jnp.exp2(cum[end:end + 1]) + ds_l[c, h]
            if rev:
                o = o + of_ref[rows, sl].astype(F32)
                y = _rms(o) * gn_ref[...]
                o_ref[rows, sl] = (y * g_ref[rows, sl].astype(F32)).astype(o_ref.dtype)
            else:
                o_ref[rows, sl] = o.astype(o_ref.dtype)


def _hgrn_scan(qs, cum, k, v, rev, gs=None, o_f=None, gn_w=None):
    m = qs.shape[0]
    nb = m // TOKEN_BLOCK
    blk = functools.partial(_scan_block, nb=nb, rev=rev)
    spec = pl.BlockSpec((TOKEN_BLOCK, D_MODEL), lambda i: (blk(i), 0))
    in_specs = [spec] * 4
    args = [qs, cum, k, v]
    if rev:
        in_specs += [spec, spec, _resident((1, HG_DK))]
        args += [gs, o_f, gn_w]
    return pl.pallas_call(
        functools.partial(_hgrn_kernel, rev=rev),
        grid=(nb,),
        in_specs=in_specs,
        out_specs=spec,
        out_shape=jax.ShapeDtypeStruct((m, D_MODEL), BF16),
        scratch_shapes=[pltpu.VMEM((HG_HEADS, HG_DK, HG_DK), F32)],
        compiler_params=_cparams(("arbitrary",)),
    )(*args)


def _ssd_kernel(xc_ref, dt_ref, dtb_ref, alog_ref, *rest, rev):
    if rev:
        zs_ref, yf_ref, dsk_ref, nw_ref, o_ref, st_ref, stage_ref = rest
    else:
        y_ref, st_ref = rest
    T = TOKEN_BLOCK

    @pl.when(pl.program_id(0) == 0)
    def _():
        st_ref[...] = jnp.zeros_like(st_ref)

    gw = SSM_HPG * SSM_HEADDIM
    groups = [slice(g * gw, (g + 1) * gw) for g in range(SSM_GROUPS)]
    b_l, cb_l, yoff_l = [], [], []
    for g in range(SSM_GROUPS):
        b_l.append(xc_ref[:, SSM_D_INNER + g * SSM_STATE:SSM_D_INNER + (g + 1) * SSM_STATE])
        c_off = SSM_D_INNER + SSM_GROUPS * SSM_STATE + g * SSM_STATE
        c_g = xc_ref[:, c_off:c_off + SSM_STATE]
        cb_l.append(_dot_nt(c_g, b_l[g]))
        yoff_l.append(_dot(c_g, st_ref[g].astype(BF16)))

    dt = jax.nn.softplus(dt_ref[...] + dtb_ref[...])
    dta = dt * (-jnp.exp(alog_ref[...]))
    r_i = lax.broadcasted_iota(jnp.int32, (T, T), 0)
    c_i = lax.broadcasted_iota(jnp.int32, (T, T), 1)
    keep = (r_i <= c_i) if rev else (r_i >= c_i)
    tri = jnp.where(keep, 1.0, 0.0).astype(BF16)
    cum = sum(_dot(tri, p) for p in _split3(dta))
    end = 0 if rev else T - 1
    total = cum[end:end + 1]
    col2 = cum * LOG2E
    row2_t = (col2 - jnp.log2(dt)).T

    half = LANES // 2
    e_h = lax.broadcasted_iota(jnp.int32, (LANES, SSM_D_INNER), 0)
    e_c = lax.broadcasted_iota(jnp.int32, (LANES, SSM_D_INNER), 1)
    expand_m = jnp.where((e_c // SSM_HEADDIM == e_h % half) & (e_h % half < SSM_HEADS), 1.0, 0.0).astype(BF16)

    def expand(f):
        hi = f.astype(BF16).astype(F32)
        lane_f = lax.broadcasted_iota(jnp.int32, f.shape, 1)
        return _dot(jnp.where(lane_f < half, hi, pltpu.roll(f - hi, half, axis=1)).astype(BF16), expand_m)

    ecum_x = expand(jnp.exp(cum))
    fac_x = expand(dt * jnp.exp(total - cum))
    dec_x = expand(jnp.broadcast_to(jnp.exp(total), (SUBLANES, LANES)))[0:1]

    for g, gl in enumerate(groups):
        xs_fac = (xc_ref[:, gl].astype(F32) * fac_x[:, gl]).astype(BF16)
        st_ref[g] = st_ref[g] * dec_x[:, gl] + _dot_tn(b_l[g], xs_fac)
    lane = lax.broadcasted_iota(jnp.int32, (T, LANES), 1)
    first = lane < SSM_HEADDIM
    hb = T // 2
    keep_h = keep[0:hb, 0:hb]
    lo_, hi_ = slice(0, hb), slice(hb, T)
    for g, gl in enumerate(groups):
        cb = cb_l[g]
        for pr in range(SSM_HPG // 2):
            h0 = g * SSM_HPG + 2 * pr
            ol = slice(g * gw + pr * LANES, g * gw + (pr + 1) * LANES)
            xs_p = xc_ref[:, ol]
            ys = []
            for hh in (h0, h0 + 1):
                col = jnp.broadcast_to(col2[:, hh:hh + 1], (T, hb))
                row = row2_t[hh:hh + 1, :]

                def tile(rs, cs, masked):
                    e = jnp.exp2(col[rs] - row[:, cs])
                    if masked:
                        e = jnp.where(keep_h, e, 0.0)
                    return (cb[rs, cs] * e).astype(BF16)

                if rev:
                    top = _dot(jnp.concatenate([tile(lo_, lo_, True), tile(lo_, hi_, False)], axis=1), xs_p)
                    bot = _dot(tile(hi_, hi_, True), xs_p[hb:])
                else:
                    top = _dot(tile(lo_, lo_, True), xs_p[:hb])
                    bot = _dot(jnp.concatenate([tile(hi_, lo_, False), tile(hi_, hi_, True)], axis=1), xs_p)
                ys.append(jnp.concatenate([top, bot], axis=0))
            yp = yoff_l[g][:, pr * LANES:(pr + 1) * LANES] * ecum_x[:, ol] + jnp.where(first, ys[0], ys[1])
            if rev:
                yt = yp + yf_ref[:, ol].astype(F32) + dsk_ref[:, ol] * xc_ref[:, ol].astype(F32)
                stage_ref[:, ol] = yt * zs_ref[:, ol].astype(F32)
            else:
                y_ref[:, ol] = yp.astype(y_ref.dtype)
        if rev:
            o_ref[:, gl] = (_rms(stage_ref[:, gl]) * nw_ref[:, gl]).astype(o_ref.dtype)


def _ssd_scan(rev, xc, dt_raw, dt_bias, a_log, zs=None, y_f=None, d_skip=None, norm_w=None):
    m = dt_raw.shape[0]
    T = TOKEN_BLOCK
    nb = m // T
    blk = functools.partial(_scan_block, nb=nb, rev=rev)
    rows = lambda w: pl.BlockSpec((T, w), lambda i: (blk(i), 0))
    in_specs = [rows(SSM_XBC), pl.BlockSpec((T, LANES), lambda i: (blk(i), 1 if rev else 0)),
                _resident((1, LANES)), _resident((1, LANES))]
    args = [xc, dt_raw, dt_bias, a_log]
    scratch = [pltpu.VMEM((SSM_GROUPS, SSM_STATE, SSM_HPG * SSM_HEADDIM), F32)]
    if rev:
        in_specs += [rows(SSM_D_INNER), rows(SSM_D_INNER), _resident((1, SSM_D_INNER)), _resident((1, SSM_D_INNER))]
        args += [zs, y_f, d_skip, norm_w]
        scratch.append(pltpu.VMEM((T, SSM_D_INNER), F32))
    return pl.pallas_call(
        functools.partial(_ssd_kernel, rev=rev),
        grid=(nb,),
        in_specs=in_specs,
        out_specs=rows(SSM_D_INNER),
        out_shape=jax.ShapeDtypeStruct((m, SSM_D_INNER), BF16),
        scratch_shapes=scratch,
        compiler_params=_cparams(("arbitrary",)),
    )(*args)


def _pool_kernel(x_ref, xp_ref, xn_ref, nw_ref, mod_ref, o_ref, ext_ref, *, nb, n_lat, n_ctx):
    T = TOKEN_BLOCK
    b = pl.program_id(0)
    is_ctx = b == nb - 1
    vp = jnp.where((b > 0) & (b < nb - 1), 1.0, 0.0)
    vn = jnp.where(b < nb - 2, 1.0, 0.0)
    norm = lambda x: _modnorm(x, nw_ref[...], mod_ref, SH1, SC1, is_ctx)
    ext_ref[0:SUBLANES] = norm(xp_ref[...]) * vp
    ext_ref[SUBLANES:SUBLANES + T] = norm(x_ref[...])
    ext_ref[SUBLANES + T:] = norm(xn_ref[...]) * vn
    t = jnp.where(is_ctx, 0, b * T) + lax.broadcasted_iota(jnp.int32, (T, 1), 0)
    seq = jnp.where(is_ctx, n_ctx, n_lat)
    for gi, win in enumerate(POOL_WINDOWS):
        sl = slice(gi * POOL_GROUP, (gi + 1) * POOL_GROUP)
        run = ext_ref[:, sl]
        k = 1
        while k < win:
            run = run + (_shift_rows(run, k) if k < SUBLANES else jnp.concatenate([run[k:], run[:k]], axis=0))
            k *= 2
        half_w = win // 2
        acc = run[0:T] if half_w == SUBLANES else _shift_rows(run, -half_w)[SUBLANES:SUBLANES + T]
        cnt = jnp.minimum(t + (win - win // 2), seq) - jnp.maximum(t - win // 2, 0)
        o_ref[:, sl] = (acc / cnt.astype(F32) - ext_ref[SUBLANES:SUBLANES + T, sl]).astype(o_ref.dtype)


def _pool_delta(x, nw, mods, layer, n_lat, n_ctx):
    m = x.shape[0]
    T = TOKEN_BLOCK
    nb = m // T
    row8 = T // SUBLANES
    return pl.pallas_call(
        functools.partial(_pool_kernel, nb=nb, n_lat=n_lat, n_ctx=n_ctx),
        grid=(nb,),
        in_specs=[pl.BlockSpec((T, D_MODEL), lambda i: (i, 0)),
                  pl.BlockSpec((SUBLANES, D_MODEL), lambda i: (jnp.maximum(i * row8 - 1, 0), 0)),
                  pl.BlockSpec((SUBLANES, D_MODEL), lambda i: (jnp.minimum((i + 1) * row8, m // SUBLANES - 1), 0)),
                  _resident((1, D_MODEL)), _mod_spec(layer)],
        out_specs=pl.BlockSpec((T, D_MODEL), lambda i: (i, 0)),
        out_shape=jax.ShapeDtypeStruct((m, D_MODEL), BF16),
        scratch_shapes=[pltpu.VMEM((T + 2 * SUBLANES, D_MODEL), F32)],
        compiler_params=_cparams(("parallel",)),
    )(x, x, x, nw, mods)


def _attn_kernel(q_ref, kc_ref, kp_ref, kn_ref, kx_ref, band_ref, sink_ref, o_ref, *, n_lat):
    T = TOKEN_BLOCK
    W = ATT_WINDOW
    i = pl.program_id(0)
    kv = jnp.concatenate([kp_ref[...], kc_ref[...], kn_ref[...], kx_ref[...]], axis=0)
    span = T + 2 * W
    rows = ATT_GRP * T
    kpos = i * T - W + lax.broadcasted_iota(jnp.int32, (1, span), 1)
    bias = band_ref[...] + jnp.where((kpos >= 0) & (kpos < n_lat), 0.0, NEG_INF)
    bias = jnp.concatenate([bias] * ATT_GRP, axis=0)
    head_row = lax.broadcasted_iota(jnp.int32, (rows, 1), 0) // T
    kd = ATT_KV_HEADS * ATT_HEAD_DIM
    def scores(kh):
        k = kv[:, kh * ATT_HEAD_DIM:(kh + 1) * ATT_HEAD_DIM]
        q = jnp.concatenate([q_ref[:, (kh * ATT_GRP + g) * ATT_HEAD_DIM:(kh * ATT_GRP + g + 1) * ATT_HEAD_DIM]
                             for g in range(ATT_GRP)], axis=0)
        return _dot_nt(q, k)

    def finish(kh, raw):
        v = kv[:, kd + kh * ATT_HEAD_DIM:kd + (kh + 1) * ATT_HEAD_DIM]
        s = jnp.concatenate([raw[:, :span] + bias, raw[:, span:]], axis=1)
        sink = jnp.zeros((rows, 1), F32)
        for g in range(ATT_GRP):
            sink = jnp.where(head_row == g, sink_ref[kh * ATT_GRP + g] * LOG2E, sink)
        mx = jnp.maximum(jnp.max(s, axis=-1, keepdims=True), sink)
        p = jnp.exp2(s - mx)
        den = jnp.sum(p, axis=-1, keepdims=True) + jnp.exp2(sink - mx)
        o = _dot(p.astype(BF16), v) / den
        for g in range(0, ATT_GRP, 2):
            pair = jnp.concatenate([o[g * T:(g + 1) * T], o[(g + 1) * T:(g + 2) * T]], axis=-1)
            c0 = (kh * ATT_GRP + g) * ATT_HEAD_DIM
            o_ref[:, c0:c0 + 2 * ATT_HEAD_DIM] = pair.astype(o_ref.dtype)

    raw = scores(0)
    for kh in range(ATT_KV_HEADS):
        nxt = scores(kh + 1) if kh + 1 < ATT_KV_HEADS else None
        finish(kh, raw)
        raw = nxt


def _attention(qkv, sink, n_lat, n_ctx):
    T = TOKEN_BLOCK
    W = ATT_WINDOW
    nq = n_lat // T
    kvw = 2 * ATT_KV_HEADS * ATT_HEAD_DIM
    kv_col = D_MODEL // kvw
    wpb = T // W
    off = jnp.arange(T + 2 * W)[None, :] - jnp.arange(T)[:, None]
    band = jnp.where((off >= 0) & (off <= 2 * W), 0.0, NEG_INF).astype(F32)
    return pl.pallas_call(
        functools.partial(_attn_kernel, n_lat=n_lat),
        grid=(nq,),
        in_specs=[pl.BlockSpec((T, D_MODEL), lambda i: (i, 0)),
                  pl.BlockSpec((T, kvw), lambda i: (i, kv_col)),
                  pl.BlockSpec((W, kvw), lambda i: (jnp.maximum(i * wpb - 1, 0), kv_col)),
                  pl.BlockSpec((W, kvw), lambda i: (jnp.minimum((i + 1) * wpb, n_lat // W - 1), kv_col)),
                  pl.BlockSpec((n_ctx, kvw), lambda i: (n_lat // n_ctx, kv_col)),
                  _resident(band.shape),
                  pl.BlockSpec(memory_space=pltpu.SMEM)],
        out_specs=pl.BlockSpec((T, D_MODEL), lambda i: (i, 0)),
        out_shape=jax.ShapeDtypeStruct((n_lat, D_MODEL), BF16),
        compiler_params=_cparams(("parallel",)),
    )(qkv, qkv, qkv, qkv, qkv, band, sink)


def _rope_tables(n_lat, n_ctx):
    half = ATT_HEAD_DIM // 2
    inv = ROPE_THETA ** (-jnp.arange(0, half, 2, dtype=F32) / half)
    grid_h = n_lat // GRID_W
    ang_r = jnp.arange(grid_h, dtype=F32)[:, None] * inv[None]
    ang_c = jnp.arange(GRID_W, dtype=F32)[:, None] * inv[None]
    first = (jnp.arange(half) < half // 2)
    signed = lambda sn: (jnp.where(first, -sn, 0.0), jnp.where(first, 0.0, sn))
    reps = LANES // ATT_HEAD_DIM

    def lanes(row_part, col_part):
        return jnp.tile(jnp.concatenate([row_part, col_part], axis=-1), (1, reps))

    def parts(ang, is_row):
        two = jnp.concatenate([ang, ang], axis=-1)
        zero = jnp.zeros_like(two)
        out = []
        for t in (jnp.cos(two),) + signed(jnp.sin(two)):
            out.append(lanes(t, zero) if is_row else lanes(zero, t))
        return out

    ident = [jnp.ones((n_ctx // GRID_W, LANES), F32)] + [jnp.zeros((n_ctx // GRID_W, LANES), F32)] * 2
    ta = jnp.stack([jnp.concatenate([a, i_], axis=0) for a, i_ in zip(parts(ang_r, True), ident)])
    tb = jnp.stack(parts(ang_c, False))
    return ta, tb


def kernel(x, c, ctx, c_ctx, ada_w, ada_b, norm_w, ffn_w_in, ffn_w_out, final_norm_w, hg_w_in, hg_lb, hg_norm_w, hg_w_out, ssm_w_in, ssm_conv_w, ssm_conv_b, ssm_dt_bias, ssm_a_log, ssm_d, ssm_norm_w, ssm_w_out, pool_w, pool_scale, att_w_qkv, att_sink, att_w_out):
    assert x.shape[0] == 1 and DEPTH == 4
    n_lat, n_ctx = x.shape[1], ctx.shape[1]
    assert n_lat % TOKEN_BLOCK == 0 and n_ctx == TOKEN_BLOCK
    m = n_lat + n_ctx
    mods = _ada_table(c, c_ctx, ada_w, ada_b)
    lb_tab = jnp.cumsum(jax.nn.softmax(hg_lb.astype(F32), axis=1), axis=1)
    row = lambda v: v.reshape(1, -1)
    ffn_w = (ffn_w_in.astype(BF16), ffn_w_out.astype(BF16))
    ffn = lambda l: ffn_w
    sds = lambda w, dt: jax.ShapeDtypeStruct((m, w), dt)

    w = hg_w_in[0].astype(BF16)
    qs, cumf, kf, cumb, kb, v, gs = _proj_call(
        _proj_hgrn_kernel, x[0], row(norm_w[0, 0]), mods, 0, [w, lb_tab[:, 0]], [_resident(w.shape), _resident((2, D_MODEL))],
        [sds(D_MODEL, BF16), sds(D_MODEL, F32), sds(D_MODEL, BF16), sds(D_MODEL, F32), sds(D_MODEL, BF16),
         sds(D_MODEL, BF16), sds(D_MODEL, BF16)], n_lat, ctx=ctx[0])
    o_f = _hgrn_scan(qs, cumf, kf, v, False)
    a = _hgrn_scan(qs, cumb, kb, v, True, gs=gs, o_f=o_f, gn_w=row(hg_norm_w[0]))
    xs = _mix_ffn(x[0], a, hg_w_out[0].astype(BF16), mods, 0, row(norm_w[0, 1]), *ffn(0), n_lat, m, ctx=ctx[0])

    w = ssm_w_in[0].astype(BF16)
    zs, xc, dt_raw = _proj_call(
        _proj_ssd_kernel, xs, row(norm_w[1, 0]), mods, 1, [w, ssm_conv_w[0], row(ssm_conv_b[0])],
        [_resident(w.shape), _resident((SSM_CONV, SSM_XBC)), _resident((1, SSM_XBC))],
        [sds(SSM_D_INNER, BF16), sds(SSM_XBC, BF16), sds(2 * LANES, F32)], n_lat, halo=True, n_rows=m,
        scratch=lambda tm: [pltpu.VMEM((tm + 2 * SUBLANES, SSM_XBC), F32)])
    lane_pad = lambda v: jnp.concatenate([v, jnp.zeros((LANES - SSM_HEADS,), F32)]).reshape(1, LANES)
    y_f = _ssd_scan(False, xc, dt_raw, lane_pad(ssm_dt_bias[0, 0]), lane_pad(ssm_a_log[0, 0]))
    a = _ssd_scan(True, xc, dt_raw, lane_pad(ssm_dt_bias[0, 1]), lane_pad(ssm_a_log[0, 1]), zs=zs, y_f=y_f,
                  d_skip=row(jnp.repeat(ssm_d[0], SSM_HEADDIM)), norm_w=row(ssm_norm_w[0]))
    xs = _mix_ffn(xs, a, ssm_w_out[0].astype(BF16), mods, 1, row(norm_w[1, 1]), *ffn(1), n_lat, m)

    a = _pool_delta(xs, row(norm_w[2, 0]), mods, 2, n_lat, n_ctx)
    xs = _mix_ffn(xs, a, pool_w[0].astype(BF16), mods, 2, row(norm_w[2, 1]), *ffn(2), n_lat, m,
                  pool_scale=row(pool_scale[0]))

    ta, tb = _rope_tables(n_lat, n_ctx)
    qd = ATT_Q_HEADS * ATT_HEAD_DIM
    kd = ATT_KV_HEADS * ATT_HEAD_DIM
    wq = jnp.concatenate([att_w_qkv[0][:, :qd] * ATT_HEAD_DIM ** -0.5, att_w_qkv[0][:, qd:]], axis=1).astype(BF16)
    (qkv,) = _proj_call(
        _proj_attn_kernel, xs, row(norm_w[3, 0]), mods, 3, [wq, ta, tb], [_resident(wq.shape), _resident(ta.shape), _resident(tb.shape)],
        [sds(qd + 2 * kd, BF16)], n_lat, q_cols=qd, rope_cols=qd + kd)
    a = _attention(qkv, att_sink[0], n_lat, n_ctx)
    out = _mix_ffn(xs, a, att_w_out[0].astype(BF16), mods, 3, row(norm_w[3, 1]), *ffn(3), n_lat, n_lat,
                   final_w=row(final_norm_w))
    return out[None]
```

```python
import functools

import jax
import jax.numpy as jnp
from jax import lax
from jax.experimental import pallas as pl
from jax.experimental.pallas import tpu as pltpu

F32 = jnp.float32
BF16 = jnp.bfloat16

D_MODEL = 1024
DEPTH = 4
GRID_W = 64
EPS = 1e-6
NEG_INF = -1e30
LOG2E = 1.4426950408889634

HG_HEADS = 8
HG_DK = 128
HG_CHUNK = 64

SSM_D_INNER = 2 * D_MODEL
SSM_HEADDIM = 64
SSM_HEADS = SSM_D_INNER // SSM_HEADDIM
SSM_GROUPS = 4
SSM_STATE = 128
SSM_CONV = 5
SSM_BC = 2 * SSM_GROUPS * SSM_STATE
SSM_XBC = SSM_D_INNER + SSM_BC
SSM_HPG = SSM_HEADS // SSM_GROUPS

POOL_WINDOWS = (2, 4, 8, 16)
POOL_GROUP = D_MODEL // len(POOL_WINDOWS)

ATT_HEAD_DIM = 64
ATT_Q_HEADS = D_MODEL // ATT_HEAD_DIM
ATT_KV_HEADS = 4
ATT_GRP = ATT_Q_HEADS // ATT_KV_HEADS
ATT_WINDOW = 128
ROPE_THETA = 10000.0

FFN_HIDDEN = -(-8 * D_MODEL // (3 * 256)) * 256

LANES = 128
SUBLANES = 8
BF16_ROWS = 16
MXU_COLS = 256
TOKEN_BLOCK = 256
ROW_TILE = 640
VMEM_LIMIT = 56 * 1024 * 1024

SH1, SC1, G1, SH2, SC2, G2 = range(6)


def _cparams(sem):
    return pltpu.CompilerParams(dimension_semantics=sem, vmem_limit_bytes=VMEM_LIMIT)


def _row_tile(m, target):
    best = None
    for t in range(HG_CHUNK, target + 1, HG_CHUNK):
        if m % t == 0:
            best = t
    assert best is not None
    return best


def _resident(shape):
    zeros = (0,) * len(shape)
    return pl.BlockSpec(shape, lambda *_: zeros, pipeline_mode=pl.Buffered(1))


def _dot(a, b):
    return jnp.dot(a, b, preferred_element_type=F32)


def _dot_nt(a, b):
    return lax.dot_general(a, b, (((1,), (1,)), ((), ())), preferred_element_type=F32)


def _dot_tn(a, b):
    return lax.dot_general(a, b, (((0,), (0,)), ((), ())), preferred_element_type=F32)


def _split2(x):
    hi = x.astype(BF16)
    lo = (x - hi.astype(F32)).astype(BF16)
    return hi, lo


def _split3(x):
    hi = x.astype(BF16)
    r = x - hi.astype(F32)
    mid = r.astype(BF16)
    lo = (r - mid.astype(F32)).astype(BF16)
    return hi, mid, lo


def _silu(x):
    return x * jax.nn.sigmoid(x)


def _mod_vec(mod_ref, slot, is_ctx):
    lo, hi = slot * D_MODEL, (slot + 1) * D_MODEL
    return jnp.where(is_ctx, mod_ref[1:2, lo:hi], mod_ref[0:1, lo:hi])


def _rms(x):
    return x * lax.rsqrt(jnp.mean(x * x, axis=-1, keepdims=True) + EPS)


def _modnorm(x, nw, mod_ref, sh_slot, sc_slot, is_ctx):
    gain = nw * (1.0 + _mod_vec(mod_ref, sc_slot, is_ctx))
    return _rms(x) * gain + _mod_vec(mod_ref, sh_slot, is_ctx)


def _per_segment(tile, tm, n_lat, fn):
    bt, off = divmod(n_lat, tm)
    if off == 0:
        return fn(slice(0, tm), tile >= bt)
    return jnp.concatenate([fn(slice(0, off), tile > bt), fn(slice(off, tm), tile >= bt)], axis=0)


def _mod_spec(layer):
    return pl.BlockSpec((None, SUBLANES, 6 * D_MODEL), lambda *_: (layer, 0, 0))


def _ada_kernel(c_ref, w_ref, b_ref, o_ref):
    a = _silu(c_ref[...])
    a_hi, a_lo = _split2(a)
    w_hi, w_lo = _split2(w_ref[...])
    o_ref[...] = _dot(a_hi, w_hi) + _dot(a_hi, w_lo) + _dot(a_lo, w_hi) + b_ref[...]


def _ada_table(c, c_ctx, ada_w, ada_b):
    cc = jnp.concatenate([c[0:1], c_ctx[None], jnp.zeros((SUBLANES - 2, D_MODEL), F32)], axis=0)
    n = 6 * D_MODEL
    tn = n // 4
    return pl.pallas_call(
        _ada_kernel,
        grid=(DEPTH, n // tn),
        in_specs=[pl.BlockSpec((SUBLANES, D_MODEL), lambda l, j: (0, 0)),
                  pl.BlockSpec((None, D_MODEL, tn), lambda l, j: (l, 0, j)),
                  pl.BlockSpec((None, 1, tn), lambda l, j: (l, 0, j))],
        out_specs=pl.BlockSpec((None, SUBLANES, tn), lambda l, j: (l, 0, j)),
        out_shape=jax.ShapeDtypeStruct((DEPTH, SUBLANES, n), F32),
        compiler_params=_cparams(("parallel", "parallel")),
    )(cc, ada_w, ada_b.reshape(DEPTH, 1, n))


def _cumsum_rows(x, rev):
    n, width = x.shape
    nt = n // SUBLANES
    t = x.reshape(nt, SUBLANES, width)
    row = lax.broadcasted_iota(jnp.int32, t.shape, 1)
    k = 1
    while k < SUBLANES:
        if rev:
            t = t + jnp.where(row < SUBLANES - k, pltpu.roll(t, SUBLANES - k, axis=1), 0.0)
        else:
            t = t + jnp.where(row >= k, pltpu.roll(t, k, axis=1), 0.0)
        k *= 2
    edge = 0 if rev else SUBLANES - 1
    order = range(nt - 1, -1, -1) if rev else range(nt)
    out, run = [None] * nt, None
    for j in order:
        out[j] = t[j] if run is None else t[j] + run
        run = out[j][edge:edge + 1]
    return jnp.concatenate(out, axis=0)


def _shift_rows(x, s):
    if s == 0:
        return x
    n, width = x.shape
    t = x.reshape(n // SUBLANES, SUBLANES, width)
    row = lax.broadcasted_iota(jnp.int32, t.shape, 1)
    rot = pltpu.roll(t, (-s) % SUBLANES, axis=1)
    if s > 0:
        y = jnp.where(row < SUBLANES - s, rot, jnp.concatenate([rot[1:], rot[:1]], axis=0))
    else:
        y = jnp.where(row >= -s, rot, jnp.concatenate([rot[-1:], rot[:-1]], axis=0))
    return y.reshape(n, width)


def _stream_rows(x_ref, ctx_ref, tile, n_lat):
    tm = x_ref.shape[0]
    off = tm - ctx_ref.shape[0]
    tail = jnp.where(tile >= n_lat // tm, ctx_ref[...], x_ref[off:tm, :])
    return jnp.concatenate([x_ref[0:off, :], tail], axis=0)


def _normed_rows(x_ref, nw_ref, mod_ref, tm, n_lat, ctx_ref=None):
    tile = pl.program_id(0)
    x = x_ref[...] if ctx_ref is None else _stream_rows(x_ref, ctx_ref, tile, n_lat)
    nw = nw_ref[...]
    return _per_segment(tile, tm, n_lat, lambda rows, c: _modnorm(x[rows], nw, mod_ref, SH1, SC1, c)).astype(BF16)


def _proj_hgrn_kernel(x_ref, ctx_ref, nw_ref, mod_ref, w_ref, lb_ref, qs_ref, cumf_ref, kf_ref, cumb_ref, kb_ref, v_ref,
                      gs_ref, *, tm, n_lat):
    h = _normed_rows(x_ref, nw_ref, mod_ref, tm, n_lat, ctx_ref)
    zf, zb, q, v, g = (_dot(h, w_ref[:, k * D_MODEL:(k + 1) * D_MODEL]) for k in (1, 2, 0, 3, 4))
    for d, (z, cum_ref, k_ref) in enumerate(((zf, cumf_ref, kf_ref), (zb, cumb_ref, kb_ref))):
        lb = lb_ref[d:d + 1, :]
        f = lb + (1.0 - lb) * jax.nn.sigmoid(z)
        k_ref[...] = (1.0 - f).astype(k_ref.dtype)
        lf = jnp.log2(f)
        for c in range(tm // HG_CHUNK):
            rows = slice(c * HG_CHUNK, (c + 1) * HG_CHUNK)
            cum_ref[rows, :] = _cumsum_rows(lf[rows], rev=bool(d))
    qs_ref[...] = _silu(q).astype(qs_ref.dtype)
    v_ref[...] = v.astype(v_ref.dtype)
    gs_ref[...] = _silu(g).astype(gs_ref.dtype)


def _conv_silu(taps, cw_ref, cb_ref, cols):
    acc = cb_ref[:, cols] + cw_ref[0:1, cols] * taps[0]
    for j in range(1, SSM_CONV):
        acc = acc + cw_ref[j:j + 1, cols] * taps[j]
    return _silu(acc)


def _proj_ssd_kernel(x_ref, xp_ref, xn_ref, nw_ref, mod_ref, w_ref, cw_ref, cb_ref, zs_ref, xc_ref, dt_ref,
                     ext_ref, *, tm, n_lat, n_rows):
    i = pl.program_id(0)
    nt = n_rows // tm
    halo = SUBLANES
    half = SSM_CONV // 2
    nw = nw_ref[...]
    hf = _per_segment(i, tm, n_lat, lambda rows, c: _modnorm(x_ref[rows, :], nw, mod_ref, SH1, SC1, c))
    halo_norm = lambda ref, row0: _modnorm(ref[...], nw, mod_ref, SH1, SC1, row0 >= n_lat)
    h = hf.astype(BF16)
    for c in range(SSM_D_INNER // D_MODEL):
        sl = slice(c * D_MODEL, (c + 1) * D_MODEL)
        zs_ref[:, sl] = _silu(_dot(h, w_ref[:, sl])).astype(zs_ref.dtype)
    dt2 = _dot(h, w_ref[:, SSM_D_INNER + SSM_XBC:])
    lane_pad = jnp.zeros((tm, LANES - SSM_HEADS), F32)
    dt_ref[...] = jnp.concatenate([dt2[:, :SSM_HEADS], lane_pad, dt2[:, SSM_HEADS:], lane_pad], axis=1)
    h_ext = jnp.concatenate([halo_norm(xp_ref, i * tm - halo), hf, halo_norm(xn_ref, (i + 1) * tm)], axis=0).astype(BF16)
    vp = jnp.where(i > 0, 1.0, 0.0)
    vn = jnp.where(i < nt - 1, 1.0, 0.0)
    chunks = [slice(c * D_MODEL, (c + 1) * D_MODEL) for c in range(SSM_XBC // D_MODEL)]
    for cols in chunks:
        y = _dot(h_ext, w_ref[:, SSM_D_INNER + cols.start:SSM_D_INNER + cols.stop])
        ext_ref[0:halo, cols] = y[0:halo] * vp
        ext_ref[halo:halo + tm, cols] = y[halo:halo + tm]
        ext_ref[halo + tm:, cols] = y[halo + tm:] * vn
    for cols in chunks:
        ext = ext_ref[:, cols]
        taps = [_shift_rows(ext, j - half)[halo:halo + tm] for j in range(SSM_CONV)]
        xc_ref[:, cols] = _conv_silu(taps, cw_ref, cb_ref, cols).astype(xc_ref.dtype)
    if n_lat % tm:
        bt, off = divmod(n_lat, tm)
        fix = BF16_ROWS
        assert off % fix == 0 and fix <= off <= tm - fix

        @pl.when(i == bt)
        def _():
            e0 = halo + off
            is_lat_row = lax.broadcasted_iota(jnp.int32, (2 * fix + 2 * halo, 1), 0) < fix + halo
            for cols in chunks:
                win = ext_ref[e0 - fix - halo:e0 + fix + halo, cols]
                lat_w = jnp.where(is_lat_row, win, 0.0)
                ctx_w = jnp.where(is_lat_row, 0.0, win)
                lat_taps = [lat_w[halo - half + j:halo - half + j + fix] for j in range(SSM_CONV)]
                ctx_taps = [ctx_w[halo + fix - half + j:halo + fix - half + j + fix] for j in range(SSM_CONV)]
                xc_ref[off - fix:off, cols] = _conv_silu(lat_taps, cw_ref, cb_ref, cols).astype(xc_ref.dtype)
                xc_ref[off:off + fix, cols] = _conv_silu(ctx_taps, cw_ref, cb_ref, cols).astype(xc_ref.dtype)


def _proj_attn_kernel(x_ref, nw_ref, mod_ref, w_ref, ta_ref, tb_ref, o_ref, *, tm, n_lat, q_cols, rope_cols):
    h = _normed_rows(x_ref, nw_ref, mod_ref, tm, n_lat)
    n = w_ref.shape[1]
    gpt = tm // GRID_W
    g0 = pl.program_id(0) * gpt
    tabs = []
    for k in range(ta_ref.shape[0]):
        rows = [ta_ref[k, pl.ds(g0 + g, 1), :] + jnp.where(g0 + g < n_lat // GRID_W, tb_ref[k], 0.0) for g in range(gpt)]
        tabs.append(jnp.concatenate(rows, axis=0))
    cos, sa, sb = tabs
    for c in range(n // MXU_COLS):
        y = _dot(h, w_ref[:, c * MXU_COLS:(c + 1) * MXU_COLS])
        for s_ in range(MXU_COLS // LANES):
            c0 = c * MXU_COLS + s_ * LANES
            ys = y[:, s_ * LANES:(s_ + 1) * LANES]
            if c0 < rope_cols:
                ys = ys * cos + pltpu.roll(ys, LANES - 16, axis=1) * sa + pltpu.roll(ys, 16, axis=1) * sb
            if c0 < q_cols:
                ys = ys * LOG2E
            o_ref[:, c0:c0 + LANES] = ys.astype(o_ref.dtype)


def _proj_call(kernel_fn, x, nw, mods, layer, extra_in, extra_specs, out_shapes, n_lat, scratch=None, halo=False, ctx=None,
               **kw):
    m = out_shapes[0].shape[0]
    tm = _row_tile(m, ROW_TILE)
    row = lambda w: pl.BlockSpec((tm, w), lambda i: (i, 0))
    lead, lead_specs = [x], [row(D_MODEL)]
    if ctx is not None:
        assert n_lat % tm + ctx.shape[0] == tm
        lead.append(ctx)
        lead_specs.append(_resident(ctx.shape))
    if halo:
        tpb = tm // SUBLANES
        lead += [x, x]
        lead_specs += [pl.BlockSpec((SUBLANES, D_MODEL), lambda i: (jnp.maximum(i * tpb - 1, 0), 0)),
                       pl.BlockSpec((SUBLANES, D_MODEL), lambda i: (jnp.minimum((i + 1) * tpb, m // SUBLANES - 1), 0))]
    return pl.pallas_call(
        functools.partial(kernel_fn, tm=tm, n_lat=n_lat, **kw),
        grid=(m // tm,),
        in_specs=lead_specs + [_resident((1, D_MODEL)), _mod_spec(layer)] + [s(tm) if callable(s) else s for s in extra_specs],
        out_specs=[row(s.shape[1]) for s in out_shapes],
        out_shape=out_shapes,
        scratch_shapes=scratch(tm) if scratch else [],
        compiler_params=_cparams(("parallel",)),
    )(*lead, nw, mods, *extra_in)


def _ffn_chunks():
    out, c0 = [], 0
    while c0 < FFN_HIDDEN:
        cw = min(D_MODEL, FFN_HIDDEN - c0)
        assert cw % MXU_COLS == 0
        out.append((c0, cw))
        c0 += cw
    return out


def _mix_ffn_kernel(x_ref, a_ref, wo_ref, *rest, tm, n_lat, pool, final, split, readout):
    rest = list(rest)
    ctx_ref = rest.pop(0) if split else None
    ob_ref, gs_ref, gn_ref = (rest.pop(0), rest.pop(0), rest.pop(0)) if readout else (None, None, None)
    ps_ref = rest.pop(0) if pool else None
    mod_ref, nw_ref, win_ref, wout_ref = rest[:4]
    rest = rest[4:]
    fw_ref = rest.pop(0) if final else None
    (o_ref,) = rest
    tile = pl.program_id(0)
    seg = functools.partial(_per_segment, tile, tm, n_lat)
    if pool:
        parts = [_dot(a_ref[:, g * POOL_GROUP:(g + 1) * POOL_GROUP], wo_ref[g]) for g in range(len(POOL_WINDOWS))]
        y = jnp.concatenate(parts, axis=-1) * ps_ref[...]
    elif readout:
        o = a_ref[...].astype(F32) + ob_ref[...].astype(F32)
        heads = [_rms(o[:, h * HG_DK:(h + 1) * HG_DK]) * gn_ref[...] for h in range(HG_HEADS)]
        y = _dot((jnp.concatenate(heads, axis=-1) * gs_ref[...].astype(F32)).astype(BF16), wo_ref[...])
    else:
        y = _dot(a_ref[...], wo_ref[...])
    x0 = _stream_rows(x_ref, ctx_ref, tile, n_lat) if split else x_ref[...]
    nw = nw_ref[...]
    x1 = seg(lambda rows, c: x0[rows] + _mod_vec(mod_ref, G1, c) * y[rows])
    h = seg(lambda rows, c: _modnorm(x1[rows], nw, mod_ref, SH2, SC2, c)).astype(BF16)
    acc = None
    for c0, cw in _ffn_chunks():
        gate = _dot(h, win_ref[:, c0:c0 + cw])
        up = _dot(h, win_ref[:, FFN_HIDDEN + c0:FFN_HIDDEN + c0 + cw])
        part = _dot((_silu(gate) * up).astype(BF16), wout_ref[c0:c0 + cw, :])
        acc = part if acc is None else acc + part
    x2 = seg(lambda rows, c: x1[rows] + _mod_vec(mod_ref, G2, c) * acc[rows])
    o_ref[...] = _rms(x2) * fw_ref[...] if final else x2


def _mix_ffn(x, a, wo, mods, layer, nw2, w_in, w_out, n_lat, rows, pool_scale=None, final_w=None, ctx=None,
             readout=None):
    layer_slab = lambda w: pl.BlockSpec((None,) + w.shape[1:], lambda i: (layer, 0, 0), pipeline_mode=pl.Buffered(1))
    pool = pool_scale is not None
    final = final_w is not None
    tm = _row_tile(rows, ROW_TILE)
    row = lambda w: pl.BlockSpec((tm, w), lambda i: (i, 0))
    in_specs = [row(D_MODEL), row(a.shape[1]), _resident(wo.shape)]
    args = [x, a, wo]
    if ctx is not None:
        assert n_lat % tm + ctx.shape[0] == tm
        in_specs.append(_resident(ctx.shape))
        args.append(ctx)
    if readout is not None:
        o_b, gs, gn_w = readout
        in_specs += [row(D_MODEL), row(D_MODEL), _resident(gn_w.shape)]
        args += [o_b, gs, gn_w]
    if pool:
        in_specs.append(_resident((1, D_MODEL)))
        args.append(pool_scale)
    in_specs += [_mod_spec(layer), _resident((1, D_MODEL)), layer_slab(w_in), layer_slab(w_out)]
    args += [mods, nw2, w_in, w_out]
    if final:
        in_specs.append(_resident((1, D_MODEL)))
        args.append(final_w)
    return pl.pallas_call(
        functools.partial(_mix_ffn_kernel, tm=tm, n_lat=n_lat, pool=pool, final=final, split=ctx is not None,
                          readout=readout is not None),
        grid=(rows // tm,),
        in_specs=in_specs,
        out_specs=row(D_MODEL),
        out_shape=jax.ShapeDtypeStruct((rows, D_MODEL), F32),
        compiler_params=_cparams(("parallel",)),
    )(*args)


def _scan_block(i, nb, rev):
    if rev:
        return jnp.where(i == 0, nb - 1, nb - 1 - i)
    return jnp.where(i == 0, nb - 1, i - 1)


def _hgrn_pass1(q_ref, cum_ref, k_ref, v_ref, rev):
    L = HG_CHUNK
    mid = L // 2 if rev else L // 2 - 1
    end = 0 if rev else L - 1
    att_l, ds_l = {}, {}
    for c in range(TOKEN_BLOCK // L):
        rows = slice(c * L, (c + 1) * L)
        for h in range(HG_HEADS):
            sl = slice(h * HG_DK, (h + 1) * HG_DK)
            q = q_ref[rows, sl].astype(F32)
            k = k_ref[rows, sl].astype(F32)
            cum = cum_ref[rows, sl]
            ref = cum[mid:mid + 1]
            last = cum[end:end + 1]
            att_l[c, h] = _dot_nt((q * jnp.exp2(cum - ref)).astype(BF16), (k * jnp.exp2(ref - cum)).astype(BF16))
            ds_l[c, h] = _dot_tn(v_ref[rows, sl], (k * jnp.exp2(last - cum)).astype(BF16))
    return att_l, ds_l


def _hgrn_pass2(q_ref, cum_ref, v_ref, o_ref, st_ref, d, att_l, ds_l, rev):
    L = HG_CHUNK
    nch = TOKEN_BLOCK // L
    end = 0 if rev else L - 1
    r_i = lax.broadcasted_iota(jnp.int32, (L, L), 0)
    c_i = lax.broadcasted_iota(jnp.int32, (L, L), 1)
    keep = (r_i <= c_i) if rev else (r_i >= c_i)
    for c in (reversed(range(nch)) if rev else range(nch)):
        rows = slice(c * L, (c + 1) * L)
        for h in range(HG_HEADS):
            sl = slice(h * HG_DK, (h + 1) * HG_DK)
            cum = cum_ref[rows, sl]
            st = st_ref[d, h]
            inter = _dot_nt((q_ref[rows, sl].astype(F32) * jnp.exp2(cum)).astype(BF16), st.astype(BF16))
            att = jnp.where(keep, att_l[c, h], 0.0).astype(BF16)
            o_ref[rows, sl] = (inter + _dot(att, v_ref[rows, sl])).astype(o_ref.dtype)
            st_ref[d, h] = st * jnp.exp2(cum[end:end + 1]) + ds_l[c, h]


def _hgrn_kernel(qf_ref, cumf_ref, kf_ref, vf_ref, qb_ref, cumb_ref, kb_ref, vb_ref, of_ref, ob_ref, st_ref):
    @pl.when(pl.program_id(0) == 0)
    def _():
        st_ref[...] = jnp.zeros_like(st_ref)

    p1f = _hgrn_pass1(qf_ref, cumf_ref, kf_ref, vf_ref, False)
    p1b = _hgrn_pass1(qb_ref, cumb_ref, kb_ref, vb_ref, True)
    _hgrn_pass2(qf_ref, cumf_ref, vf_ref, of_ref, st_ref, 0, *p1f, False)
    _hgrn_pass2(qb_ref, cumb_ref, vb_ref, ob_ref, st_ref, 1, *p1b, True)


def _hgrn_scan(qs, cumf, kf, cumb, kb, v):
    m = qs.shape[0]
    nb = m // TOKEN_BLOCK
    fwd = pl.BlockSpec((TOKEN_BLOCK, D_MODEL), lambda i: (_scan_block(i, nb, False), 0))
    bwd = pl.BlockSpec((TOKEN_BLOCK, D_MODEL), lambda i: (_scan_block(i, nb, True), 0))
    out = jax.ShapeDtypeStruct((m, D_MODEL), BF16)
    return pl.pallas_call(
        _hgrn_kernel,
        grid=(nb,),
        in_specs=[fwd] * 4 + [bwd] * 4,
        out_specs=[fwd, bwd],
        out_shape=[out, out],
        scratch_shapes=[pltpu.VMEM((2, HG_HEADS, HG_DK, HG_DK), F32)],
        compiler_params=_cparams(("arbitrary",)),
    )(qs, cumf, kf, v, qs, cumb, kb, v)


def _ssd_kernel(xc_ref, dt_ref, dtb_ref, alog_ref, *rest, rev):
    if rev:
        zs_ref, yf_ref, dsk_ref, nw_ref, o_ref, st_ref, stage_ref = rest
    else:
        y_ref, st_ref = rest
    T = TOKEN_BLOCK

    @pl.when(pl.program_id(0) == 0)
    def _():
        st_ref[...] = jnp.zeros_like(st_ref)

    gw = SSM_HPG * SSM_HEADDIM
    groups = [slice(g * gw, (g + 1) * gw) for g in range(SSM_GROUPS)]
    b_l, cb_l, yoff_l = [], [], []
    for g in range(SSM_GROUPS):
        b_l.append(xc_ref[:, SSM_D_INNER + g * SSM_STATE:SSM_D_INNER + (g + 1) * SSM_STATE])
        c_off = SSM_D_INNER + SSM_GROUPS * SSM_STATE + g * SSM_STATE
        c_g = xc_ref[:, c_off:c_off + SSM_STATE]
        cb_l.append(_dot_nt(c_g, b_l[g]))
        yoff_l.append(_dot(c_g, st_ref[g].astype(BF16)))

    dt = jax.nn.softplus(dt_ref[...] + dtb_ref[...])
    dta = dt * (-jnp.exp(alog_ref[...]))
    r_i = lax.broadcasted_iota(jnp.int32, (T, T), 0)
    c_i = lax.broadcasted_iota(jnp.int32, (T, T), 1)
    keep = (r_i <= c_i) if rev else (r_i >= c_i)
    tri = jnp.where(keep, 1.0, 0.0).astype(BF16)
    cum = sum(_dot(tri, p) for p in _split3(dta))
    end = 0 if rev else T - 1
    total = cum[end:end + 1]
    col2 = cum * LOG2E
    row2_t = (col2 - jnp.log2(dt)).T

    half = LANES // 2
    e_h = lax.broadcasted_iota(jnp.int32, (LANES, SSM_D_INNER), 0)
    e_c = lax.broadcasted_iota(jnp.int32, (LANES, SSM_D_INNER), 1)
    expand_m = jnp.where((e_c // SSM_HEADDIM == e_h % half) & (e_h % half < SSM_HEADS), 1.0, 0.0).astype(BF16)

    def expand(f):
        hi = f.astype(BF16).astype(F32)
        lane_f = lax.broadcasted_iota(jnp.int32, f.shape, 1)
        return _dot(jnp.where(lane_f < half, hi, pltpu.roll(f - hi, half, axis=1)).astype(BF16), expand_m)

    ecum_x = expand(jnp.exp(cum))
    fac_x = expand(dt * jnp.exp(total - cum))
    dec_x = expand(jnp.broadcast_to(jnp.exp(total), (SUBLANES, LANES)))[0:1]

    for g, gl in enumerate(groups):
        xs_fac = (xc_ref[:, gl].astype(F32) * fac_x[:, gl]).astype(BF16)
        st_ref[g] = st_ref[g] * dec_x[:, gl] + _dot_tn(b_l[g], xs_fac)
    lane = lax.broadcasted_iota(jnp.int32, (T, LANES), 1)
    first = lane < SSM_HEADDIM
    hb = T // 2
    keep_h = keep[0:hb, 0:hb]
    lo_, hi_ = slice(0, hb), slice(hb, T)
    for g, gl in enumerate(groups):
        cb = cb_l[g]
        for pr in range(SSM_HPG // 2):
            h0 = g * SSM_HPG + 2 * pr
            ol = slice(g * gw + pr * LANES, g * gw + (pr + 1) * LANES)
            xs_p = xc_ref[:, ol]
            ys = []
            for hh in (h0, h0 + 1):
                col = jnp.broadcast_to(col2[:, hh:hh + 1], (T, hb))
                row = row2_t[hh:hh + 1, :]

                def tile(rs, cs, masked):
                    e = jnp.exp2(col[rs] - row[:, cs])
                    if masked:
                        e = jnp.where(keep_h, e, 0.0)
                    return (cb[rs, cs] * e).astype(BF16)

                if rev:
                    top = _dot(jnp.concatenate([tile(lo_, lo_, True), tile(lo_, hi_, False)], axis=1), xs_p)
                    bot = _dot(tile(hi_, hi_, True), xs_p[hb:])
                else:
                    top = _dot(tile(lo_, lo_, True), xs_p[:hb])
                    bot = _dot(jnp.concatenate([tile(hi_, lo_, False), tile(hi_, hi_, True)], axis=1), xs_p)
                ys.append(jnp.concatenate([top, bot], axis=0))
            yp = yoff_l[g][:, pr * LANES:(pr + 1) * LANES] * ecum_x[:, ol] + jnp.where(first, ys[0], ys[1])
            if rev:
                yt = yp + yf_ref[:, ol].astype(F32) + dsk_ref[:, ol] * xc_ref[:, ol].astype(F32)
                stage_ref[:, ol] = yt * zs_ref[:, ol].astype(F32)
            else:
                y_ref[:, ol] = yp.astype(y_ref.dtype)
        if rev:
            o_ref[:, gl] = (_rms(stage_ref[:, gl]) * nw_ref[:, gl]).astype(o_ref.dtype)


def _ssd_scan(rev, xc, dt_raw, dt_bias, a_log, zs=None, y_f=None, d_skip=None, norm_w=None):
    m = dt_raw.shape[0]
    T = TOKEN_BLOCK
    nb = m // T
    blk = functools.partial(_scan_block, nb=nb, rev=rev)
    rows = lambda w: pl.BlockSpec((T, w), lambda i: (blk(i), 0))
    in_specs = [rows(SSM_XBC), pl.BlockSpec((T, LANES), lambda i: (blk(i), 1 if rev else 0)),
                _resident((1, LANES)), _resident((1, LANES))]
    args = [xc, dt_raw, dt_bias, a_log]
    scratch = [pltpu.VMEM((SSM_GROUPS, SSM_STATE, SSM_HPG * SSM_HEADDIM), F32)]
    if rev:
        in_specs += [rows(SSM_D_INNER), rows(SSM_D_INNER), _resident((1, SSM_D_INNER)), _resident((1, SSM_D_INNER))]
        args += [zs, y_f, d_skip, norm_w]
        scratch.append(pltpu.VMEM((T, SSM_D_INNER), F32))
    return pl.pallas_call(
        functools.partial(_ssd_kernel, rev=rev),
        grid=(nb,),
        in_specs=in_specs,
        out_specs=rows(SSM_D_INNER),
        out_shape=jax.ShapeDtypeStruct((m, SSM_D_INNER), BF16),
        scratch_shapes=scratch,
        compiler_params=_cparams(("arbitrary",)),
    )(*args)


def _pool_kernel(x_ref, xp_ref, xn_ref, nw_ref, mod_ref, o_ref, ext_ref, *, nb, n_lat, n_ctx):
    T = TOKEN_BLOCK
    b = pl.program_id(0)
    is_ctx = b == nb - 1
    vp = jnp.where((b > 0) & (b < nb - 1), 1.0, 0.0)
    vn = jnp.where(b < nb - 2, 1.0, 0.0)
    norm = lambda x: _modnorm(x, nw_ref[...], mod_ref, SH1, SC1, is_ctx)
    ext_ref[0:SUBLANES] = norm(xp_ref[...]) * vp
    ext_ref[SUBLANES:SUBLANES + T] = norm(x_ref[...])
    ext_ref[SUBLANES + T:] = norm(xn_ref[...]) * vn
    t = jnp.where(is_ctx, 0, b * T) + lax.broadcasted_iota(jnp.int32, (T, 1), 0)
    seq = jnp.where(is_ctx, n_ctx, n_lat)
    for gi, win in enumerate(POOL_WINDOWS):
        sl = slice(gi * POOL_GROUP, (gi + 1) * POOL_GROUP)
        run = ext_ref[:, sl]
        k = 1
        while k < win:
            run = run + (_shift_rows(run, k) if k < SUBLANES else jnp.concatenate([run[k:], run[:k]], axis=0))
            k *= 2
        half_w = win // 2
        acc = run[0:T] if half_w == SUBLANES else _shift_rows(run, -half_w)[SUBLANES:SUBLANES + T]
        cnt = jnp.minimum(t + (win - win // 2), seq) - jnp.maximum(t - win // 2, 0)
        o_ref[:, sl] = (acc / cnt.astype(F32) - ext_ref[SUBLANES:SUBLANES + T, sl]).astype(o_ref.dtype)


def _pool_delta(x, nw, mods, layer, n_lat, n_ctx):
    m = x.shape[0]
    T = TOKEN_BLOCK
    nb = m // T
    row8 = T // SUBLANES
    return pl.pallas_call(
        functools.partial(_pool_kernel, nb=nb, n_lat=n_lat, n_ctx=n_ctx),
        grid=(nb,),
        in_specs=[pl.BlockSpec((T, D_MODEL), lambda i: (i, 0)),
                  pl.BlockSpec((SUBLANES, D_MODEL), lambda i: (jnp.maximum(i * row8 - 1, 0), 0)),
                  pl.BlockSpec((SUBLANES, D_MODEL), lambda i: (jnp.minimum((i + 1) * row8, m // SUBLANES - 1), 0)),
                  _resident((1, D_MODEL)), _mod_spec(layer)],
        out_specs=pl.BlockSpec((T, D_MODEL), lambda i: (i, 0)),
        out_shape=jax.ShapeDtypeStruct((m, D_MODEL), BF16),
        scratch_shapes=[pltpu.VMEM((T + 2 * SUBLANES, D_MODEL), F32)],
        compiler_params=_cparams(("parallel",)),
    )(x, x, x, nw, mods)


def _attn_kernel(q_ref, kc_ref, kp_ref, kn_ref, kx_ref, band_ref, sink_ref, o_ref, *, n_lat):
    T = TOKEN_BLOCK
    W = ATT_WINDOW
    i = pl.program_id(0)
    kv = jnp.concatenate([kp_ref[...], kc_ref[...], kn_ref[...], kx_ref[...]], axis=0)
    span = T + 2 * W
    rows = ATT_GRP * T
    kpos = i * T - W + lax.broadcasted_iota(jnp.int32, (1, span), 1)
    bias = band_ref[...] + jnp.where((kpos >= 0) & (kpos < n_lat), 0.0, NEG_INF)
    bias = jnp.concatenate([bias] * ATT_GRP, axis=0)
    head_row = lax.broadcasted_iota(jnp.int32, (rows, 1), 0) // T
    kd = ATT_KV_HEADS * ATT_HEAD_DIM
    def scores(kh):
        k = kv[:, kh * ATT_HEAD_DIM:(kh + 1) * ATT_HEAD_DIM]
        q = jnp.concatenate([q_ref[:, (kh * ATT_GRP + g) * ATT_HEAD_DIM:(kh * ATT_GRP + g + 1) * ATT_HEAD_DIM]
                             for g in range(ATT_GRP)], axis=0)
        return _dot_nt(q, k)

    def finish(kh, raw):
        v = kv[:, kd + kh * ATT_HEAD_DIM:kd + (kh + 1) * ATT_HEAD_DIM]
        s = jnp.concatenate([raw[:, :span] + bias, raw[:, span:]], axis=1)
        sink = jnp.zeros((rows, 1), F32)
        for g in range(ATT_GRP):
            sink = jnp.where(head_row == g, sink_ref[kh * ATT_GRP + g] * LOG2E, sink)
        mx = jnp.maximum(jnp.max(s, axis=-1, keepdims=True), sink)
        p = jnp.exp2(s - mx)
        den = jnp.sum(p, axis=-1, keepdims=True) + jnp.exp2(sink - mx)
        o = _dot(p.astype(BF16), v) / den
        for g in range(0, ATT_GRP, 2):
            pair = jnp.concatenate([o[g * T:(g + 1) * T], o[(g + 1) * T:(g + 2) * T]], axis=-1)
            c0 = (kh * ATT_GRP + g) * ATT_HEAD_DIM
            o_ref[:, c0:c0 + 2 * ATT_HEAD_DIM] = pair.astype(o_ref.dtype)

    raw = scores(0)
    for kh in range(ATT_KV_HEADS):
        nxt = scores(kh + 1) if kh + 1 < ATT_KV_HEADS else None
        finish(kh, raw)
        raw = nxt


def _attention(qkv, sink, n_lat, n_ctx):
    T = TOKEN_BLOCK
    W = ATT_WINDOW
    nq = n_lat // T
    kvw = 2 * ATT_KV_HEADS * ATT_HEAD_DIM
    kv_col = D_MODEL // kvw
    wpb = T // W
    off = jnp.arange(T + 2 * W)[None, :] - jnp.arange(T)[:, None]
    band = jnp.where((off >= 0) & (off <= 2 * W), 0.0, NEG_INF).astype(F32)
    return pl.pallas_call(
        functools.partial(_attn_kernel, n_lat=n_lat),
        grid=(nq,),
        in_specs=[pl.BlockSpec((T, D_MODEL), lambda i: (i, 0)),
                  pl.BlockSpec((T, kvw), lambda i: (i, kv_col)),
                  pl.BlockSpec((W, kvw), lambda i: (jnp.maximum(i * wpb - 1, 0), kv_col)),
                  pl.BlockSpec((W, kvw), lambda i: (jnp.minimum((i + 1) * wpb, n_lat // W - 1), kv_col)),
                  pl.BlockSpec((n_ctx, kvw), lambda i: (n_lat // n_ctx, kv_col)),
                  _resident(band.shape),
                  pl.BlockSpec(memory_space=pltpu.SMEM)],
        out_specs=pl.BlockSpec((T, D_MODEL), lambda i: (i, 0)),
        out_shape=jax.ShapeDtypeStruct((n_lat, D_MODEL), BF16),
        compiler_params=_cparams(("parallel",)),
    )(qkv, qkv, qkv, qkv, qkv, band, sink)


def _rope_tables(n_lat, n_ctx):
    half = ATT_HEAD_DIM // 2
    inv = ROPE_THETA ** (-jnp.arange(0, half, 2, dtype=F32) / half)
    grid_h = n_lat // GRID_W
    ang_r = jnp.arange(grid_h, dtype=F32)[:, None] * inv[None]
    ang_c = jnp.arange(GRID_W, dtype=F32)[:, None] * inv[None]
    first = (jnp.arange(half) < half // 2)
    signed = lambda sn: (jnp.where(first, -sn, 0.0), jnp.where(first, 0.0, sn))
    reps = LANES // ATT_HEAD_DIM

    def lanes(row_part, col_part):
        return jnp.tile(jnp.concatenate([row_part, col_part], axis=-1), (1, reps))

    def parts(ang, is_row):
        two = jnp.concatenate([ang, ang], axis=-1)
        zero = jnp.zeros_like(two)
        out = []
        for t in (jnp.cos(two),) + signed(jnp.sin(two)):
            out.append(lanes(t, zero) if is_row else lanes(zero, t))
        return out

    ident = [jnp.ones((n_ctx // GRID_W, LANES), F32)] + [jnp.zeros((n_ctx // GRID_W, LANES), F32)] * 2
    ta = jnp.stack([jnp.concatenate([a, i_], axis=0) for a, i_ in zip(parts(ang_r, True), ident)])
    tb = jnp.stack(parts(ang_c, False))
    return ta, tb


def kernel(x, c, ctx, c_ctx, ada_w, ada_b, norm_w, ffn_w_in, ffn_w_out, final_norm_w, hg_w_in, hg_lb, hg_norm_w, hg_w_out, ssm_w_in, ssm_conv_w, ssm_conv_b, ssm_dt_bias, ssm_a_log, ssm_d, ssm_norm_w, ssm_w_out, pool_w, pool_scale, att_w_qkv, att_sink, att_w_out):
    assert x.shape[0] == 1 and DEPTH == 4
    n_lat, n_ctx = x.shape[1], ctx.shape[1]
    assert n_lat % TOKEN_BLOCK == 0 and n_ctx == TOKEN_BLOCK
    m = n_lat + n_ctx
    mods = _ada_table(c, c_ctx, ada_w, ada_b)
    lb_tab = jnp.cumsum(jax.nn.softmax(hg_lb.astype(F32), axis=1), axis=1)
    row = lambda v: v.reshape(1, -1)
    ffn_w = (ffn_w_in.astype(BF16), ffn_w_out.astype(BF16))
    ffn = lambda l: ffn_w
    sds = lambda w, dt: jax.ShapeDtypeStruct((m, w), dt)

    w = hg_w_in[0].astype(BF16)
    qs, cumf, kf, cumb, kb, v, gs = _proj_call(
        _proj_hgrn_kernel, x[0], row(norm_w[0, 0]), mods, 0, [w, lb_tab[:, 0]], [_resident(w.shape), _resident((2, D_MODEL))],
        [sds(D_MODEL, BF16), sds(D_MODEL, F32), sds(D_MODEL, BF16), sds(D_MODEL, F32), sds(D_MODEL, BF16),
         sds(D_MODEL, BF16), sds(D_MODEL, BF16)], n_lat, ctx=ctx[0])
    o_f, o_b = _hgrn_scan(qs, cumf, kf, cumb, kb, v)
    xs = _mix_ffn(x[0], o_f, hg_w_out[0].astype(BF16), mods, 0, row(norm_w[0, 1]), *ffn(0), n_lat, m, ctx=ctx[0],
                  readout=(o_b, gs, row(hg_norm_w[0])))

    w = ssm_w_in[0].astype(BF16)
    zs, xc, dt_raw = _proj_call(
        _proj_ssd_kernel, xs, row(norm_w[1, 0]), mods, 1, [w, ssm_conv_w[0], row(ssm_conv_b[0])],
        [_resident(w.shape), _resident((SSM_CONV, SSM_XBC)), _resident((1, SSM_XBC))],
        [sds(SSM_D_INNER, BF16), sds(SSM_XBC, BF16), sds(2 * LANES, F32)], n_lat, halo=True, n_rows=m,
        scratch=lambda tm: [pltpu.VMEM((tm + 2 * SUBLANES, SSM_XBC), F32)])
    lane_pad = lambda v: jnp.concatenate([v, jnp.zeros((LANES - SSM_HEADS,), F32)]).reshape(1, LANES)
    y_f = _ssd_scan(False, xc, dt_raw, lane_pad(ssm_dt_bias[0, 0]), lane_pad(ssm_a_log[0, 0]))
    a = _ssd_scan(True, xc, dt_raw, lane_pad(ssm_dt_bias[0, 1]), lane_pad(ssm_a_log[0, 1]), zs=zs, y_f=y_f,
                  d_skip=row(jnp.repeat(ssm_d[0], SSM_HEADDIM)), norm_w=row(ssm_norm_w[0]))
    xs = _mix_ffn(xs, a, ssm_w_out[0].astype(BF16), mods, 1, row(norm_w[1, 1]), *ffn(1), n_lat, m)

    a = _pool_delta(xs, row(norm_w[2, 0]), mods, 2, n_lat, n_ctx)
    xs = _mix_ffn(xs, a, pool_w[0].astype(BF16), mods, 2, row(norm_w[2, 1]), *ffn(2), n_lat, m,
                  pool_scale=row(pool_scale[0]))

    ta, tb = _rope_tables(n_lat, n_ctx)
    qd = ATT_Q_HEADS * ATT_HEAD_DIM
    kd = ATT_KV_HEADS * ATT_HEAD_DIM
    wq = jnp.concatenate([att_w_qkv[0][:, :qd] * ATT_HEAD_DIM ** -0.5, att_w_qkv[0][:, qd:]], axis=1).astype(BF16)
    (qkv,) = _proj_call(
        _proj_attn_kernel, xs, row(norm_w[3, 0]), mods, 3, [wq, ta, tb], [_resident(wq.shape), _resident(ta.shape), _resident(tb.shape)],
        [sds(qd + 2 * kd, BF16)], n_lat, q_cols=qd, rope_cols=qd + kd)
    a = _attention(qkv, att_sink[0], n_lat, n_ctx)
    out = _mix_ffn(xs, a, att_w_out[0].astype(BF16), mods, 3, row(norm_w[3, 1]), *ffn(3), n_lat, n_lat,
                   final_w=row(final_norm_w))
    return out[None]
```

```python
import functools

import jax
import jax.numpy as jnp
from jax import lax
from jax.experimental import pallas as pl
from jax.experimental.pallas import tpu as pltpu

F32 = jnp.float32
BF16 = jnp.bfloat16

D_MODEL = 1024
DEPTH = 4
GRID_W = 64
EPS = 1e-6
NEG_INF = -1e30
LOG2E = 1.4426950408889634

HG_HEADS = 8
HG_DK = 128
HG_CHUNK = 64

SSM_D_INNER = 2 * D_MODEL
SSM_HEADDIM = 64
SSM_HEADS = SSM_D_INNER // SSM_HEADDIM
SSM_GROUPS = 4
SSM_STATE = 128
SSM_CONV = 5
SSM_BC = 2 * SSM_GROUPS * SSM_STATE
SSM_XBC = SSM_D_INNER + SSM_BC
SSM_HPG = SSM_HEADS // SSM_GROUPS

POOL_WINDOWS = (2, 4, 8, 16)
POOL_GROUP = D_MODEL // len(POOL_WINDOWS)

ATT_HEAD_DIM = 64
ATT_Q_HEADS = D_MODEL // ATT_HEAD_DIM
ATT_KV_HEADS = 4
ATT_GRP = ATT_Q_HEADS // ATT_KV_HEADS
ATT_WINDOW = 128
ROPE_THETA = 10000.0

FFN_HIDDEN = -(-8 * D_MODEL // (3 * 256)) * 256

LANES = 128
SUBLANES = 8
BF16_ROWS = 16
MXU_COLS = 256
TOKEN_BLOCK = 256
ROW_TILE = 640
VMEM_LIMIT = 56 * 1024 * 1024

SH1, SC1, G1, SH2, SC2, G2 = range(6)


def _cparams(sem):
    return pltpu.CompilerParams(dimension_semantics=sem, vmem_limit_bytes=VMEM_LIMIT)


def _row_tile(m, target):
    best = None
    for t in range(HG_CHUNK, target + 1, HG_CHUNK):
        if m % t == 0:
            best = t
    assert best is not None
    return best


def _resident(shape):
    zeros = (0,) * len(shape)
    return pl.BlockSpec(shape, lambda *_: zeros, pipeline_mode=pl.Buffered(1))


def _dot(a, b):
    return jnp.dot(a, b, preferred_element_type=F32)


def _dot_nt(a, b):
    return lax.dot_general(a, b, (((1,), (1,)), ((), ())), preferred_element_type=F32)


def _dot_tn(a, b):
    return lax.dot_general(a, b, (((0,), (0,)), ((), ())), preferred_element_type=F32)


def _split2(x):
    hi = x.astype(BF16)
    lo = (x - hi.astype(F32)).astype(BF16)
    return hi, lo


def _split3(x):
    hi = x.astype(BF16)
    r = x - hi.astype(F32)
    mid = r.astype(BF16)
    lo = (r - mid.astype(F32)).astype(BF16)
    return hi, mid, lo


def _silu(x):
    return x * jax.nn.sigmoid(x)


def _mod_vec(mod_ref, slot, is_ctx):
    lo, hi = slot * D_MODEL, (slot + 1) * D_MODEL
    return jnp.where(is_ctx, mod_ref[1:2, lo:hi], mod_ref[0:1, lo:hi])


def _rms(x):
    return x * lax.rsqrt(jnp.mean(x * x, axis=-1, keepdims=True) + EPS)


def _modnorm(x, nw, mod_ref, sh_slot, sc_slot, is_ctx):
    gain = nw * (1.0 + _mod_vec(mod_ref, sc_slot, is_ctx))
    return _rms(x) * gain + _mod_vec(mod_ref, sh_slot, is_ctx)


def _per_segment(tile, tm, n_lat, fn):
    bt, off = divmod(n_lat, tm)
    if off == 0:
        return fn(slice(0, tm), tile >= bt)
    return jnp.concatenate([fn(slice(0, off), tile > bt), fn(slice(off, tm), tile >= bt)], axis=0)


def _mod_spec(layer):
    return pl.BlockSpec((None, SUBLANES, 6 * D_MODEL), lambda *_: (layer, 0, 0))


def _ada_kernel(c_ref, w_ref, b_ref, o_ref):
    a = _silu(c_ref[...])
    a_hi, a_lo = _split2(a)
    w_hi, w_lo = _split2(w_ref[...])
    o_ref[...] = _dot(a_hi, w_hi) + _dot(a_hi, w_lo) + _dot(a_lo, w_hi) + b_ref[...]


def _ada_table(c, c_ctx, ada_w, ada_b):
    cc = jnp.concatenate([c[0:1], c_ctx[None], jnp.zeros((SUBLANES - 2, D_MODEL), F32)], axis=0)
    n = 6 * D_MODEL
    tn = n // 4
    return pl.pallas_call(
        _ada_kernel,
        grid=(DEPTH, n // tn),
        in_specs=[pl.BlockSpec((SUBLANES, D_MODEL), lambda l, j: (0, 0)),
                  pl.BlockSpec((None, D_MODEL, tn), lambda l, j: (l, 0, j)),
                  pl.BlockSpec((None, 1, tn), lambda l, j: (l, 0, j))],
        out_specs=pl.BlockSpec((None, SUBLANES, tn), lambda l, j: (l, 0, j)),
        out_shape=jax.ShapeDtypeStruct((DEPTH, SUBLANES, n), F32),
        compiler_params=_cparams(("parallel", "parallel")),
    )(cc, ada_w, ada_b.reshape(DEPTH, 1, n))


def _cumsum_rows(x, rev):
    n, width = x.shape
    nt = n // SUBLANES
    t = x.reshape(nt, SUBLANES, width)
    row = lax.broadcasted_iota(jnp.int32, t.shape, 1)
    k = 1
    while k < SUBLANES:
        if rev:
            t = t + jnp.where(row < SUBLANES - k, pltpu.roll(t, SUBLANES - k, axis=1), 0.0)
        else:
            t = t + jnp.where(row >= k, pltpu.roll(t, k, axis=1), 0.0)
        k *= 2
    edge = 0 if rev else SUBLANES - 1
    order = range(nt - 1, -1, -1) if rev else range(nt)
    out, run = [None] * nt, None
    for j in order:
        out[j] = t[j] if run is None else t[j] + run
        run = out[j][edge:edge + 1]
    return jnp.concatenate(out, axis=0)


def _shift_rows(x, s):
    if s == 0:
        return x
    n, width = x.shape
    t = x.reshape(n // SUBLANES, SUBLANES, width)
    row = lax.broadcasted_iota(jnp.int32, t.shape, 1)
    rot = pltpu.roll(t, (-s) % SUBLANES, axis=1)
    if s > 0:
        y = jnp.where(row < SUBLANES - s, rot, jnp.concatenate([rot[1:], rot[:1]], axis=0))
    else:
        y = jnp.where(row >= -s, rot, jnp.concatenate([rot[-1:], rot[:-1]], axis=0))
    return y.reshape(n, width)


def _stream_rows(x_ref, ctx_ref, tile, n_lat):
    tm = x_ref.shape[0]
    off = tm - ctx_ref.shape[0]
    tail = jnp.where(tile >= n_lat // tm, ctx_ref[...], x_ref[off:tm, :])
    return jnp.concatenate([x_ref[0:off, :], tail], axis=0)


def _normed_rows(x_ref, nw_ref, mod_ref, tm, n_lat, ctx_ref=None):
    tile = pl.program_id(0)
    x = x_ref[...] if ctx_ref is None else _stream_rows(x_ref, ctx_ref, tile, n_lat)
    nw = nw_ref[...]
    return _per_segment(tile, tm, n_lat, lambda rows, c: _modnorm(x[rows], nw, mod_ref, SH1, SC1, c)).astype(BF16)


def _proj_hgrn_kernel(x_ref, ctx_ref, nw_ref, mod_ref, w_ref, lb_ref, qs_ref, cumf_ref, kf_ref, cumb_ref, kb_ref, v_ref,
                      gs_ref, *, tm, n_lat):
    h = _normed_rows(x_ref, nw_ref, mod_ref, tm, n_lat, ctx_ref)
    zf, zb, q, v, g = (_dot(h, w_ref[:, k * D_MODEL:(k + 1) * D_MODEL]) for k in (1, 2, 0, 3, 4))
    for d, (z, cum_ref, k_ref) in enumerate(((zf, cumf_ref, kf_ref), (zb, cumb_ref, kb_ref))):
        lb = lb_ref[d:d + 1, :]
        f = lb + (1.0 - lb) * jax.nn.sigmoid(z)
        k_ref[...] = (1.0 - f).astype(k_ref.dtype)
        lf = jnp.log2(f)
        for c in range(tm // HG_CHUNK):
            rows = slice(c * HG_CHUNK, (c + 1) * HG_CHUNK)
            cum_ref[rows, :] = _cumsum_rows(lf[rows], rev=bool(d))
    qs_ref[...] = _silu(q).astype(qs_ref.dtype)
    v_ref[...] = v.astype(v_ref.dtype)
    gs_ref[...] = _silu(g).astype(gs_ref.dtype)


def _conv_silu(taps, cw_ref, cb_ref, cols):
    acc = cb_ref[:, cols] + cw_ref[0:1, cols] * taps[0]
    for j in range(1, SSM_CONV):
        acc = acc + cw_ref[j:j + 1, cols] * taps[j]
    return _silu(acc)


def _proj_ssd_kernel(x_ref, xp_ref, xn_ref, nw_ref, mod_ref, w_ref, cw_ref, cb_ref, zs_ref, xc_ref, dt_ref,
                     ext_ref, *, tm, n_lat, n_rows):
    i = pl.program_id(0)
    nt = n_rows // tm
    halo = SUBLANES
    half = SSM_CONV // 2
    nw = nw_ref[...]
    hf = _per_segment(i, tm, n_lat, lambda rows, c: _modnorm(x_ref[rows, :], nw, mod_ref, SH1, SC1, c))
    halo_norm = lambda ref, row0: _modnorm(ref[...], nw, mod_ref, SH1, SC1, row0 >= n_lat)
    h = hf.astype(BF16)
    for c in range(SSM_D_INNER // D_MODEL):
        sl = slice(c * D_MODEL, (c + 1) * D_MODEL)
        zs_ref[:, sl] = _silu(_dot(h, w_ref[:, sl])).astype(zs_ref.dtype)
    dt2 = _dot(h, w_ref[:, SSM_D_INNER + SSM_XBC:])
    lane_pad = jnp.zeros((tm, LANES - SSM_HEADS), F32)
    dt_ref[...] = jnp.concatenate([dt2[:, :SSM_HEADS], lane_pad, dt2[:, SSM_HEADS:], lane_pad], axis=1)
    h_ext = jnp.concatenate([halo_norm(xp_ref, i * tm - halo), hf, halo_norm(xn_ref, (i + 1) * tm)], axis=0).astype(BF16)
    vp = jnp.where(i > 0, 1.0, 0.0)
    vn = jnp.where(i < nt - 1, 1.0, 0.0)
    chunks = [slice(c * D_MODEL, (c + 1) * D_MODEL) for c in range(SSM_XBC // D_MODEL)]
    for cols in chunks:
        y = _dot(h_ext, w_ref[:, SSM_D_INNER + cols.start:SSM_D_INNER + cols.stop])
        ext_ref[0:halo, cols] = y[0:halo] * vp
        ext_ref[halo:halo + tm, cols] = y[halo:halo + tm]
        ext_ref[halo + tm:, cols] = y[halo + tm:] * vn
    for cols in chunks:
        ext = ext_ref[:, cols]
        taps = [_shift_rows(ext, j - half)[halo:halo + tm] for j in range(SSM_CONV)]
        xc_ref[:, cols] = _conv_silu(taps, cw_ref, cb_ref, cols).astype(xc_ref.dtype)
    if n_lat % tm:
        bt, off = divmod(n_lat, tm)
        fix = BF16_ROWS
        assert off % fix == 0 and fix <= off <= tm - fix

        @pl.when(i == bt)
        def _():
            e0 = halo + off
            is_lat_row = lax.broadcasted_iota(jnp.int32, (2 * fix + 2 * halo, 1), 0) < fix + halo
            for cols in chunks:
                win = ext_ref[e0 - fix - halo:e0 + fix + halo, cols]
                lat_w = jnp.where(is_lat_row, win, 0.0)
                ctx_w = jnp.where(is_lat_row, 0.0, win)
                lat_taps = [lat_w[halo - half + j:halo - half + j + fix] for j in range(SSM_CONV)]
                ctx_taps = [ctx_w[halo + fix - half + j:halo + fix - half + j + fix] for j in range(SSM_CONV)]
                xc_ref[off - fix:off, cols] = _conv_silu(lat_taps, cw_ref, cb_ref, cols).astype(xc_ref.dtype)
                xc_ref[off:off + fix, cols] = _conv_silu(ctx_taps, cw_ref, cb_ref, cols).astype(xc_ref.dtype)


def _proj_attn_kernel(x_ref, nw_ref, mod_ref, w_ref, ta_ref, tb_ref, o_ref, *, tm, n_lat, q_cols, rope_cols):
    h = _normed_rows(x_ref, nw_ref, mod_ref, tm, n_lat)
    n = w_ref.shape[1]
    gpt = tm // GRID_W
    g0 = pl.program_id(0) * gpt
    tabs = []
    for k in range(ta_ref.shape[0]):
        rows = [ta_ref[k, pl.ds(g0 + g, 1), :] + jnp.where(g0 + g < n_lat // GRID_W, tb_ref[k], 0.0) for g in range(gpt)]
        tabs.append(jnp.concatenate(rows, axis=0))
    cos, sa, sb = tabs
    for c in range(n // MXU_COLS):
        y = _dot(h, w_ref[:, c * MXU_COLS:(c + 1) * MXU_COLS])
        for s_ in range(MXU_COLS // LANES):
            c0 = c * MXU_COLS + s_ * LANES
            ys = y[:, s_ * LANES:(s_ + 1) * LANES]
            if c0 < rope_cols:
                ys = ys * cos + pltpu.roll(ys, LANES - 16, axis=1) * sa + pltpu.roll(ys, 16, axis=1) * sb
            if c0 < q_cols:
                ys = ys * LOG2E
            o_ref[:, c0:c0 + LANES] = ys.astype(o_ref.dtype)


def _proj_call(kernel_fn, x, nw, mods, layer, extra_in, extra_specs, out_shapes, n_lat, scratch=None, halo=False, ctx=None,
               **kw):
    m = out_shapes[0].shape[0]
    tm = _row_tile(m, ROW_TILE)
    row = lambda w: pl.BlockSpec((tm, w), lambda i: (i, 0))
    lead, lead_specs = [x], [row(D_MODEL)]
    if ctx is not None:
        assert n_lat % tm + ctx.shape[0] == tm
        lead.append(ctx)
        lead_specs.append(_resident(ctx.shape))
    if halo:
        tpb = tm // SUBLANES
        lead += [x, x]
        lead_specs += [pl.BlockSpec((SUBLANES, D_MODEL), lambda i: (jnp.maximum(i * tpb - 1, 0), 0)),
                       pl.BlockSpec((SUBLANES, D_MODEL), lambda i: (jnp.minimum((i + 1) * tpb, m // SUBLANES - 1), 0))]
    return pl.pallas_call(
        functools.partial(kernel_fn, tm=tm, n_lat=n_lat, **kw),
        grid=(m // tm,),
        in_specs=lead_specs + [_resident((1, D_MODEL)), _mod_spec(layer)] + [s(tm) if callable(s) else s for s in extra_specs],
        out_specs=[row(s.shape[1]) for s in out_shapes],
        out_shape=out_shapes,
        scratch_shapes=scratch(tm) if scratch else [],
        compiler_params=_cparams(("parallel",)),
    )(*lead, nw, mods, *extra_in)


def _ffn_chunks():
    out, c0 = [], 0
    while c0 < FFN_HIDDEN:
        cw = min(D_MODEL, FFN_HIDDEN - c0)
        assert cw % MXU_COLS == 0
        out.append((c0, cw))
        c0 += cw
    return out


def _mix_ffn_kernel(x_ref, a_ref, wo_ref, *rest, tm, n_lat, pool, final, split, readout):
    rest = list(rest)
    ctx_ref = rest.pop(0) if split else None
    ob_ref, gs_ref, gn_ref = (rest.pop(0), rest.pop(0), rest.pop(0)) if readout else (None, None, None)
    ps_ref = rest.pop(0) if pool else None
    mod_ref, nw_ref, win_ref, wout_ref = rest[:4]
    rest = rest[4:]
    fw_ref = rest.pop(0) if final else None
    (o_ref,) = rest
    tile = pl.program_id(0)
    seg = functools.partial(_per_segment, tile, tm, n_lat)
    if pool:
        parts = [_dot(a_ref[:, g * POOL_GROUP:(g + 1) * POOL_GROUP], wo_ref[g]) for g in range(len(POOL_WINDOWS))]
        y = jnp.concatenate(parts, axis=-1) * ps_ref[...]
    elif readout:
        o = a_ref[...].astype(F32) + ob_ref[...].astype(F32)
        heads = [_rms(o[:, h * HG_DK:(h + 1) * HG_DK]) * gn_ref[...] for h in range(HG_HEADS)]
        y = _dot((jnp.concatenate(heads, axis=-1) * gs_ref[...].astype(F32)).astype(BF16), wo_ref[...])
    else:
        y = _dot(a_ref[...], wo_ref[...])
    x0 = _stream_rows(x_ref, ctx_ref, tile, n_lat) if split else x_ref[...]
    nw = nw_ref[...]
    x1 = seg(lambda rows, c: x0[rows] + _mod_vec(mod_ref, G1, c) * y[rows])
    h = seg(lambda rows, c: _modnorm(x1[rows], nw, mod_ref, SH2, SC2, c)).astype(BF16)
    acc = None
    for c0, cw in _ffn_chunks():
        gate = _dot(h, win_ref[:, c0:c0 + cw])
        up = _dot(h, win_ref[:, FFN_HIDDEN + c0:FFN_HIDDEN + c0 + cw])
        part = _dot((_silu(gate) * up).astype(BF16), wout_ref[c0:c0 + cw, :])
        acc = part if acc is None else acc + part
    x2 = seg(lambda rows, c: x1[rows] + _mod_vec(mod_ref, G2, c) * acc[rows])
    o_ref[...] = _rms(x2) * fw_ref[...] if final else x2


def _mix_ffn(x, a, wo, mods, layer, nw2, w_in, w_out, n_lat, rows, pool_scale=None, final_w=None, ctx=None,
             readout=None):
    layer_slab = lambda w: pl.BlockSpec((None,) + w.shape[1:], lambda i: (layer, 0, 0), pipeline_mode=pl.Buffered(1))
    pool = pool_scale is not None
    final = final_w is not None
    tm = _row_tile(rows, ROW_TILE)
    row = lambda w: pl.BlockSpec((tm, w), lambda i: (i, 0))
    in_specs = [row(D_MODEL), row(a.shape[1]), _resident(wo.shape)]
    args = [x, a, wo]
    if ctx is not None:
        assert n_lat % tm + ctx.shape[0] == tm
        in_specs.append(_resident(ctx.shape))
        args.append(ctx)
    if readout is not None:
        o_b, gs, gn_w = readout
        in_specs += [row(D_MODEL), row(D_MODEL), _resident(gn_w.shape)]
        args += [o_b, gs, gn_w]
    if pool:
        in_specs.append(_resident((1, D_MODEL)))
        args.append(pool_scale)
    in_specs += [_mod_spec(layer), _resident((1, D_MODEL)), layer_slab(w_in), layer_slab(w_out)]
    args += [mods, nw2, w_in, w_out]
    if final:
        in_specs.append(_resident((1, D_MODEL)))
        args.append(final_w)
    return pl.pallas_call(
        functools.partial(_mix_ffn_kernel, tm=tm, n_lat=n_lat, pool=pool, final=final, split=ctx is not None,
                          readout=readout is not None),
        grid=(rows // tm,),
        in_specs=in_specs,
        out_specs=row(D_MODEL),
        out_shape=jax.ShapeDtypeStruct((rows, D_MODEL), F32),
        compiler_params=_cparams(("parallel",)),
    )(*args)


def _scan_block(i, nb, rev):
    if rev:
        return jnp.where(i == 0, nb - 1, nb - 1 - i)
    return jnp.where(i == 0, nb - 1, i - 1)


def _hgrn_pass1(q_ref, cum_ref, k_ref, v_ref, rev):
    L = HG_CHUNK
    mid = L // 2 if rev else L // 2 - 1
    end = 0 if rev else L - 1
    att_l, ds_l = {}, {}
    for c in range(TOKEN_BLOCK // L):
        rows = slice(c * L, (c + 1) * L)
        for h in range(HG_HEADS):
            sl = slice(h * HG_DK, (h + 1) * HG_DK)
            q = q_ref[rows, sl].astype(F32)
            k = k_ref[rows, sl].astype(F32)
            cum = cum_ref[rows, sl]
            ref = cum[mid:mid + 1]
            last = cum[end:end + 1]
            att_l[c, h] = _dot_nt((q * jnp.exp2(cum - ref)).astype(BF16), (k * jnp.exp2(ref - cum)).astype(BF16))
            ds_l[c, h] = _dot_tn(v_ref[rows, sl], (k * jnp.exp2(last - cum)).astype(BF16))
    return att_l, ds_l


def _hgrn_pass2(q_ref, cum_ref, v_ref, o_ref, st_ref, d, att_l, ds_l, rev):
    L = HG_CHUNK
    nch = TOKEN_BLOCK // L
    end = 0 if rev else L - 1
    r_i = lax.broadcasted_iota(jnp.int32, (L, L), 0)
    c_i = lax.broadcasted_iota(jnp.int32, (L, L), 1)
    keep = (r_i <= c_i) if rev else (r_i >= c_i)
    states = [st_ref[d, h] for h in range(HG_HEADS)]
    for c in (reversed(range(nch)) if rev else range(nch)):
        rows = slice(c * L, (c + 1) * L)
        for h in range(HG_HEADS):
            sl = slice(h * HG_DK, (h + 1) * HG_DK)
            cum = cum_ref[rows, sl]
            st = states[h]
            inter = _dot_nt((q_ref[rows, sl].astype(F32) * jnp.exp2(cum)).astype(BF16), st.astype(BF16))
            att = jnp.where(keep, att_l[c, h], 0.0).astype(BF16)
            o_ref[rows, sl] = (inter + _dot(att, v_ref[rows, sl])).astype(o_ref.dtype)
            states[h] = st * jnp.exp2(cum[end:end + 1]) + ds_l[c, h]
    for h in range(HG_HEADS):
        st_ref[d, h] = states[h]


def _hgrn_kernel(qf_ref, cumf_ref, kf_ref, vf_ref, qb_ref, cumb_ref, kb_ref, vb_ref, of_ref, ob_ref, st_ref):
    @pl.when(pl.program_id(0) == 0)
    def _():
        st_ref[...] = jnp.zeros_like(st_ref)

    p1f = _hgrn_pass1(qf_ref, cumf_ref, kf_ref, vf_ref, False)
    p1b = _hgrn_pass1(qb_ref, cumb_ref, kb_ref, vb_ref, True)
    _hgrn_pass2(qf_ref, cumf_ref, vf_ref, of_ref, st_ref, 0, *p1f, False)
    _hgrn_pass2(qb_ref, cumb_ref, vb_ref, ob_ref, st_ref, 1, *p1b, True)


def _hgrn_scan(qs, cumf, kf, cumb, kb, v):
    m = qs.shape[0]
    nb = m // TOKEN_BLOCK
    fwd = pl.BlockSpec((TOKEN_BLOCK, D_MODEL), lambda i: (_scan_block(i, nb, False), 0))
    bwd = pl.BlockSpec((TOKEN_BLOCK, D_MODEL), lambda i: (_scan_block(i, nb, True), 0))
    out = jax.ShapeDtypeStruct((m, D_MODEL), BF16)
    return pl.pallas_call(
        _hgrn_kernel,
        grid=(nb,),
        in_specs=[fwd] * 4 + [bwd] * 4,
        out_specs=[fwd, bwd],
        out_shape=[out, out],
        scratch_shapes=[pltpu.VMEM((2, HG_HEADS, HG_DK, HG_DK), F32)],
        compiler_params=_cparams(("arbitrary",)),
    )(qs, cumf, kf, v, qs, cumb, kb, v)


def _ssd_kernel(xc_ref, dt_ref, dtb_ref, alog_ref, *rest, rev):
    if rev:
        zs_ref, yf_ref, dsk_ref, nw_ref, o_ref, st_ref, stage_ref = rest
    else:
        y_ref, st_ref = rest
    T = TOKEN_BLOCK

    @pl.when(pl.program_id(0) == 0)
    def _():
        st_ref[...] = jnp.zeros_like(st_ref)

    gw = SSM_HPG * SSM_HEADDIM
    groups = [slice(g * gw, (g + 1) * gw) for g in range(SSM_GROUPS)]
    b_l, cb_l, yoff_l = [], [], []
    for g in range(SSM_GROUPS):
        b_l.append(xc_ref[:, SSM_D_INNER + g * SSM_STATE:SSM_D_INNER + (g + 1) * SSM_STATE])
        c_off = SSM_D_INNER + SSM_GROUPS * SSM_STATE + g * SSM_STATE
        c_g = xc_ref[:, c_off:c_off + SSM_STATE]
        cb_l.append(_dot_nt(c_g, b_l[g]))
        yoff_l.append(_dot(c_g, st_ref[g].astype(BF16)))

    dt = jax.nn.softplus(dt_ref[...] + dtb_ref[...])
    dta = dt * (-jnp.exp(alog_ref[...]))
    r_i = lax.broadcasted_iota(jnp.int32, (T, T), 0)
    c_i = lax.broadcasted_iota(jnp.int32, (T, T), 1)
    keep = (r_i <= c_i) if rev else (r_i >= c_i)
    tri = jnp.where(keep, 1.0, 0.0).astype(BF16)
    cum = sum(_dot(tri, p) for p in _split3(dta))
    end = 0 if rev else T - 1
    total = cum[end:end + 1]
    col2 = cum * LOG2E
    row2_t = (col2 - jnp.log2(dt)).T

    half = LANES // 2
    e_h = lax.broadcasted_iota(jnp.int32, (LANES, SSM_D_INNER), 0)
    e_c = lax.broadcasted_iota(jnp.int32, (LANES, SSM_D_INNER), 1)
    expand_m = jnp.where((e_c // SSM_HEADDIM == e_h % half) & (e_h % half < SSM_HEADS), 1.0, 0.0).astype(BF16)

    def expand(f):
        hi = f.astype(BF16).astype(F32)
        lane_f = lax.broadcasted_iota(jnp.int32, f.shape, 1)
        return _dot(jnp.where(lane_f < half, hi, pltpu.roll(f - hi, half, axis=1)).astype(BF16), expand_m)

    ecum_x = expand(jnp.exp(cum))
    fac_x = expand(dt * jnp.exp(total - cum))
    dec_x = expand(jnp.broadcast_to(jnp.exp(total), (SUBLANES, LANES)))[0:1]

    for g, gl in enumerate(groups):
        xs_fac = (xc_ref[:, gl].astype(F32) * fac_x[:, gl]).astype(BF16)
        st_ref[g] = st_ref[g] * dec_x[:, gl] + _dot_tn(b_l[g], xs_fac)
    lane = lax.broadcasted_iota(jnp.int32, (T, LANES), 1)
    first = lane < SSM_HEADDIM
    hb = T // 2
    keep_h = keep[0:hb, 0:hb]
    lo_, hi_ = slice(0, hb), slice(hb, T)
    for g, gl in enumerate(groups):
        cb = cb_l[g]
        for pr in range(SSM_HPG // 2):
            h0 = g * SSM_HPG + 2 * pr
            ol = slice(g * gw + pr * LANES, g * gw + (pr + 1) * LANES)
            xs_p = xc_ref[:, ol]
            ys = []
            for hh in (h0, h0 + 1):
                col = jnp.broadcast_to(col2[:, hh:hh + 1], (T, hb))
                row = row2_t[hh:hh + 1, :]

                def tile(rs, cs, masked):
                    e = jnp.exp2(col[rs] - row[:, cs])
                    if masked:
                        e = jnp.where(keep_h, e, 0.0)
                    return (cb[rs, cs] * e).astype(BF16)

                if rev:
                    top = _dot(jnp.concatenate([tile(lo_, lo_, True), tile(lo_, hi_, False)], axis=1), xs_p)
                    bot = _dot(tile(hi_, hi_, True), xs_p[hb:])
                else:
                    top = _dot(tile(lo_, lo_, True), xs_p[:hb])
                    bot = _dot(jnp.concatenate([tile(hi_, lo_, False), tile(hi_, hi_, True)], axis=1), xs_p)
                ys.append(jnp.concatenate([top, bot], axis=0))
            yp = yoff_l[g][:, pr * LANES:(pr + 1) * LANES] * ecum_x[:, ol] + jnp.where(first, ys[0], ys[1])
            if rev:
                yt = yp + yf_ref[:, ol].astype(F32) + dsk_ref[:, ol] * xc_ref[:, ol].astype(F32)
                stage_ref[:, ol] = yt * zs_ref[:, ol].astype(F32)
            else:
                y_ref[:, ol] = yp.astype(y_ref.dtype)
        if rev:
            o_ref[:, gl] = (_rms(stage_ref[:, gl]) * nw_ref[:, gl]).astype(o_ref.dtype)


def _ssd_scan(rev, xc, dt_raw, dt_bias, a_log, zs=None, y_f=None, d_skip=None, norm_w=None):
    m = dt_raw.shape[0]
    T = TOKEN_BLOCK
    nb = m // T
    blk = functools.partial(_scan_block, nb=nb, rev=rev)
    rows = lambda w: pl.BlockSpec((T, w), lambda i: (blk(i), 0))
    in_specs = [rows(SSM_XBC), pl.BlockSpec((T, LANES), lambda i: (blk(i), 1 if rev else 0)),
                _resident((1, LANES)), _resident((1, LANES))]
    args = [xc, dt_raw, dt_bias, a_log]
    scratch = [pltpu.VMEM((SSM_GROUPS, SSM_STATE, SSM_HPG * SSM_HEADDIM), F32)]
    if rev:
        in_specs += [rows(SSM_D_INNER), rows(SSM_D_INNER), _resident((1, SSM_D_INNER)), _resident((1, SSM_D_INNER))]
        args += [zs, y_f, d_skip, norm_w]
        scratch.append(pltpu.VMEM((T, SSM_D_INNER), F32))
    return pl.pallas_call(
        functools.partial(_ssd_kernel, rev=rev),
        grid=(nb,),
        in_specs=in_specs,
        out_specs=rows(SSM_D_INNER),
        out_shape=jax.ShapeDtypeStruct((m, SSM_D_INNER), BF16),
        scratch_shapes=scratch,
        compiler_params=_cparams(("arbitrary",)),
    )(*args)


def _pool_kernel(x_ref, xp_ref, xn_ref, nw_ref, mod_ref, o_ref, ext_ref, *, nb, n_lat, n_ctx):
    T = TOKEN_BLOCK
    b = pl.program_id(0)
    is_ctx = b == nb - 1
    vp = jnp.where((b > 0) & (b < nb - 1), 1.0, 0.0)
    vn = jnp.where(b < nb - 2, 1.0, 0.0)
    norm = lambda x: _modnorm(x, nw_ref[...], mod_ref, SH1, SC1, is_ctx)
    ext_ref[0:SUBLANES] = norm(xp_ref[...]) * vp
    ext_ref[SUBLANES:SUBLANES + T] = norm(x_ref[...])
    ext_ref[SUBLANES + T:] = norm(xn_ref[...]) * vn
    t = jnp.where(is_ctx, 0, b * T) + lax.broadcasted_iota(jnp.int32, (T, 1), 0)
    seq = jnp.where(is_ctx, n_ctx, n_lat)
    for gi, win in enumerate(POOL_WINDOWS):
        sl = slice(gi * POOL_GROUP, (gi + 1) * POOL_GROUP)
        run = ext_ref[:, sl]
        k = 1
        while k < win:
            run = run + (_shift_rows(run, k) if k < SUBLANES else jnp.concatenate([run[k:], run[:k]], axis=0))
            k *= 2
        half_w = win // 2
        acc = run[0:T] if half_w == SUBLANES else _shift_rows(run, -half_w)[SUBLANES:SUBLANES + T]
        cnt = jnp.minimum(t + (win - win // 2), seq) - jnp.maximum(t - win // 2, 0)
        o_ref[:, sl] = (acc / cnt.astype(F32) - ext_ref[SUBLANES:SUBLANES + T, sl]).astype(o_ref.dtype)


def _pool_delta(x, nw, mods, layer, n_lat, n_ctx):
    m = x.shape[0]
    T = TOKEN_BLOCK
    nb = m // T
    row8 = T // SUBLANES
    return pl.pallas_call(
        functools.partial(_pool_kernel, nb=nb, n_lat=n_lat, n_ctx=n_ctx),
        grid=(nb,),
        in_specs=[pl.BlockSpec((T, D_MODEL), lambda i: (i, 0)),
                  pl.BlockSpec((SUBLANES, D_MODEL), lambda i: (jnp.maximum(i * row8 - 1, 0), 0)),
                  pl.BlockSpec((SUBLANES, D_MODEL), lambda i: (jnp.minimum((i + 1) * row8, m // SUBLANES - 1), 0)),
                  _resident((1, D_MODEL)), _mod_spec(layer)],
        out_specs=pl.BlockSpec((T, D_MODEL), lambda i: (i, 0)),
        out_shape=jax.ShapeDtypeStruct((m, D_MODEL), BF16),
        scratch_shapes=[pltpu.VMEM((T + 2 * SUBLANES, D_MODEL), F32)],
        compiler_params=_cparams(("parallel",)),
    )(x, x, x, nw, mods)


def _attn_kernel(q_ref, kc_ref, kp_ref, kn_ref, kx_ref, band_ref, sink_ref, o_ref, *, n_lat):
    T = TOKEN_BLOCK
    W = ATT_WINDOW
    i = pl.program_id(0)
    kv = jnp.concatenate([kp_ref[...], kc_ref[...], kn_ref[...], kx_ref[...]], axis=0)
    span = T + 2 * W
    rows = ATT_GRP * T
    kpos = i * T - W + lax.broadcasted_iota(jnp.int32, (1, span), 1)
    bias = band_ref[...] + jnp.where((kpos >= 0) & (kpos < n_lat), 0.0, NEG_INF)
    bias = jnp.concatenate([bias] * ATT_GRP, axis=0)
    head_row = lax.broadcasted_iota(jnp.int32, (rows, 1), 0) // T
    kd = ATT_KV_HEADS * ATT_HEAD_DIM
    def scores(kh):
        k = kv[:, kh * ATT_HEAD_DIM:(kh + 1) * ATT_HEAD_DIM]
        q = jnp.concatenate([q_ref[:, (kh * ATT_GRP + g) * ATT_HEAD_DIM:(kh * ATT_GRP + g + 1) * ATT_HEAD_DIM]
                             for g in range(ATT_GRP)], axis=0)
        return _dot_nt(q, k)

    def finish(kh, raw):
        v = kv[:, kd + kh * ATT_HEAD_DIM:kd + (kh + 1) * ATT_HEAD_DIM]
        s = jnp.concatenate([raw[:, :span] + bias, raw[:, span:]], axis=1)
        sink = jnp.zeros((rows, 1), F32)
        for g in range(ATT_GRP):
            sink = jnp.where(head_row == g, sink_ref[kh * ATT_GRP + g] * LOG2E, sink)
        mx = jnp.maximum(jnp.max(s, axis=-1, keepdims=True), sink)
        p = jnp.exp2(s - mx)
        den = jnp.sum(p, axis=-1, keepdims=True) + jnp.exp2(sink - mx)
        o = _dot(p.astype(BF16), v) / den
        for g in range(0, ATT_GRP, 2):
            pair = jnp.concatenate([o[g * T:(g + 1) * T], o[(g + 1) * T:(g + 2) * T]], axis=-1)
            c0 = (kh * ATT_GRP + g) * ATT_HEAD_DIM
            o_ref[:, c0:c0 + 2 * ATT_HEAD_DIM] = pair.astype(o_ref.dtype)

    raw = scores(0)
    for kh in range(ATT_KV_HEADS):
        nxt = scores(kh + 1) if kh + 1 < ATT_KV_HEADS else None
        finish(kh, raw)
        raw = nxt


def _attention(qkv, sink, n_lat, n_ctx):
    T = TOKEN_BLOCK
    W = ATT_WINDOW
    nq = n_lat // T
    kvw = 2 * ATT_KV_HEADS * ATT_HEAD_DIM
    kv_col = D_MODEL // kvw
    wpb = T // W
    off = jnp.arange(T + 2 * W)[None, :] - jnp.arange(T)[:, None]
    band = jnp.where((off >= 0) & (off <= 2 * W), 0.0, NEG_INF).astype(F32)
    return pl.pallas_call(
        functools.partial(_attn_kernel, n_lat=n_lat),
        grid=(nq,),
        in_specs=[pl.BlockSpec((T, D_MODEL), lambda i: (i, 0)),
                  pl.BlockSpec((T, kvw), lambda i: (i, kv_col)),
                  pl.BlockSpec((W, kvw), lambda i: (jnp.maximum(i * wpb - 1, 0), kv_col)),
                  pl.BlockSpec((W, kvw), lambda i: (jnp.minimum((i + 1) * wpb, n_lat // W - 1), kv_col)),
                  pl.BlockSpec((n_ctx, kvw), lambda i: (n_lat // n_ctx, kv_col)),
                  _resident(band.shape),
                  pl.BlockSpec(memory_space=pltpu.SMEM)],
        out_specs=pl.BlockSpec((T, D_MODEL), lambda i: (i, 0)),
        out_shape=jax.ShapeDtypeStruct((n_lat, D_MODEL), BF16),
        compiler_params=_cparams(("parallel",)),
    )(qkv, qkv, qkv, qkv, qkv, band, sink)


def _rope_tables(n_lat, n_ctx):
    half = ATT_HEAD_DIM // 2
    inv = ROPE_THETA ** (-jnp.arange(0, half, 2, dtype=F32) / half)
    grid_h = n_lat // GRID_W
    ang_r = jnp.arange(grid_h, dtype=F32)[:, None] * inv[None]
    ang_c = jnp.arange(GRID_W, dtype=F32)[:, None] * inv[None]
    first = (jnp.arange(half) < half // 2)
    signed = lambda sn: (jnp.where(first, -sn, 0.0), jnp.where(first, 0.0, sn))
    reps = LANES // ATT_HEAD_DIM

    def lanes(row_part, col_part):
        return jnp.tile(jnp.concatenate([row_part, col_part], axis=-1), (1, reps))

    def parts(ang, is_row):
        two = jnp.concatenate([ang, ang], axis=-1)
        zero = jnp.zeros_like(two)
        out = []
        for t in (jnp.cos(two),) + signed(jnp.sin(two)):
            out.append(lanes(t, zero) if is_row else lanes(zero, t))
        return out

    ident = [jnp.ones((n_ctx // GRID_W, LANES), F32)] + [jnp.zeros((n_ctx // GRID_W, LANES), F32)] * 2
    ta = jnp.stack([jnp.concatenate([a, i_], axis=0) for a, i_ in zip(parts(ang_r, True), ident)])
    tb = jnp.stack(parts(ang_c, False))
    return ta, tb


def kernel(x, c, ctx, c_ctx, ada_w, ada_b, norm_w, ffn_w_in, ffn_w_out, final_norm_w, hg_w_in, hg_lb, hg_norm_w, hg_w_out, ssm_w_in, ssm_conv_w, ssm_conv_b, ssm_dt_bias, ssm_a_log, ssm_d, ssm_norm_w, ssm_w_out, pool_w, pool_scale, att_w_qkv, att_sink, att_w_out):
    assert x.shape[0] == 1 and DEPTH == 4
    n_lat, n_ctx = x.shape[1], ctx.shape[1]
    assert n_lat % TOKEN_BLOCK == 0 and n_ctx == TOKEN_BLOCK
    m = n_lat + n_ctx
    mods = _ada_table(c, c_ctx, ada_w, ada_b)
    lb_tab = jnp.cumsum(jax.nn.softmax(hg_lb.astype(F32), axis=1), axis=1)
    row = lambda v: v.reshape(1, -1)
    ffn_w = (ffn_w_in.astype(BF16), ffn_w_out.astype(BF16))
    ffn = lambda l: ffn_w
    sds = lambda w, dt: jax.ShapeDtypeStruct((m, w), dt)

    w = hg_w_in[0].astype(BF16)
    qs, cumf, kf, cumb, kb, v, gs = _proj_call(
        _proj_hgrn_kernel, x[0], row(norm_w[0, 0]), mods, 0, [w, lb_tab[:, 0]], [_resident(w.shape), _resident((2, D_MODEL))],
        [sds(D_MODEL, BF16), sds(D_MODEL, F32), sds(D_MODEL, BF16), sds(D_MODEL, F32), sds(D_MODEL, BF16),
         sds(D_MODEL, BF16), sds(D_MODEL, BF16)], n_lat, ctx=ctx[0])
    o_f, o_b = _hgrn_scan(qs, cumf, kf, cumb, kb, v)
    xs = _mix_ffn(x[0], o_f, hg_w_out[0].astype(BF16), mods, 0, row(norm_w[0, 1]), *ffn(0), n_lat, m, ctx=ctx[0],
                  readout=(o_b, gs, row(hg_norm_w[0])))

    w = ssm_w_in[0].astype(BF16)
    zs, xc, dt_raw = _proj_call(
        _proj_ssd_kernel, xs, row(norm_w[1, 0]), mods, 1, [w, ssm_conv_w[0], row(ssm_conv_b[0])],
        [_resident(w.shape), _resident((SSM_CONV, SSM_XBC)), _resident((1, SSM_XBC))],
        [sds(SSM_D_INNER, BF16), sds(SSM_XBC, BF16), sds(2 * LANES, F32)], n_lat, halo=True, n_rows=m,
        scratch=lambda tm: [pltpu.VMEM((tm + 2 * SUBLANES, SSM_XBC), F32)])
    lane_pad = lambda v: jnp.concatenate([v, jnp.zeros((LANES - SSM_HEADS,), F32)]).reshape(1, LANES)
    y_f = _ssd_scan(False, xc, dt_raw, lane_pad(ssm_dt_bias[0, 0]), lane_pad(ssm_a_log[0, 0]))
    a = _ssd_scan(True, xc, dt_raw, lane_pad(ssm_dt_bias[0, 1]), lane_pad(ssm_a_log[0, 1]), zs=zs, y_f=y_f,
                  d_skip=row(jnp.repeat(ssm_d[0], SSM_HEADDIM)), norm_w=row(ssm_norm_w[0]))
    xs = _mix_ffn(xs, a, ssm_w_out[0].astype(BF16), mods, 1, row(norm_w[1, 1]), *ffn(1), n_lat, m)

    a = _pool_delta(xs, row(norm_w[2, 0]), mods, 2, n_lat, n_ctx)
    xs = _mix_ffn(xs, a, pool_w[0].astype(BF16), mods, 2, row(norm_w[2, 1]), *ffn(2), n_lat, m,
                  pool_scale=row(pool_scale[0]))

    ta, tb = _rope_tables(n_lat, n_ctx)
    qd = ATT_Q_HEADS * ATT_HEAD_DIM
    kd = ATT_KV_HEADS * ATT_HEAD_DIM
    wq = jnp.concatenate([att_w_qkv[0][:, :qd] * ATT_HEAD_DIM ** -0.5, att_w_qkv[0][:, qd:]], axis=1).astype(BF16)
    (qkv,) = _proj_call(
        _proj_attn_kernel, xs, row(norm_w[3, 0]), mods, 3, [wq, ta, tb], [_resident(wq.shape), _resident(ta.shape), _resident(tb.shape)],
        [sds(qd + 2 * kd, BF16)], n_lat, q_cols=qd, rope_cols=qd + kd)
    a = _attention(qkv, att_sink[0], n_lat, n_ctx)
    out = _mix_ffn(xs, a, att_w_out[0].astype(BF16), mods, 3, row(norm_w[3, 1]), *ffn(3), n_lat, n_lat,
                   final_w=row(final_norm_w))
    return out[None]
```
